```python
import math
import jax, jax.numpy as jnp
from jax import lax
import numpy as np

D_MODEL = 1024
BATCH = 8
SEQ = 4096
DEPTH = 1
DEC_BATCH = 32
DEC_SEQ = 8
PAST_LEN = 16384
PAGE_SIZE = 128

ML_HEADS = 4
ML_HEAD_DIM = 128
ML_WIDTH = ML_HEADS * ML_HEAD_DIM
CONV_W = 4
MLSTM_CHUNK = 64
DA_HEADS = 4
DA_HEAD_DIM = 64
DA_V_DIM = 2 * DA_HEAD_DIM
DA_WIDTH = DA_HEADS * DA_V_DIM
MIX_WIDTH = ML_WIDTH + DA_WIDTH
Q_BLOCK = 128
IN_SIZES = (ML_WIDTH, ML_WIDTH, ML_WIDTH, ML_HEADS, ML_HEADS,
            DA_HEADS * 2 * DA_HEAD_DIM, DA_HEADS * 2 * DA_HEAD_DIM, DA_WIDTH)
N_IN = 3 * ML_WIDTH + 2 * ML_HEADS + 4 * DA_HEADS * DA_HEAD_DIM + DA_WIDTH
N_GROUPS = 4
EXPERTS_PER_GROUP = 8
N_EXPERTS = N_GROUPS * EXPERTS_PER_GROUP
TOP_K_IN_GROUP = 2
D_EXPERT = D_MODEL // 2
ALPHA = (2 * DEPTH) ** 0.25
BETA = (8 * DEPTH) ** -0.25
LN_EPS = 1e-5

kernel_name = "hymba_mlstm_diffattn_hmoe_step"

F32 = jnp.float32


def _split_cols(p):
    out, o = [], 0
    for s in IN_SIZES:
        out.append(p[..., o:o + s])
        o += s
    return out


def _layernorm(x, g, b):
    xf = x.astype(F32)
    mu = xf.mean(-1, keepdims=True)
    var = jnp.square(xf - mu).mean(-1, keepdims=True)
    return ((xf - mu) * lax.rsqrt(var + LN_EPS) * g + b).astype(x.dtype)


def _head_layernorm(x, g):
    xf = x.astype(F32)
    mu = xf.mean(-1, keepdims=True)
    var = jnp.square(xf - mu).mean(-1, keepdims=True)
    return (xf - mu) * lax.rsqrt(var + LN_EPS) * g


def _head_rmsnorm(x, g):
    xf = x.astype(F32)
    return xf * lax.rsqrt(jnp.mean(jnp.square(xf), -1, keepdims=True) + LN_EPS) * g


def _short_conv(xm, buf, w, b):
    S = xm.shape[1]
    xp = jnp.concatenate([buf.astype(xm.dtype), xm], axis=1)
    y = b
    for j in range(CONV_W):
        y = y + w[j] * xp[:, j:j + S]
    return jax.nn.silu(y), xp[:, xp.shape[1] - (CONV_W - 1):]


def _mlstm_chunkwise(q, k, v, i_pre, f_pre, C0, n0, m0, chunk):
    B, S, H, DK = q.shape
    nc = S // chunk

    def to_chunks(t):
        return t.astype(F32).reshape(B, nc, chunk, H, -1).transpose(1, 0, 3, 2, 4)

    def gate_chunks(t):
        return t.reshape(B, nc, chunk, H).transpose(1, 0, 3, 2)

    qc = to_chunks(q) * (DK ** -0.5)
    kc = to_chunks(k)
    vc = to_chunks(v)
    ic = gate_chunks(i_pre.astype(F32))
    lfc = gate_chunks(jax.nn.log_sigmoid(f_pre.astype(F32)))
    causal = jnp.tril(jnp.ones((chunk, chunk), dtype=bool))

    def step(carry, inp):
        C, n, m = carry
        qb, kb, vb, ib, lfb = inp
        bcum = jnp.cumsum(lfb, axis=-1)
        logD = jnp.where(causal, bcum[..., :, None] - bcum[..., None, :] + ib[..., None, :], -jnp.inf)
        inter = bcum + m[..., None]
        m_t = jnp.maximum(inter, logD.max(-1))
        w_inter = jnp.exp(inter - m_t)
        s = jnp.einsum('bhld,bhsd->bhls', qb, kb) * jnp.exp(logD - m_t[..., None])
        num = w_inter[..., None] * jnp.einsum('bhld,bhde->bhle', qb, C) + jnp.einsum('bhls,bhse->bhle', s, vb)
        den = w_inter * jnp.einsum('bhld,bhd->bhl', qb, n) + s.sum(-1)
        h = num / jnp.maximum(jnp.abs(den), jnp.exp(-m_t))[..., None]
        m_new = m_t[..., -1]
        w_state = jnp.exp(bcum[..., -1] + m - m_new)
        w_k = jnp.exp(bcum[..., -1:] - bcum + ib - m_new[..., None])
        C_new = w_state[..., None, None] * C + jnp.einsum('bhs,bhsd,bhse->bhde', w_k, kb, vb)
        n_new = w_state[..., None] * n + jnp.einsum('bhs,bhsd->bhd', w_k, kb)
        return (C_new, n_new, m_new), h

    (C1, n1, m1), hs = lax.scan(step, (C0.astype(F32), n0.astype(F32), m0.astype(F32)),
                                (qc, kc, vc, ic, lfc))
    h = hs.transpose(1, 0, 3, 2, 4).reshape(B, S, H, -1)
    return h, C1, n1, m1


def _diff_weights(s1, s2, mask, lam):
    scale = DA_HEAD_DIM ** -0.5
    p1 = jax.nn.softmax(jnp.where(mask, s1.astype(F32) * scale, -jnp.inf), axis=-1)
    p2 = jax.nn.softmax(jnp.where(mask, s2.astype(F32) * scale, -jnp.inf), axis=-1)
    return p1 - lam * p2


def _diff_attn_prompt(q, k, v, lam):
    B, S, H, _ = q.shape
    nb = S // Q_BLOCK
    qb = q.reshape(B, nb, Q_BLOCK, H, 2 * DA_HEAD_DIM).transpose(1, 0, 2, 3, 4)
    k1, k2 = k[..., :DA_HEAD_DIM], k[..., DA_HEAD_DIM:]
    kpos = jnp.arange(S)

    def block(args):
        qi, bi = args
        qpos = bi * Q_BLOCK + jnp.arange(Q_BLOCK)
        mask = qpos[:, None] >= kpos[None, :]
        s1 = jnp.einsum('bqhd,bkhd->bhqk', qi[..., :DA_HEAD_DIM], k1)
        s2 = jnp.einsum('bqhd,bkhd->bhqk', qi[..., DA_HEAD_DIM:], k2)
        a = _diff_weights(s1, s2, mask, lam)
        return jnp.einsum('bhqk,bkhd->bqhd', a, v)

    out = lax.map(block, (qb, jnp.arange(nb)))
    return out.transpose(1, 0, 2, 3, 4).reshape(B, S, H, DA_V_DIM)


def _diff_attn_sample(q, k, v, k_past, v_past, lam):
    Sd = q.shape[1]
    P = k_past.shape[1]
    q1, q2 = q[..., :DA_HEAD_DIM], q[..., DA_HEAD_DIM:]
    s1 = jnp.concatenate([jnp.einsum('bqhd,bkhd->bhqk', q1, k_past[..., :DA_HEAD_DIM]),
                          jnp.einsum('bqhd,bkhd->bhqk', q1, k[..., :DA_HEAD_DIM])], axis=-1)
    s2 = jnp.concatenate([jnp.einsum('bqhd,bkhd->bhqk', q2, k_past[..., DA_HEAD_DIM:]),
                          jnp.einsum('bqhd,bkhd->bhqk', q2, k[..., DA_HEAD_DIM:])], axis=-1)
    mask = jnp.concatenate([jnp.ones((Sd, P), dtype=bool), jnp.tril(jnp.ones((Sd, Sd), dtype=bool))], axis=-1)
    a = _diff_weights(s1, s2, mask, lam)
    return (jnp.einsum('bhqk,bkhd->bqhd', a[..., :P], v_past)
            + jnp.einsum('bhqk,bkhd->bqhd', a[..., P:], v))


def _hmoe(h, p):
    B, S, D = h.shape
    hf = h.reshape(B * S, D)
    g_logits = (hf @ p["w_rg"] + p["b_rg"]).astype(F32)
    g_prob = jax.nn.softmax(g_logits, axis=-1)
    g_idx = jnp.argmax(g_logits, axis=-1)
    e_logits = (hf @ p["w_re"] + p["b_re"]).astype(F32).reshape(-1, N_GROUPS, EXPERTS_PER_GROUP)
    e_sel = jnp.take_along_axis(e_logits, g_idx[:, None, None], axis=1)[:, 0]
    top_v, top_i = lax.top_k(e_sel, TOP_K_IN_GROUP)
    gp = jnp.take_along_axis(g_prob, g_idx[:, None], axis=1)
    w_top = jax.nn.softmax(top_v, axis=-1) * gp
    eid = g_idx[:, None] * EXPERTS_PER_GROUP + top_i
    combine = jnp.sum(jax.nn.one_hot(eid, N_EXPERTS, dtype=F32) * w_top[..., None], axis=1)
    y = jnp.zeros(hf.shape, F32)
    for e in range(N_EXPERTS):
        a = jax.nn.silu(hf @ p["w_gate"][e]) * (hf @ p["w_up"][e])
        y = y + combine[:, e:e + 1] * (a @ p["w_down"][e])
    return y.reshape(B, S, D).astype(h.dtype)


def _layer(x, c, p, lam_init, conv_buf, C0, n0, m0, kv_past):
    B, S, _ = x.shape
    mod = jax.nn.silu(c) @ p["w_ada"] + p["b_ada"]
    sh_a, sc_a, g_a, sh_f, sc_f, g_f = jnp.split(mod[:, None, :], 6, axis=-1)
    h = x * (1 + sc_a) + sh_a
    xm, vm, om, ipre, fpre, qd, kd, vd = _split_cols(h @ p["w_in"])
    ca, conv_new = _short_conv(xm, conv_buf, p["w_conv"], p["b_conv"])
    cah = ca.reshape(B, S, ML_HEADS, ML_HEAD_DIM)
    qm = jnp.einsum('bshd,hde->bshe', cah, p["w_mq"])
    km = jnp.einsum('bshd,hde->bshe', cah, p["w_mk"])
    vmh = vm.reshape(B, S, ML_HEADS, ML_HEAD_DIM)
    chunk = MLSTM_CHUNK if S % MLSTM_CHUNK == 0 else S
    hc, C1, n1, m1 = _mlstm_chunkwise(qm, km, vmh, ipre + p["b_i"], fpre + p["b_f"], C0, n0, m0, chunk)
    hm = (_head_layernorm(hc, p["gn_m"]) + p["skip_m"] * cah) * jax.nn.sigmoid(
        om.reshape(B, S, ML_HEADS, ML_HEAD_DIM))
    qd = qd.reshape(B, S, DA_HEADS, 2 * DA_HEAD_DIM)
    kd = kd.reshape(B, S, DA_HEADS, 2 * DA_HEAD_DIM)
    vd = vd.reshape(B, S, DA_HEADS, DA_V_DIM)
    lam = (jnp.exp(jnp.sum(p["lam_q1"] * p["lam_k1"]).astype(F32))
           - jnp.exp(jnp.sum(p["lam_q2"] * p["lam_k2"]).astype(F32)) + lam_init)
    if kv_past is None:
        ad = _diff_attn_prompt(qd, kd, vd, lam)
    else:
        ad = _diff_attn_sample(qd, kd, vd, kv_past[0], kv_past[1], lam)
    ad = _head_rmsnorm(ad, p["subln_g"]) * (1.0 - lam_init)
    mixed = jnp.concatenate([hm.reshape(B, S, ML_WIDTH).astype(x.dtype),
                             ad.reshape(B, S, DA_WIDTH).astype(x.dtype)], axis=-1) @ p["w_out"]
    x = _layernorm(ALPHA * x + (1 + g_a) * mixed, p["ln1_g"], p["ln1_b"])
    h2 = x * (1 + sc_f) + sh_f
    x = _layernorm(ALPHA * x + (1 + g_f) * _hmoe(h2, p), p["ln2_g"], p["ln2_b"])
    return x, kd, vd, C1, n1, m1, conv_new


def setup_inputs(seed: int = 0) -> dict:
    key = jax.random.key(seed)
    ks = iter(jax.random.split(key, 48))
    nrm = lambda shape, s: jax.random.normal(next(ks), shape, F32) * s
    n_pages = PAST_LEN // PAGE_SIZE
    n_used = DEC_BATCH * n_pages
    n_pool = n_used + n_used // 4
    D = D_MODEL
    inp = {}
    inp["x_prompt"] = nrm((BATCH, SEQ, D), 1.0)
    inp["x_sample"] = nrm((DEC_BATCH, DEC_SEQ, D), 1.0)
    inp["cache_k"] = nrm((DEPTH, n_pool, PAGE_SIZE, DA_HEADS, 2 * DA_HEAD_DIM), 1.0)
    inp["cache_v"] = nrm((DEPTH, n_pool, PAGE_SIZE, DA_HEADS, DA_V_DIM), 1.0)
    inp["state_C"] = nrm((DEPTH, DEC_BATCH, ML_HEADS, ML_HEAD_DIM, ML_HEAD_DIM), 0.1)
    inp["state_n"] = nrm((DEPTH, DEC_BATCH, ML_HEADS, ML_HEAD_DIM), 0.5)
    inp["state_m"] = nrm((DEPTH, DEC_BATCH, ML_HEADS), 1.0)
    inp["state_conv"] = nrm((DEPTH, DEC_BATCH, CONV_W - 1, ML_WIDTH), 1.0)
    perm = jax.random.permutation(next(ks), n_pool)[:n_used]
    inp["page_table"] = perm.reshape(DEC_BATCH, n_pages).astype(jnp.int32)
    inp["c_prompt"] = nrm((BATCH, D), 1.0)
    inp["c_sample"] = nrm((DEC_BATCH, D), 1.0)
    inp["w_ada"] = nrm((DEPTH, D, 6 * D), 0.5 * D ** -0.5)
    inp["b_ada"] = nrm((DEPTH, 6 * D), 0.02)
    inp["w_in"] = nrm((DEPTH, D, N_IN), D ** -0.5)
    inp["w_conv"] = nrm((DEPTH, CONV_W, ML_WIDTH), CONV_W ** -0.5)
    inp["b_conv"] = nrm((DEPTH, ML_WIDTH), 0.02)
    inp["w_mq"] = nrm((DEPTH, ML_HEADS, ML_HEAD_DIM, ML_HEAD_DIM), ML_HEAD_DIM ** -0.5)
    inp["w_mk"] = nrm((DEPTH, ML_HEADS, ML_HEAD_DIM, ML_HEAD_DIM), ML_HEAD_DIM ** -0.5)
    inp["b_i"] = nrm((DEPTH, ML_HEADS), 0.1)
    inp["b_f"] = jnp.linspace(3.0, 6.0, ML_HEADS, dtype=F32)[None, :] + nrm((DEPTH, ML_HEADS), 0.1)
    inp["gn_m"] = 1.0 + nrm((DEPTH, ML_HEADS, ML_HEAD_DIM), 0.02)
    inp["skip_m"] = 1.0 + nrm((DEPTH, ML_HEADS, ML_HEAD_DIM), 0.02)
    inp["lam_q1"] = nrm((DEPTH, DA_HEAD_DIM), 0.1)
    inp["lam_k1"] = nrm((DEPTH, DA_HEAD_DIM), 0.1)
    inp["lam_q2"] = nrm((DEPTH, DA_HEAD_DIM), 0.1)
    inp["lam_k2"] = nrm((DEPTH, DA_HEAD_DIM), 0.1)
    inp["subln_g"] = 1.0 + nrm((DEPTH, DA_V_DIM), 0.02)
    inp["w_out"] = nrm((DEPTH, MIX_WIDTH, D), BETA * MIX_WIDTH ** -0.5)
    inp["ln1_g"] = 1.0 + nrm((DEPTH, D), 0.02)
    inp["ln1_b"] = nrm((DEPTH, D), 0.02)
    inp["w_rg"] = nrm((DEPTH, D, N_GROUPS), D ** -0.5)
    inp["b_rg"] = nrm((DEPTH, N_GROUPS), 0.01)
    inp["w_re"] = nrm((DEPTH, D, N_EXPERTS), D ** -0.5)
    inp["b_re"] = nrm((DEPTH, N_EXPERTS), 0.01)
    inp["w_gate"] = nrm((DEPTH, N_EXPERTS, D, D_EXPERT), D ** -0.5)
    inp["w_up"] = nrm((DEPTH, N_EXPERTS, D, D_EXPERT), D ** -0.5)
    inp["w_down"] = nrm((DEPTH, N_EXPERTS, D_EXPERT, D), BETA * D_EXPERT ** -0.5)
    inp["ln2_g"] = 1.0 + nrm((DEPTH, D), 0.02)
    inp["ln2_b"] = nrm((DEPTH, D), 0.02)
    return inp


def reference(x_prompt, x_sample, cache_k, cache_v, state_C, state_n, state_m, state_conv, page_table,
              c_prompt, c_sample, w_ada, b_ada, w_in, w_conv, b_conv, w_mq, w_mk, b_i, b_f, gn_m, skip_m,
              lam_q1, lam_k1, lam_q2, lam_k2, subln_g, w_out, ln1_g, ln1_b, w_rg, b_rg, w_re, b_re,
              w_gate, w_up, w_down, ln2_g, ln2_b):
    n_pages = page_table.shape[1]
    page = cache_k.shape[2]
    Bp = x_prompt.shape[0]
    Bs = x_sample.shape[0]
    yp, ys = x_prompt, x_sample
    kp_l, vp_l, Cp_l, np_l, mp_l, cvp_l = [], [], [], [], [], []
    ks_l, vs_l, Cs_l, ns_l, ms_l, cvs_l = [], [], [], [], [], []
    for l in range(DEPTH):
        p = {"w_ada": w_ada[l], "b_ada": b_ada[l], "w_in": w_in[l], "w_conv": w_conv[l], "b_conv": b_conv[l],
             "w_mq": w_mq[l], "w_mk": w_mk[l], "b_i": b_i[l], "b_f": b_f[l], "gn_m": gn_m[l],
             "skip_m": skip_m[l], "lam_q1": lam_q1[l], "lam_k1": lam_k1[l], "lam_q2": lam_q2[l],
             "lam_k2": lam_k2[l], "subln_g": subln_g[l], "w_out": w_out[l], "ln1_g": ln1_g[l],
             "ln1_b": ln1_b[l], "w_rg": w_rg[l], "b_rg": b_rg[l], "w_re": w_re[l], "b_re": b_re[l],
             "w_gate": w_gate[l], "w_up": w_up[l], "w_down": w_down[l], "ln2_g": ln2_g[l], "ln2_b": ln2_b[l]}
        lam_init = 0.8 - 0.6 * math.exp(-0.3 * l)
        yp, kp, vp, Cp, np_, mp, cvp = _layer(
            yp, c_prompt, p, lam_init,
            jnp.zeros((Bp, CONV_W - 1, ML_WIDTH), yp.dtype),
            jnp.zeros((Bp, ML_HEADS, ML_HEAD_DIM, ML_HEAD_DIM), F32),
            jnp.zeros((Bp, ML_HEADS, ML_HEAD_DIM), F32),
            jnp.zeros((Bp, ML_HEADS), F32), None)
        k_past = cache_k[l, page_table].reshape(Bs, n_pages * page, DA_HEADS, 2 * DA_HEAD_DIM)
        v_past = cache_v[l, page_table].reshape(Bs, n_pages * page, DA_HEADS, DA_V_DIM)
        ys, kS, vS, Cs, ns, ms, cvs = _layer(
            ys, c_sample, p, lam_init, state_conv[l], state_C[l], state_n[l], state_m[l], (k_past, v_past))
        kp_l.append(kp); vp_l.append(vp)
        Cp_l.append(Cp.astype(x_prompt.dtype)); np_l.append(np_.astype(x_prompt.dtype))
        mp_l.append(mp.astype(x_prompt.dtype)); cvp_l.append(cvp)
        ks_l.append(kS.astype(cache_k.dtype)); vs_l.append(vS.astype(cache_v.dtype))
        Cs_l.append(Cs.astype(state_C.dtype)); ns_l.append(ns.astype(state_n.dtype))
        ms_l.append(ms.astype(state_m.dtype)); cvs_l.append(cvs.astype(state_conv.dtype))
    return (yp, ys,
            jnp.stack(kp_l), jnp.stack(vp_l), jnp.stack(Cp_l), jnp.stack(np_l), jnp.stack(mp_l), jnp.stack(cvp_l),
            jnp.stack(ks_l), jnp.stack(vs_l), jnp.stack(Cs_l), jnp.stack(ns_l), jnp.stack(ms_l), jnp.stack(cvs_l))
```

```python
import functools
import math

import jax
import jax.numpy as jnp
from jax import lax
from jax.experimental import pallas as pl
from jax.experimental.pallas import tpu as pltpu

F32 = jnp.float32
BF16 = jnp.bfloat16
I32 = jnp.int32
HIGHEST = lax.Precision.HIGHEST

LN_EPS = 1e-5
ML_HEADS = 4
ML_HEAD_DIM = 128
ML_WIDTH = ML_HEADS * ML_HEAD_DIM
CONV_W = 4
DA_HEADS = 4
DA_HEAD_DIM = 64
DA_V_DIM = 2 * DA_HEAD_DIM
DA_WIDTH = DA_HEADS * DA_V_DIM
N_GROUPS = 4
EXPERTS_PER_GROUP = 8
N_EXPERTS = N_GROUPS * EXPERTS_PER_GROUP
GATE_LANES = 128
ROUTER_ROWS = 8 + N_EXPERTS
SUBLANES = 8
LANES = 128
ROW_TILE = 256
VMEM_LIMIT = 56 * 2 ** 20

_NT = (((1,), (1,)), ((), ()))


def _params(sem, vmem=None):
    return pltpu.CompilerParams(dimension_semantics=sem, vmem_limit_bytes=vmem)


def _sigmoid(x):
    return jax.nn.sigmoid(x)


def _log_sigmoid(x):
    return jnp.minimum(x, 0.0) - jnp.log1p(jnp.exp(-jnp.abs(x)))


def _layernorm_rows(y, g, b):
    mu = jnp.mean(y, axis=-1, keepdims=True)
    d = y - mu
    var = jnp.mean(d * d, axis=-1, keepdims=True)
    return d * lax.rsqrt(var + LN_EPS) * g + b


def _ada_kernel(c_ref, w_ref, b_ref, o_ref):
    c = c_ref[...]
    s = c * _sigmoid(c)
    o_ref[...] = jnp.dot(s, w_ref[...], preferred_element_type=F32, precision=HIGHEST) + b_ref[...]


def _ada(c, w, b):
    bc, d = c.shape
    n = w.shape[1]
    tn = 512
    return pl.pallas_call(
        _ada_kernel,
        grid=(n // tn,),
        in_specs=[pl.BlockSpec((bc, d), lambda j: (0, 0)),
                  pl.BlockSpec((d, tn), lambda j: (0, j)),
                  pl.BlockSpec((1, tn), lambda j: (0, j))],
        out_specs=pl.BlockSpec((bc, tn), lambda j: (0, j)),
        out_shape=jax.ShapeDtypeStruct((bc, n), F32),
        compiler_params=_params(("arbitrary",)),
    )(c, w, b)


def _proj_kernel(x_ref, sc_ref, sh_ref, wa_ref, wg_ref, wb_ref, bg_ref, cbuf_ref, wconv_ref, bconv_ref, wqk_ref,
                 ca_ref, q_ref, k_ref, v_ref, om_ref, g_ref, qd_ref, kd_ref, vd_ref, cnew_ref, ext_ref,
                 *, bb, ts):
    si = pl.program_id(1)
    m = bb * ts
    d = x_ref.shape[-1]
    h = (x_ref[...] * (1.0 + sc_ref[...]) + sh_ref[...]).reshape(m, d).astype(BF16)
    pa = jnp.dot(h, wa_ref[...], preferred_element_type=F32)
    pb = jnp.dot(h, wb_ref[...], preferred_element_type=F32)
    g = jnp.dot(h, wg_ref[...], preferred_element_type=F32) + bg_ref[...]
    g_ref[...] = g.reshape(bb, ts, GATE_LANES)
    w = ML_WIDTH
    v_ref[...] = pa[:, w:2 * w].reshape(bb, ts, w).astype(v_ref.dtype)
    om_ref[...] = pa[:, 2 * w:3 * w].reshape(bb, ts, w)
    qd_ref[...] = pb[:, 0:w].reshape(bb, ts, w).astype(qd_ref.dtype)
    kd_ref[...] = pb[:, w:2 * w].reshape(bb, ts, w)
    vd_ref[...] = pb[:, 2 * w:3 * w].reshape(bb, ts, w)

    @pl.when(si == 0)
    def _():
        ext_ref[:, 5:8, :] = cbuf_ref[...]

    @pl.when(si > 0)
    def _():
        ext_ref[:, 0:8, :] = ext_ref[:, ts:ts + 8, :]

    ext_ref[:, 8:8 + ts, :] = pa[:, 0:w].reshape(bb, ts, w)
    y = bconv_ref[...]
    for j in range(CONV_W):
        y = y + wconv_ref[j:j + 1, :] * ext_ref[:, 5 + j:5 + j + ts, :]
    ca = y * _sigmoid(y)
    ca_ref[...] = ca
    cnew_ref[...] = ext_ref[:, ts + 5:ts + 8, :]

    ca2 = ca.reshape(m, w)
    hd = ML_HEAD_DIM
    for hh in range(ML_HEADS):
        qk = jnp.dot(ca2[:, hh * hd:(hh + 1) * hd].astype(BF16), wqk_ref[hh], preferred_element_type=F32)
        q_ref[:, :, hh * hd:(hh + 1) * hd] = (qk[:, 0:hd] * (hd ** -0.5)).reshape(bb, ts, hd).astype(q_ref.dtype)
        k_ref[:, :, hh * hd:(hh + 1) * hd] = qk[:, hd:2 * hd].reshape(bb, ts, hd).astype(k_ref.dtype)


def _proj(x, sc, sh, wa, wg, wb, bg, cbuf, wconv, bconv, wqk, *, bb, ts, act):
    b, s, d = x.shape
    w = ML_WIDTH
    grid = (b // bb, s // ts)
    tok = lambda n: pl.BlockSpec((bb, ts, n), lambda i, j: (i, j, 0))
    per_b = lambda r, n: pl.BlockSpec((bb, r, n), lambda i, j: (i, 0, 0))
    const = lambda shape: pl.BlockSpec(shape, lambda i, j: (0,) * len(shape))
    out_shape = (
        jax.ShapeDtypeStruct((b, s, w), F32),
        jax.ShapeDtypeStruct((b, s, w), act),
        jax.ShapeDtypeStruct((b, s, w), act),
        jax.ShapeDtypeStruct((b, s, w), act),
        jax.ShapeDtypeStruct((b, s, w), F32),
        jax.ShapeDtypeStruct((b, s, GATE_LANES), F32),
        jax.ShapeDtypeStruct((b, s, w), act),
        jax.ShapeDtypeStruct((b, s, w), F32),
        jax.ShapeDtypeStruct((b, s, w), F32),
        jax.ShapeDtypeStruct((b, CONV_W - 1, w), F32),
    )
    out_specs = (tok(w), tok(w), tok(w), tok(w), tok(w), tok(GATE_LANES), tok(w), tok(w), tok(w),
                 per_b(CONV_W - 1, w))
    return pl.pallas_call(
        functools.partial(_proj_kernel, bb=bb, ts=ts),
        grid=grid,
        in_specs=[tok(d), per_b(1, d), per_b(1, d), const(wa.shape), const(wg.shape), const(wb.shape),
                  const(bg.shape), per_b(CONV_W - 1, w), const(wconv.shape), const(bconv.shape),
                  const(wqk.shape)],
        out_specs=out_specs,
        out_shape=out_shape,
        scratch_shapes=[pltpu.VMEM((bb, ts + 8, w), F32)],
        compiler_params=_params(("arbitrary", "arbitrary"), VMEM_LIMIT),
    )(x, sc, sh, wa, wg, wb, bg, cbuf, wconv, bconv, wqk)


def _mlstm_kernel(q_ref, k_ref, v_ref, g_ref, ca_ref, om_ref, c0_ref, n0_ref, m0_ref, gn_ref, skip_ref,
                  hm_ref, c1_ref, n1_ref, m1_ref, c_s, n_s, m_s, *, chunk):
    si = pl.program_id(1)
    ln = chunk
    hd = ML_HEAD_DIM

    @pl.when(si == 0)
    def _():
        c_s[...] = c0_ref[0]
        n_s[...] = n0_ref[0]
        m_s[...] = m0_ref[0]

    g = g_ref[0]
    row = lax.broadcasted_iota(I32, (ln, GATE_LANES), 0)
    bc = _log_sigmoid(g)
    shift = 1
    while shift < ln:
        bc = bc + jnp.where(row >= shift, pltpu.roll(bc, shift, axis=0), 0.0)
        shift *= 2
    g_t = g.T
    bc_t = bc.T
    causal = lax.broadcasted_iota(I32, (ln, ln), 0) >= lax.broadcasted_iota(I32, (ln, ln), 1)

    for hh in range(ML_HEADS):
        cols = slice(hh * hd, (hh + 1) * hd)
        qb = q_ref[0, :, cols].astype(BF16)
        kf = k_ref[0, :, cols].astype(F32)
        kb = kf.astype(BF16)
        vb = v_ref[0, :, cols].astype(BF16)
        b_col = bc[:, ML_HEADS + hh:ML_HEADS + hh + 1]
        i_col = g[:, hh:hh + 1]
        b_row = bc_t[ML_HEADS + hh:ML_HEADS + hh + 1, :]
        i_row = g_t[hh:hh + 1, :]
        m_prev = m_s[:, hh:hh + 1]
        log_d = jnp.where(causal, b_col - b_row + i_row, -jnp.inf)
        inter = b_col + m_prev
        m_t = jnp.maximum(inter, jnp.max(log_d, axis=-1, keepdims=True))
        w_inter = jnp.exp(inter - m_t)
        s = lax.dot_general(qb, kb, _NT, preferred_element_type=F32) * jnp.exp(log_d - m_t)
        c_old = c_s[hh]
        n_old = n_s[hh:hh + 1, :]
        num = (w_inter * jnp.dot(qb, c_old.astype(BF16), preferred_element_type=F32)
               + jnp.dot(s.astype(BF16), vb, preferred_element_type=F32))
        den = (w_inter * jnp.sum(qb.astype(F32) * n_old, axis=-1, keepdims=True)
               + jnp.sum(s, axis=-1, keepdims=True))
        hc = num / jnp.maximum(jnp.abs(den), jnp.exp(-m_t))
        m_new = m_t[ln - 1:ln, :]
        b_last = b_col[ln - 1:ln, :]
        w_state = jnp.exp(b_last + m_prev - m_new)
        kw = jnp.exp(b_last - b_col + i_col - m_new) * kf
        c_s[hh] = w_state * c_old + jnp.dot(kw.T.astype(BF16), vb, preferred_element_type=F32)
        n_s[hh:hh + 1, :] = w_state * n_old + jnp.sum(kw, axis=0, keepdims=True)
        m_s[:, hh:hh + 1] = m_new
        mu = jnp.mean(hc, axis=-1, keepdims=True)
        dlt = hc - mu
        var = jnp.mean(dlt * dlt, axis=-1, keepdims=True)
        hn = dlt * lax.rsqrt(var + LN_EPS) * gn_ref[hh:hh + 1, :]
        out = (hn + skip_ref[hh:hh + 1, :] * ca_ref[0, :, cols]) * _sigmoid(om_ref[0, :, cols])
        hm_ref[0, :, cols] = out.astype(hm_ref.dtype)

    @pl.when(si == pl.num_programs(1) - 1)
    def _():
        c1_ref[0] = c_s[...]
        n1_ref[0] = n_s[...]
        m1_ref[0] = m_s[...]


def _mlstm(q, k, v, g, ca, om, c0, n0, m0, gn, skip, *, chunk, act):
    b, s, w = q.shape
    h, hd = ML_HEADS, ML_HEAD_DIM
    tok = lambda n: pl.BlockSpec((1, chunk, n), lambda i, j: (i, j, 0))
    c_spec = pl.BlockSpec((1, h, hd, hd), lambda i, j: (i, 0, 0, 0))
    n_spec = pl.BlockSpec((1, h, hd), lambda i, j: (i, 0, 0))
    m_spec = pl.BlockSpec((1, 1, h), lambda i, j: (i, 0, 0))
    hw_spec = pl.BlockSpec((h, hd), lambda i, j: (0, 0))
    return pl.pallas_call(
        functools.partial(_mlstm_kernel, chunk=chunk),
        grid=(b, s // chunk),
        in_specs=[tok(w), tok(w), tok(w), tok(GATE_LANES), tok(w), tok(w), c_spec, n_spec, m_spec,
                  hw_spec, hw_spec],
        out_specs=(tok(w), c_spec, n_spec, m_spec),
        out_shape=(jax.ShapeDtypeStruct((b, s, w), act),
                   jax.ShapeDtypeStruct((b, h, hd, hd), F32),
                   jax.ShapeDtypeStruct((b, h, hd), F32),
                   jax.ShapeDtypeStruct((b, 1, h), F32)),
        scratch_shapes=[pltpu.VMEM((h, hd, hd), F32), pltpu.VMEM((h, hd), F32), pltpu.VMEM((1, h), F32)],
        compiler_params=_params(("arbitrary", "arbitrary"), VMEM_LIMIT),
    )(q, k, v, g, ca, om, c0, n0, m0, gn, skip)


def _lam(lq1_ref, lk1_ref, lq2_ref, lk2_ref, lam_init):
    a = jnp.sum(lq1_ref[...] * lk1_ref[...], axis=-1, keepdims=True)
    b = jnp.sum(lq2_ref[...] * lk2_ref[...], axis=-1, keepdims=True)
    return jnp.exp(a) - jnp.exp(b) + lam_init


def _head_rms(o, sg, lam_init):
    return o * lax.rsqrt(jnp.mean(o * o, axis=-1, keepdims=True) + LN_EPS) * sg * (1.0 - lam_init)


def _softmax_update(s, vt, m_ref, l_ref, a_ref):
    m_old = m_ref[...]
    m_new = jnp.maximum(m_old, jnp.max(s, axis=-1, keepdims=True))
    alpha = jnp.exp(m_old - m_new)
    p = jnp.exp(s - m_new)
    l_ref[...] = alpha * l_ref[...] + jnp.sum(p, axis=-1, keepdims=True)
    a_ref[...] = alpha * a_ref[...] + jnp.dot(p.astype(BF16), vt, preferred_element_type=F32)
    m_ref[...] = m_new


def _dattn_kernel(lq1_ref, lk1_ref, lq2_ref, lk2_ref, sg_ref, q_ref, k_ref, v_ref, o_ref,
                  m1, l1, a1, m2, l2, a2, *, tq, lam_init):
    qi = pl.program_id(2)
    lam = _lam(lq1_ref, lk1_ref, lq2_ref, lk2_ref, lam_init)
    q = q_ref[0].astype(F32) * (DA_HEAD_DIM ** -0.5)
    lane = lax.broadcasted_iota(I32, q.shape, 1)
    q1 = jnp.where(lane < DA_HEAD_DIM, q, 0.0).astype(BF16)
    q2 = jnp.where(lane >= DA_HEAD_DIM, q, 0.0).astype(BF16)
    for m_ref, l_ref, a_ref in ((m1, l1, a1), (m2, l2, a2)):
        m_ref[...] = jnp.full(m_ref.shape, -jnp.inf, F32)
        l_ref[...] = jnp.zeros(l_ref.shape, F32)
        a_ref[...] = jnp.zeros(a_ref.shape, F32)
    on_or_below = (lax.broadcasted_iota(I32, (tq, tq), 0) >= lax.broadcasted_iota(I32, (tq, tq), 1))

    def kv_step(j, masked):
        start = pl.multiple_of(j * tq, tq)
        kt = k_ref[0, pl.ds(start, tq), :].astype(BF16)
        vt = v_ref[0, pl.ds(start, tq), :].astype(BF16)
        for qz, m_ref, l_ref, a_ref in ((q1, m1, l1, a1), (q2, m2, l2, a2)):
            s = lax.dot_general(qz, kt, _NT, preferred_element_type=F32)
            if masked:
                s = jnp.where(on_or_below, s, -jnp.inf)
            _softmax_update(s, vt, m_ref, l_ref, a_ref)

    def body(j, carry):
        kv_step(j, False)
        return carry

    lax.fori_loop(0, qi, body, 0)
    kv_step(qi, True)
    o = a1[...] / l1[...] - lam * (a2[...] / l2[...])
    o_ref[0] = _head_rms(o, sg_ref[...], lam_init).astype(o_ref.dtype)


def _dattn_prompt(qd, kd, vd, lams, sg, *, tq, lam_init, act):
    b, s, w = qd.shape
    dv = DA_V_DIM
    lam_spec = pl.BlockSpec((1, DA_HEAD_DIM), lambda i, h, j: (0, 0))
    return pl.pallas_call(
        functools.partial(_dattn_kernel, tq=tq, lam_init=lam_init),
        grid=(b, DA_HEADS, s // tq),
        in_specs=[lam_spec, lam_spec, lam_spec, lam_spec,
                  pl.BlockSpec((1, dv), lambda i, h, j: (0, 0)),
                  pl.BlockSpec((1, tq, dv), lambda i, h, j: (i, j, h)),
                  pl.BlockSpec((1, s, dv), lambda i, h, j: (i, 0, h)),
                  pl.BlockSpec((1, s, dv), lambda i, h, j: (i, 0, h))],
        out_specs=pl.BlockSpec((1, tq, dv), lambda i, h, j: (i, j, h)),
        out_shape=jax.ShapeDtypeStruct((b, s, w), act),
        scratch_shapes=[pltpu.VMEM((tq, 1), F32), pltpu.VMEM((tq, 1), F32), pltpu.VMEM((tq, dv), F32),
                        pltpu.VMEM((tq, 1), F32), pltpu.VMEM((tq, 1), F32), pltpu.VMEM((tq, dv), F32)],
        compiler_params=_params(("arbitrary", "arbitrary", "arbitrary"), VMEM_LIMIT),
    )(*lams, sg, qd, kd, vd)


def _sattn_kernel(pt_ref, lq1_ref, lk1_ref, lq2_ref, lk2_ref, sg_ref, q_ref, kn_ref, vn_ref, *rest,
                  pps, sd, lam_init):
    k_refs = rest[0:pps]
    v_refs = rest[pps:2 * pps]
    o_ref, qbd_s, m_s, l_s, a_s = rest[2 * pps:]
    j = pl.program_id(1)
    nrow = 2 * DA_HEADS * sd
    dv = DA_V_DIM

    @pl.when(j == 0)
    def _():
        q = q_ref[0].astype(F32) * (DA_HEAD_DIM ** -0.5)
        qt = jnp.concatenate([q] * (2 * DA_HEADS), axis=0)
        r = lax.broadcasted_iota(I32, qt.shape, 0)
        c = lax.broadcasted_iota(I32, qt.shape, 1)
        same_head = ((r // sd) % DA_HEADS) == (c // dv)
        same_map = (r // (DA_HEADS * sd)) == ((c // DA_HEAD_DIM) % 2)
        qbd_s[...] = jnp.where(same_head & same_map, qt, 0.0).astype(BF16)
        m_s[...] = jnp.full(m_s.shape, -jnp.inf, F32)
        l_s[...] = jnp.zeros(l_s.shape, F32)
        a_s[...] = jnp.zeros(a_s.shape, F32)

    qbd = qbd_s[...]
    s = jnp.concatenate(
        [lax.dot_general(qbd, k_refs[p][0].astype(BF16), _NT, preferred_element_type=F32) for p in range(pps)],
        axis=1)
    page = k_refs[0].shape[1]
    m_old = m_s[...]
    m_new = jnp.maximum(m_old, jnp.max(s, axis=-1, keepdims=True))
    alpha = jnp.exp(m_old - m_new)
    p_all = jnp.exp(s - m_new)
    l_s[...] = alpha * l_s[...] + jnp.sum(p_all, axis=-1, keepdims=True)
    pv = jnp.dot(p_all[:, 0:page].astype(BF16), v_refs[0][0].astype(BF16), preferred_element_type=F32)
    for p in range(1, pps):
        pv = pv + jnp.dot(p_all[:, p * page:(p + 1) * page].astype(BF16), v_refs[p][0].astype(BF16),
                          preferred_element_type=F32)
    a_s[...] = alpha * a_s[...] + pv
    m_s[...] = m_new

    @pl.when(j == pl.num_programs(1) - 1)
    def _():
        pad = jnp.zeros((LANES - sd, kn_ref.shape[-1]), F32)
        kn = jnp.concatenate([kn_ref[0], pad], axis=0).astype(BF16)
        vn = jnp.concatenate([vn_ref[0], pad], axis=0).astype(BF16)
        sn = lax.dot_general(qbd, kn, _NT, preferred_element_type=F32)
        r = lax.broadcasted_iota(I32, sn.shape, 0)
        c = lax.broadcasted_iota(I32, sn.shape, 1)
        sn = jnp.where((c < sd) & (c <= (r % sd)), sn, -jnp.inf)
        _softmax_update(sn, vn, m_s, l_s, a_s)
        lam = _lam(lq1_ref, lk1_ref, lq2_ref, lk2_ref, lam_init)
        acc = a_s[...]
        lsum = l_s[...]
        half = DA_HEADS * sd
        for hh in range(DA_HEADS):
            rows1 = slice(hh * sd, (hh + 1) * sd)
            rows2 = slice(half + hh * sd, half + (hh + 1) * sd)
            cols = slice(hh * dv, (hh + 1) * dv)
            o = acc[rows1, cols] / lsum[rows1, :] - lam * (acc[rows2, cols] / lsum[rows2, :])
            o_ref[0, :, cols] = _head_rms(o, sg_ref[...], lam_init).astype(o_ref.dtype)


def _dattn_sample(qd, kn, vn, cache_k, cache_v, page_table, lams, sg, *, pps, lam_init):
    b, sd, w = qd.shape
    n_pages = page_table.shape[1]
    page = cache_k.shape[1]
    nrow = 2 * DA_HEADS * sd
    lam_spec = pl.BlockSpec((1, DA_HEAD_DIM), lambda i, j, pt: (0, 0))
    tok = pl.BlockSpec((1, sd, w), lambda i, j, pt: (i, 0, 0))

    def page_spec(p):
        return pl.BlockSpec((1, page, w), lambda i, j, pt: (pt[i, j * pps + p], 0, 0))

    grid_spec = pltpu.PrefetchScalarGridSpec(
        num_scalar_prefetch=1,
        grid=(b, n_pages // pps),
        in_specs=[lam_spec, lam_spec, lam_spec, lam_spec,
                  pl.BlockSpec((1, DA_V_DIM), lambda i, j, pt: (0, 0)),
                  tok, tok, tok]
                 + [page_spec(p) for p in range(pps)] + [page_spec(p) for p in range(pps)],
        out_specs=tok,
        scratch_shapes=[pltpu.VMEM((nrow, w), BF16), pltpu.VMEM((nrow, 1), F32), pltpu.VMEM((nrow, 1), F32),
                        pltpu.VMEM((nrow, w), F32)],
    )
    return pl.pallas_call(
        functools.partial(_sattn_kernel, pps=pps, sd=sd, lam_init=lam_init),
        grid_spec=grid_spec,
        out_shape=jax.ShapeDtypeStruct((b, sd, w), F32),
        compiler_params=_params(("arbitrary", "arbitrary"), VMEM_LIMIT),
    )(page_table, *lams, sg, qd, kn, vn, *([cache_k] * pps), *([cache_v] * pps))


def _mix_kernel(hm_ref, ad_ref, x_ref, ga_ref, scf_ref, shf_ref, wo1_ref, wo2_ref, g1_ref, b1_ref,
                wrt_ref, brt_ref, cin_ref, x1_ref, h2_ref, ri_ref, rw_ref, cnt_ref, carry_s,
                *, bb, ts, alpha):
    m = bb * ts
    d = x_ref.shape[-1]

    @pl.when((pl.program_id(0) == 0) & (pl.program_id(1) == 0))
    def _():
        carry_s[...] = cin_ref[...]

    hm = hm_ref[...].reshape(m, ML_WIDTH).astype(BF16)
    ad = ad_ref[...].reshape(m, DA_WIDTH).astype(BF16)
    mixed = (jnp.dot(hm, wo1_ref[...], preferred_element_type=F32)
             + jnp.dot(ad, wo2_ref[...], preferred_element_type=F32))
    y = alpha * x_ref[...] + (1.0 + ga_ref[...]) * mixed.reshape(bb, ts, d)
    x1 = _layernorm_rows(y, g1_ref[...], b1_ref[...])
    x1_ref[...] = x1
    h2 = (x1 * (1.0 + scf_ref[...]) + shf_ref[...]).reshape(m, d)
    h2_ref[...] = h2

    lt = lax.dot_general(wrt_ref[...], h2, _NT, preferred_element_type=F32, precision=HIGHEST) + brt_ref[...]
    gl = lt[0:N_GROUPS]
    gmax = jnp.max(gl, axis=0, keepdims=True)
    r4 = lax.broadcasted_iota(I32, gl.shape, 0)
    gidx = jnp.min(jnp.where(gl == gmax, r4, N_GROUPS), axis=0, keepdims=True)
    gp = 1.0 / jnp.sum(jnp.exp(gl - gmax), axis=0, keepdims=True)
    epg = EXPERTS_PER_GROUP
    esel = lt[8 + (N_GROUPS - 1) * epg:8 + N_GROUPS * epg]
    for grp in range(N_GROUPS - 2, -1, -1):
        esel = jnp.where(gidx == grp, lt[8 + grp * epg:8 + (grp + 1) * epg], esel)
    r8 = lax.broadcasted_iota(I32, esel.shape, 0)
    t1 = jnp.max(esel, axis=0, keepdims=True)
    i1 = jnp.min(jnp.where(esel == t1, r8, epg), axis=0, keepdims=True)
    rest = jnp.where(r8 == i1, -jnp.inf, esel)
    t2 = jnp.max(rest, axis=0, keepdims=True)
    i2 = jnp.min(jnp.where(rest == t2, r8, epg), axis=0, keepdims=True)
    z = jnp.exp(t2 - t1)
    w1 = gp / (1.0 + z)
    w2 = gp * z / (1.0 + z)
    e0 = gidx * epg + i1
    e1 = gidx * epg + i2

    r32 = lax.broadcasted_iota(I32, (N_EXPERTS, m), 0)
    hit0 = r32 == e0
    hit1 = r32 == e1
    onehot = jnp.where(hit0, 1.0, jnp.where(hit1, 1.0, 0.0))
    before = (lax.broadcasted_iota(I32, (m, m), 0) < lax.broadcasted_iota(I32, (m, m), 1))
    prefix = jnp.dot(onehot.astype(BF16), jnp.where(before, 1.0, 0.0).astype(BF16),
                     preferred_element_type=F32) + carry_s[:, 0:1]
    rank0 = jnp.sum(jnp.where(hit0, prefix, 0.0), axis=0, keepdims=True).astype(I32)
    rank1 = jnp.sum(jnp.where(hit1, prefix, 0.0), axis=0, keepdims=True).astype(I32)
    carry_s[...] = carry_s[...] + jnp.sum(onehot, axis=1, keepdims=True)
    cnt_ref[...] = carry_s[...]
    rr = lax.broadcasted_iota(I32, (SUBLANES, m), 0)
    ri_ref[...] = jnp.where(rr == 0, e0, jnp.where(rr == 1, e1, jnp.where(rr == 2, rank0,
                            jnp.where(rr == 3, rank1, 0))))
    rw_ref[...] = jnp.where(rr == 0, w1, jnp.where(rr == 1, w2, 0.0))


def _mix(hm, ad, x, ga, scf, shf, wo1, wo2, g1, b1, wrt, brt, cin, *, bb, ts, alpha):
    b, s, d = x.shape
    m = bb * ts
    t = b * s
    ns = s // ts
    tok = lambda n: pl.BlockSpec((bb, ts, n), lambda i, j: (i, j, 0))
    per_b = pl.BlockSpec((bb, 1, d), lambda i, j: (i, 0, 0))
    const = lambda shape: pl.BlockSpec(shape, lambda i, j: (0,) * len(shape))
    lin = pl.BlockSpec((SUBLANES, m), lambda i, j: (0, i * ns + j))
    return pl.pallas_call(
        functools.partial(_mix_kernel, bb=bb, ts=ts, alpha=alpha),
        grid=(b // bb, ns),
        in_specs=[tok(ML_WIDTH), tok(DA_WIDTH), tok(d), per_b, per_b, per_b, const(wo1.shape), const(wo2.shape),
                  const(g1.shape), const(b1.shape), const(wrt.shape), const(brt.shape), const(cin.shape)],
        out_specs=(tok(d), pl.BlockSpec((m, d), lambda i, j: (i * ns + j, 0)), lin, lin, const(cin.shape)),
        out_shape=(jax.ShapeDtypeStruct((b, s, d), F32),
                   jax.ShapeDtypeStruct((t, d), F32),
                   jax.ShapeDtypeStruct((SUBLANES, t), I32),
                   jax.ShapeDtypeStruct((SUBLANES, t), F32),
                   jax.ShapeDtypeStruct(cin.shape, F32)),
        scratch_shapes=[pltpu.VMEM(cin.shape, F32)],
        compiler_params=_params(("arbitrary", "arbitrary"), VMEM_LIMIT),
    )(hm, ad, x, ga, scf, shf, wo1, wo2, g1, b1, wrt, brt, cin)


def _plan_kernel(ri_ref, cnt_ref, pos_ref, tile_ref, *, row_tile):
    cnt = cnt_ref[...]
    padded = jnp.floor((cnt + (row_tile - 1)) * (1.0 / row_tile)) * row_tile
    row = lax.broadcasted_iota(I32, cnt.shape, 0)
    ends = padded
    shift = 1
    while shift < N_EXPERTS:
        ends = ends + jnp.where(row >= shift, pltpu.roll(ends, shift, axis=0), 0.0)
        shift *= 2
    offs = (ends - padded)[:, 0:1]
    tp = ri_ref.shape[1]
    r32 = lax.broadcasted_iota(I32, (N_EXPERTS, tp), 0)
    pos0 = jnp.sum(jnp.where(r32 == ri_ref[0:1, :], offs, 0.0), axis=0, keepdims=True).astype(I32) + ri_ref[2:3, :]
    pos1 = jnp.sum(jnp.where(r32 == ri_ref[1:2, :], offs, 0.0), axis=0, keepdims=True).astype(I32) + ri_ref[3:4, :]
    rr = lax.broadcasted_iota(I32, (SUBLANES, tp), 0)
    pos_ref[...] = jnp.where(rr == 0, pos0, jnp.where(rr == 1, pos1, 0))
    nt = tile_ref.shape[1]
    first_row = (lax.broadcasted_iota(I32, (N_EXPERTS, nt), 1) * row_tile).astype(F32)
    done = jnp.sum(jnp.where(ends[:, 0:1] <= first_row, 1, 0), axis=0, keepdims=True)
    expert = jnp.minimum(done, N_EXPERTS - 1)
    used = (ends[N_EXPERTS - 1:N_EXPERTS, 0:1] * (1.0 / row_tile)).astype(I32)
    rt = lax.broadcasted_iota(I32, (SUBLANES, nt), 0)
    tile_ref[...] = jnp.where(rt == 0, expert, jnp.where(rt == 1, used, 0))


def _plan(ri, cnt, *, row_tile, n_tiles_pad):
    t = ri.shape[1]
    tp = min(t, 4096)
    return pl.pallas_call(
        functools.partial(_plan_kernel, row_tile=row_tile),
        grid=(t // tp,),
        in_specs=[pl.BlockSpec((SUBLANES, tp), lambda i: (0, i)), pl.BlockSpec(cnt.shape, lambda i: (0, 0))],
        out_specs=(pl.BlockSpec((SUBLANES, tp), lambda i: (0, i)),
                   pl.BlockSpec((SUBLANES, n_tiles_pad), lambda i: (0, 0))),
        out_shape=(jax.ShapeDtypeStruct((SUBLANES, t), I32), jax.ShapeDtypeStruct((SUBLANES, n_tiles_pad), I32)),
        compiler_params=_params(("arbitrary",)),
    )(ri, cnt)


def _scatter_kernel(pos_ref, h2_ref, xs_in_ref, xs_ref, sem, *, tm, t):
    del xs_in_ref
    base = pl.program_id(0) * tm

    def row_copy(r, p):
        return pltpu.make_async_copy(h2_ref.at[pl.ds(r, 1)], xs_ref.at[pl.ds(p, 1)], sem)

    def body(r, carry):
        row_copy(r, pos_ref[base + r]).start()
        row_copy(r, pos_ref[t + base + r]).start()
        return carry

    lax.fori_loop(0, tm, body, 0)
    for _ in range(2):
        pltpu.make_async_copy(h2_ref, xs_ref.at[pl.ds(0, tm)], sem).wait()


def _scatter(pos_flat, h2, xs0, *, tm):
    t, d = h2.shape
    grid_spec = pltpu.PrefetchScalarGridSpec(
        num_scalar_prefetch=1,
        grid=(t // tm,),
        in_specs=[pl.BlockSpec((tm, d), lambda i, pos: (i, 0)), pl.BlockSpec(memory_space=pl.ANY)],
        out_specs=pl.BlockSpec(memory_space=pl.ANY),
        scratch_shapes=[pltpu.SemaphoreType.DMA(())],
    )
    return pl.pallas_call(
        functools.partial(_scatter_kernel, tm=tm, t=t),
        grid_spec=grid_spec,
        out_shape=jax.ShapeDtypeStruct(xs0.shape, xs0.dtype),
        input_output_aliases={2: 0},
        compiler_params=_params(("arbitrary",), VMEM_LIMIT),
    )(pos_flat, h2, xs0)


def _experts_kernel(te_ref, nu_ref, x_ref, wg_ref, wu_ref, wd_ref, y_ref, wgb, wub, wdb):
    i = pl.program_id(0)

    @pl.when(i < nu_ref[0])
    def _():
        @pl.when((i == 0) | (te_ref[i] != te_ref[jnp.maximum(i - 1, 0)]))
        def _():
            wgb[...] = wg_ref[0].astype(BF16)
            wub[...] = wu_ref[0].astype(BF16)
            wdb[...] = wd_ref[0].astype(BF16)

        x = x_ref[...].astype(BF16)
        a = jnp.dot(x, wgb[...], preferred_element_type=F32)
        u = jnp.dot(x, wub[...], preferred_element_type=F32)
        act = (a * _sigmoid(a)) * u
        y_ref[...] = jnp.dot(act.astype(BF16), wdb[...], preferred_element_type=F32)

    @pl.when(i >= nu_ref[0])
    def _():
        y_ref[...] = jnp.zeros(y_ref.shape, F32)


def _experts(tile_expert, n_used, xs, w_gate, w_up, w_down, *, row_tile):
    p, d = xs.shape
    de = w_gate.shape[-1]
    row_map = lambda i, te, nu: (jnp.minimum(i, nu[0] - 1), 0)
    grid_spec = pltpu.PrefetchScalarGridSpec(
        num_scalar_prefetch=2,
        grid=(p // row_tile,),
        in_specs=[pl.BlockSpec((row_tile, d), row_map),
                  pl.BlockSpec((1, d, de), lambda i, te, nu: (te[i], 0, 0)),
                  pl.BlockSpec((1, d, de), lambda i, te, nu: (te[i], 0, 0)),
                  pl.BlockSpec((1, de, d), lambda i, te, nu: (te[i], 0, 0))],
        out_specs=pl.BlockSpec((row_tile, d), lambda i, te, nu: (i, 0)),
        scratch_shapes=[pltpu.VMEM((d, de), BF16), pltpu.VMEM((d, de), BF16), pltpu.VMEM((de, d), BF16)],
    )
    return pl.pallas_call(
        _experts_kernel,
        grid_spec=grid_spec,
        out_shape=jax.ShapeDtypeStruct((p, d), F32),
        compiler_params=_params(("arbitrary",), VMEM_LIMIT),
    )(tile_expert, n_used, xs, w_gate, w_up, w_down)


def _combine_kernel(pos_ref, x1_ref, gf_ref, rw_ref, ys_ref, g2_ref, b2_ref, o_ref, buf, sem,
                    *, bb, ts, t, alpha):
    m = bb * ts
    d = x1_ref.shape[-1]
    base = (pl.program_id(0) * pl.num_programs(1) + pl.program_id(1)) * m

    def row_copy(slot, r, p):
        return pltpu.make_async_copy(ys_ref.at[pl.ds(p, 1)], buf.at[slot, pl.ds(r, 1)], sem)

    def body(r, carry):
        row_copy(0, r, pos_ref[base + r]).start()
        row_copy(1, r, pos_ref[t + base + r]).start()
        return carry

    lax.fori_loop(0, m, body, 0)
    for slot in range(2):
        pltpu.make_async_copy(ys_ref.at[pl.ds(0, m)], buf.at[slot], sem).wait()
    wcol = jnp.concatenate([rw_ref[...], jnp.zeros((LANES - SUBLANES, m), F32)], axis=0).T
    moe = wcol[:, 0:1] * buf[0] + wcol[:, 1:2] * buf[1]
    y = alpha * x1_ref[...] + (1.0 + gf_ref[...]) * moe.reshape(bb, ts, d)
    o_ref[...] = _layernorm_rows(y, g2_ref[...], b2_ref[...])


def _combine(pos_flat, x1, gf, rw, ys, g2, b2, *, bb, ts, alpha):
    b, s, d = x1.shape
    m = bb * ts
    ns = s // ts
    grid_spec = pltpu.PrefetchScalarGridSpec(
        num_scalar_prefetch=1,
        grid=(b // bb, ns),
        in_specs=[pl.BlockSpec((bb, ts, d), lambda i, j, pos: (i, j, 0)),
                  pl.BlockSpec((bb, 1, d), lambda i, j, pos: (i, 0, 0)),
                  pl.BlockSpec((SUBLANES, m), lambda i, j, pos: (0, i * ns + j)),
                  pl.BlockSpec(memory_space=pl.ANY),
                  pl.BlockSpec((1, d), lambda i, j, pos: (0, 0)),
                  pl.BlockSpec((1, d), lambda i, j, pos: (0, 0))],
        out_specs=pl.BlockSpec((bb, ts, d), lambda i, j, pos: (i, j, 0)),
        scratch_shapes=[pltpu.VMEM((2, m, d), F32), pltpu.SemaphoreType.DMA(())],
    )
    return pl.pallas_call(
        functools.partial(_combine_kernel, bb=bb, ts=ts, t=b * s, alpha=alpha),
        grid_spec=grid_spec,
        out_shape=jax.ShapeDtypeStruct((b, s, d), F32),
        compiler_params=_params(("arbitrary", "arbitrary"), VMEM_LIMIT),
    )(pos_flat, x1, gf, rw, ys, g2, b2)


def _layer(x, mod, p, lam_init, alpha, conv_buf, c0, n0, m0, paged, *, sample):
    b, s, d = x.shape
    sh_a, sc_a, g_a, sh_f, sc_f, g_f = mod
    if sample:
        bb, ts, act = b, s, F32
    else:
        bb, ts, act = 1, min(s, 512), BF16
    ca, q, k, v, om, g, qd, kd, vd, conv_new = _proj(
        x, sc_a, sh_a, p["wa"], p["wg"], p["wb"], p["bg"], conv_buf, p["w_conv"], p["b_conv"], p["wqk"],
        bb=bb, ts=ts, act=act)

    if sample:
        chunk = LANES
        pad_rows = lambda a: jnp.pad(a, ((0, 0), (0, chunk - s), (0, 0)))
        lane = jnp.arange(GATE_LANES)
        gate_pad = jnp.where(lane < ML_HEADS, -jnp.inf, jnp.where(lane < 2 * ML_HEADS, jnp.inf, 0.0)).astype(F32)
        g_in = jnp.concatenate([g, jnp.broadcast_to(gate_pad, (b, chunk - s, GATE_LANES))], axis=1)
        hm, c1, n1, m1 = _mlstm(pad_rows(q), pad_rows(k), pad_rows(v), g_in, pad_rows(ca), pad_rows(om),
                                c0, n0, m0, p["gn_m"], p["skip_m"], chunk=chunk, act=act)
        hm = hm[:, :s]
        cache_k, cache_v, page_table = paged
        ad = _dattn_sample(qd, kd, vd, cache_k, cache_v, page_table, p["lams"], p["subln_g"],
                           pps=min(8, page_table.shape[1]), lam_init=lam_init)
    else:
        hm, c1, n1, m1 = _mlstm(q, k, v, g, ca, om, c0, n0, m0, p["gn_m"], p["skip_m"],
                                chunk=min(s, 256), act=act)
        ad = _dattn_prompt(qd, kd, vd, p["lams"], p["subln_g"], tq=min(s, 512), lam_init=lam_init, act=act)

    t = b * s
    cin = jnp.zeros((N_EXPERTS, LANES), F32)
    x1, h2, ri, rw, cnt = _mix(hm, ad, x, g_a, sc_f, sh_f, p["wo1"], p["wo2"], p["ln1_g"], p["ln1_b"],
                               p["wrt"], p["brt"], cin, bb=bb, ts=ts, alpha=alpha)
    row_tile = ROW_TILE
    n_tiles = (2 * t) // row_tile + N_EXPERTS
    n_tiles_pad = -(-n_tiles // LANES) * LANES
    pos, tiles = _plan(ri, cnt, row_tile=row_tile, n_tiles_pad=n_tiles_pad)
    pos_flat = pos[0:2].reshape(2 * t)
    xs0 = jnp.zeros((n_tiles * row_tile, d), F32)
    xs = _scatter(pos_flat, h2, xs0, tm=min(t, 1024))
    ys = _experts(tiles[0, :n_tiles], tiles[1, 0:1], xs, p["w_gate"], p["w_up"], p["w_down"], row_tile=row_tile)
    cts = ts if sample else min(s, 256)
    y = _combine(pos_flat, x1, g_f, rw, ys, p["ln2_g"], p["ln2_b"], bb=bb, ts=cts, alpha=alpha)
    return y, kd, vd, c1, n1, m1, conv_new


def _layer_params(l, w_in, w_conv, b_conv, w_mq, w_mk, b_i, b_f, gn_m, skip_m, lam_q1, lam_k1, lam_q2, lam_k2,
                  subln_g, w_out, ln1_g, ln1_b, w_rg, b_rg, w_re, b_re, w_gate, w_up, w_down, ln2_g, ln2_b):
    w3 = 3 * ML_WIDTH
    n_gate = 2 * ML_HEADS
    wi = w_in[l]
    d = wi.shape[0]
    wg = jnp.zeros((d, GATE_LANES), F32).at[:, :n_gate].set(wi[:, w3:w3 + n_gate])
    bg = jnp.zeros((1, GATE_LANES), F32).at[0, :ML_HEADS].set(b_i[l]).at[0, ML_HEADS:n_gate].set(b_f[l])
    wrt = jnp.zeros((ROUTER_ROWS, d), F32).at[:N_GROUPS].set(w_rg[l].T).at[8:].set(w_re[l].T)
    brt = jnp.zeros((ROUTER_ROWS, 1), F32).at[:N_GROUPS, 0].set(b_rg[l]).at[8:, 0].set(b_re[l])
    return {
        "wa": wi[:, :w3].astype(BF16),
        "wg": wg.astype(BF16),
        "wb": wi[:, w3 + n_gate:].astype(BF16),
        "bg": bg,
        "w_conv": w_conv[l],
        "b_conv": b_conv[l][None, :],
        "wqk": jnp.concatenate([w_mq[l], w_mk[l]], axis=-1).astype(BF16),
        "gn_m": gn_m[l], "skip_m": skip_m[l],
        "lams": (lam_q1[l][None, :], lam_k1[l][None, :], lam_q2[l][None, :], lam_k2[l][None, :]),
        "subln_g": subln_g[l][None, :],
        "wo1": w_out[l][:ML_WIDTH].astype(BF16),
        "wo2": w_out[l][ML_WIDTH:].astype(BF16),
        "ln1_g": ln1_g[l][None, :], "ln1_b": ln1_b[l][None, :],
        "wrt": wrt, "brt": brt,
        "w_gate": w_gate[l], "w_up": w_up[l], "w_down": w_down[l],
        "ln2_g": ln2_g[l][None, :], "ln2_b": ln2_b[l][None, :],
    }


def kernel(x_prompt, x_sample, cache_k, cache_v, state_C, state_n, state_m, state_conv, page_table, c_prompt, c_sample, w_ada, b_ada, w_in, w_conv, b_conv, w_mq, w_mk, b_i, b_f, gn_m, skip_m, lam_q1, lam_k1, lam_q2, lam_k2, subln_g, w_out, ln1_g, ln1_b, w_rg, b_rg, w_re, b_re, w_gate, w_up, w_down, ln2_g, ln2_b):
    depth = w_ada.shape[0]
    bp, sp, d = x_prompt.shape
    bs, ss, _ = x_sample.shape
    alpha = (2 * depth) ** 0.25
    yp, ys = x_prompt, x_sample
    outs_p = [[] for _ in range(6)]
    outs_s = [[] for _ in range(6)]
    c_all = jnp.concatenate([c_prompt, c_sample], axis=0)
    for l in range(depth):
        p = _layer_params(l, w_in, w_conv, b_conv, w_mq, w_mk, b_i, b_f, gn_m, skip_m, lam_q1, lam_k1, lam_q2,
                          lam_k2, subln_g, w_out, ln1_g, ln1_b, w_rg, b_rg, w_re, b_re, w_gate, w_up, w_down,
                          ln2_g, ln2_b)
        lam_init = 0.8 - 0.6 * math.exp(-0.3 * l)
        mod = _ada(c_all, w_ada[l], b_ada[l][None, :])
        mod_p = tuple(mod[:bp, None, i * d:(i + 1) * d] for i in range(6))
        mod_s = tuple(mod[bp:, None, i * d:(i + 1) * d] for i in range(6))
        h, hd = ML_HEADS, ML_HEAD_DIM
        res_p = _layer(yp, mod_p, p, lam_init, alpha,
                       jnp.zeros((bp, CONV_W - 1, ML_WIDTH), F32), jnp.zeros((bp, h, hd, hd), F32),
                       jnp.zeros((bp, h, hd), F32), jnp.zeros((bp, 1, h), F32), None, sample=False)
        n_pool, page = cache_k.shape[1], cache_k.shape[2]
        paged = (cache_k[l].reshape(n_pool, page, DA_WIDTH), cache_v[l].reshape(n_pool, page, DA_WIDTH), page_table)
        res_s = _layer(ys, mod_s, p, lam_init, alpha, state_conv[l], state_C[l], state_n[l],
                       state_m[l][:, None, :], paged, sample=True)
        yp, ys = res_p[0], res_s[0]
        for outs, res, nb, ns in ((outs_p, res_p, bp, sp), (outs_s, res_s, bs, ss)):
            outs[0].append(res[1].reshape(nb, ns, DA_HEADS, 2 * DA_HEAD_DIM))
            outs[1].append(res[2].reshape(nb, ns, DA_HEADS, DA_V_DIM))
            outs[2].append(res[3])
            outs[3].append(res[4])
            outs[4].append(res[5].reshape(nb, h))
            outs[5].append(res[6])
    return (yp, ys, *(jnp.stack(o) for o in outs_p), *(jnp.stack(o) for o in outs_s))
```

```python
import functools
import math

import jax
import jax.numpy as jnp
from jax import lax
from jax.experimental import pallas as pl
from jax.experimental.pallas import tpu as pltpu

F32 = jnp.float32
BF16 = jnp.bfloat16
I32 = jnp.int32
HIGHEST = lax.Precision.HIGHEST

LN_EPS = 1e-5
ML_HEADS = 4
ML_HEAD_DIM = 128
ML_WIDTH = ML_HEADS * ML_HEAD_DIM
CONV_W = 4
DA_HEADS = 4
DA_HEAD_DIM = 64
DA_V_DIM = 2 * DA_HEAD_DIM
DA_WIDTH = DA_HEADS * DA_V_DIM
N_GROUPS = 4
EXPERTS_PER_GROUP = 8
N_EXPERTS = N_GROUPS * EXPERTS_PER_GROUP
GATE_LANES = 128
ROUTER_ROWS = 8 + N_EXPERTS
SUBLANES = 8
LANES = 128
ROW_TILE = 256
VMEM_LIMIT = 56 * 2 ** 20

_NT = (((1,), (1,)), ((), ()))
LOG2E = 1.4426950408889634


def _params(sem, vmem=None):
    return pltpu.CompilerParams(dimension_semantics=sem, vmem_limit_bytes=vmem)


def _sigmoid(x):
    return jax.nn.sigmoid(x)


def _log_sigmoid(x):
    return jnp.minimum(x, 0.0) - jnp.log1p(jnp.exp(-jnp.abs(x)))


def _layernorm_rows(y, g, b):
    mu = jnp.mean(y, axis=-1, keepdims=True)
    d = y - mu
    var = jnp.mean(d * d, axis=-1, keepdims=True)
    return d * lax.rsqrt(var + LN_EPS) * g + b


def _ada_kernel(c_ref, w_ref, b_ref, o_ref):
    c = c_ref[...]
    s = c * _sigmoid(c)
    o_ref[...] = jnp.dot(s, w_ref[...], preferred_element_type=F32, precision=HIGHEST) + b_ref[...]


def _ada(c, w, b):
    bc, d = c.shape
    n = w.shape[1]
    tn = 512
    return pl.pallas_call(
        _ada_kernel,
        grid=(n // tn,),
        in_specs=[pl.BlockSpec((bc, d), lambda j: (0, 0)),
                  pl.BlockSpec((d, tn), lambda j: (0, j)),
                  pl.BlockSpec((1, tn), lambda j: (0, j))],
        out_specs=pl.BlockSpec((bc, tn), lambda j: (0, j)),
        out_shape=jax.ShapeDtypeStruct((bc, n), F32),
        compiler_params=_params(("arbitrary",)),
    )(c, w, b)


def _proj_kernel(x_ref, sc_ref, sh_ref, wa_ref, wg_ref, wb_ref, bg_ref, cbuf_ref, wconv_ref, bconv_ref, wqk_ref,
                 ca_ref, q_ref, k_ref, v_ref, om_ref, g_ref, qd_ref, kd_ref, vd_ref, cnew_ref, ext_ref,
                 *, bb, ts):
    si = pl.program_id(1)
    m = bb * ts
    d = x_ref.shape[-1]
    h = (x_ref[...] * (1.0 + sc_ref[...]) + sh_ref[...]).reshape(m, d).astype(BF16)
    pa = jnp.dot(h, wa_ref[...], preferred_element_type=F32)
    pb = jnp.dot(h, wb_ref[...], preferred_element_type=F32)
    g = jnp.dot(h, wg_ref[...], preferred_element_type=F32) + bg_ref[...]
    g_ref[...] = g.reshape(bb, ts, GATE_LANES)
    w = ML_WIDTH
    v_ref[...] = pa[:, w:2 * w].reshape(bb, ts, w).astype(v_ref.dtype)
    om_ref[...] = pa[:, 2 * w:3 * w].reshape(bb, ts, w)
    qd_ref[...] = pb[:, 0:w].reshape(bb, ts, w).astype(qd_ref.dtype)
    kd_ref[...] = pb[:, w:2 * w].reshape(bb, ts, w)
    vd_ref[...] = pb[:, 2 * w:3 * w].reshape(bb, ts, w)

    @pl.when(si == 0)
    def _():
        ext_ref[:, 5:8, :] = cbuf_ref[...]

    @pl.when(si > 0)
    def _():
        ext_ref[:, 0:8, :] = ext_ref[:, ts:ts + 8, :]

    ext_ref[:, 8:8 + ts, :] = pa[:, 0:w].reshape(bb, ts, w)
    y = bconv_ref[...]
    for j in range(CONV_W):
        y = y + wconv_ref[j:j + 1, :] * ext_ref[:, 5 + j:5 + j + ts, :]
    ca = y * _sigmoid(y)
    ca_ref[...] = ca
    cnew_ref[...] = ext_ref[:, ts + 5:ts + 8, :]

    ca2 = ca.reshape(m, w)
    hd = ML_HEAD_DIM
    for hh in range(ML_HEADS):
        qk = jnp.dot(ca2[:, hh * hd:(hh + 1) * hd].astype(BF16), wqk_ref[hh], preferred_element_type=F32)
        q_ref[:, :, hh * hd:(hh + 1) * hd] = (qk[:, 0:hd] * (hd ** -0.5)).reshape(bb, ts, hd).astype(q_ref.dtype)
        k_ref[:, :, hh * hd:(hh + 1) * hd] = qk[:, hd:2 * hd].reshape(bb, ts, hd).astype(k_ref.dtype)


def _proj(x, sc, sh, wa, wg, wb, bg, cbuf, wconv, bconv, wqk, *, bb, ts, act):
    b, s, d = x.shape
    w = ML_WIDTH
    grid = (b // bb, s // ts)
    tok = lambda n: pl.BlockSpec((bb, ts, n), lambda i, j: (i, j, 0))
    per_b = lambda r, n: pl.BlockSpec((bb, r, n), lambda i, j: (i, 0, 0))
    const = lambda shape: pl.BlockSpec(shape, lambda i, j: (0,) * len(shape))
    out_shape = (
        jax.ShapeDtypeStruct((b, s, w), F32),
        jax.ShapeDtypeStruct((b, s, w), act),
        jax.ShapeDtypeStruct((b, s, w), act),
        jax.ShapeDtypeStruct((b, s, w), act),
        jax.ShapeDtypeStruct((b, s, w), F32),
        jax.ShapeDtypeStruct((b, s, GATE_LANES), F32),
        jax.ShapeDtypeStruct((b, s, w), act),
        jax.ShapeDtypeStruct((b, s, w), F32),
        jax.ShapeDtypeStruct((b, s, w), F32),
        jax.ShapeDtypeStruct((b, CONV_W - 1, w), F32),
    )
    out_specs = (tok(w), tok(w), tok(w), tok(w), tok(w), tok(GATE_LANES), tok(w), tok(w), tok(w),
                 per_b(CONV_W - 1, w))
    return pl.pallas_call(
        functools.partial(_proj_kernel, bb=bb, ts=ts),
        grid=grid,
        in_specs=[tok(d), per_b(1, d), per_b(1, d), const(wa.shape), const(wg.shape), const(wb.shape),
                  const(bg.shape), per_b(CONV_W - 1, w), const(wconv.shape), const(bconv.shape),
                  const(wqk.shape)],
        out_specs=out_specs,
        out_shape=out_shape,
        scratch_shapes=[pltpu.VMEM((bb, ts + 8, w), F32)],
        compiler_params=_params(("arbitrary", "arbitrary"), VMEM_LIMIT),
    )(x, sc, sh, wa, wg, wb, bg, cbuf, wconv, bconv, wqk)


def _mlstm_kernel(q_ref, k_ref, v_ref, g_ref, ca_ref, om_ref, c0_ref, n0_ref, m0_ref, gn_ref, skip_ref,
                  hm_ref, c1_ref, n1_ref, m1_ref, c_s, n_s, m_s, *, chunk):
    si = pl.program_id(1)
    ln = chunk
    hd = ML_HEAD_DIM

    @pl.when(si == 0)
    def _():
        c_s[...] = c0_ref[0]
        n_s[...] = n0_ref[0]
        m_s[...] = m0_ref[0]

    g = g_ref[0]
    row = lax.broadcasted_iota(I32, (ln, GATE_LANES), 0)
    bc = _log_sigmoid(g)
    shift = 1
    while shift < ln:
        bc = bc + jnp.where(row >= shift, pltpu.roll(bc, shift, axis=0), 0.0)
        shift *= 2
    g_t = g.T
    bc_t = bc.T
    causal = lax.broadcasted_iota(I32, (ln, ln), 0) >= lax.broadcasted_iota(I32, (ln, ln), 1)

    for hh in range(ML_HEADS):
        cols = slice(hh * hd, (hh + 1) * hd)
        qb = q_ref[0, :, cols].astype(BF16)
        kf = k_ref[0, :, cols].astype(F32)
        kb = kf.astype(BF16)
        vb = v_ref[0, :, cols].astype(BF16)
        b_col = bc[:, ML_HEADS + hh:ML_HEADS + hh + 1]
        i_col = g[:, hh:hh + 1]
        b_row = bc_t[ML_HEADS + hh:ML_HEADS + hh + 1, :]
        i_row = g_t[hh:hh + 1, :]
        m_prev = m_s[:, hh:hh + 1]
        log_d = jnp.where(causal, b_col - b_row + i_row, -jnp.inf)
        inter = b_col + m_prev
        m_t = jnp.maximum(inter, jnp.max(log_d, axis=-1, keepdims=True))
        w_inter = jnp.exp(inter - m_t)
        s = lax.dot_general(qb, kb, _NT, preferred_element_type=F32) * jnp.exp(log_d - m_t)
        c_old = c_s[hh]
        n_old = n_s[hh:hh + 1, :]
        num = (w_inter * jnp.dot(qb, c_old.astype(BF16), preferred_element_type=F32)
               + jnp.dot(s.astype(BF16), vb, preferred_element_type=F32))
        den = (w_inter * jnp.sum(qb.astype(F32) * n_old, axis=-1, keepdims=True)
               + jnp.sum(s, axis=-1, keepdims=True))
        hc = num / jnp.maximum(jnp.abs(den), jnp.exp(-m_t))
        m_new = m_t[ln - 1:ln, :]
        b_last = b_col[ln - 1:ln, :]
        w_state = jnp.exp(b_last + m_prev - m_new)
        kw = jnp.exp(b_last - b_col + i_col - m_new) * kf
        c_s[hh] = w_state * c_old + jnp.dot(kw.T.astype(BF16), vb, preferred_element_type=F32)
        n_s[hh:hh + 1, :] = w_state * n_old + jnp.sum(kw, axis=0, keepdims=True)
        m_s[:, hh:hh + 1] = m_new
        mu = jnp.mean(hc, axis=-1, keepdims=True)
        dlt = hc - mu
        var = jnp.mean(dlt * dlt, axis=-1, keepdims=True)
        hn = dlt * lax.rsqrt(var + LN_EPS) * gn_ref[hh:hh + 1, :]
        out = (hn + skip_ref[hh:hh + 1, :] * ca_ref[0, :, cols]) * _sigmoid(om_ref[0, :, cols])
        hm_ref[0, :, cols] = out.astype(hm_ref.dtype)

    @pl.when(si == pl.num_programs(1) - 1)
    def _():
        c1_ref[0] = c_s[...]
        n1_ref[0] = n_s[...]
        m1_ref[0] = m_s[...]


def _mlstm(q, k, v, g, ca, om, c0, n0, m0, gn, skip, *, chunk, act):
    b, s, w = q.shape
    h, hd = ML_HEADS, ML_HEAD_DIM
    tok = lambda n: pl.BlockSpec((1, chunk, n), lambda i, j: (i, j, 0))
    c_spec = pl.BlockSpec((1, h, hd, hd), lambda i, j: (i, 0, 0, 0))
    n_spec = pl.BlockSpec((1, h, hd), lambda i, j: (i, 0, 0))
    m_spec = pl.BlockSpec((1, 1, h), lambda i, j: (i, 0, 0))
    hw_spec = pl.BlockSpec((h, hd), lambda i, j: (0, 0))
    return pl.pallas_call(
        functools.partial(_mlstm_kernel, chunk=chunk),
        grid=(b, s // chunk),
        in_specs=[tok(w), tok(w), tok(w), tok(GATE_LANES), tok(w), tok(w), c_spec, n_spec, m_spec,
                  hw_spec, hw_spec],
        out_specs=(tok(w), c_spec, n_spec, m_spec),
        out_shape=(jax.ShapeDtypeStruct((b, s, w), act),
                   jax.ShapeDtypeStruct((b, h, hd, hd), F32),
                   jax.ShapeDtypeStruct((b, h, hd), F32),
                   jax.ShapeDtypeStruct((b, 1, h), F32)),
        scratch_shapes=[pltpu.VMEM((h, hd, hd), F32), pltpu.VMEM((h, hd), F32), pltpu.VMEM((1, h), F32)],
        compiler_params=_params(("arbitrary", "arbitrary"), VMEM_LIMIT),
    )(q, k, v, g, ca, om, c0, n0, m0, gn, skip)


def _lam(lq1_ref, lk1_ref, lq2_ref, lk2_ref, lam_init):
    a = jnp.sum(lq1_ref[...] * lk1_ref[...], axis=-1, keepdims=True)
    b = jnp.sum(lq2_ref[...] * lk2_ref[...], axis=-1, keepdims=True)
    return jnp.exp(a) - jnp.exp(b) + lam_init


def _head_rms(o, sg, lam_init):
    return o * lax.rsqrt(jnp.mean(o * o, axis=-1, keepdims=True) + LN_EPS) * sg * (1.0 - lam_init)


def _softmax_update(s, vt, m_ref, l_ref, a_ref):
    m_old = m_ref[...]
    m_new = jnp.maximum(m_old, jnp.max(s, axis=-1, keepdims=True))
    alpha = jnp.exp2(m_old - m_new)
    p = jnp.exp2(s - m_new)
    l_ref[...] = alpha * l_ref[...] + jnp.sum(p, axis=-1, keepdims=True)
    a_ref[...] = alpha * a_ref[...] + jnp.dot(p.astype(BF16), vt, preferred_element_type=F32)
    m_ref[...] = m_new


def _dattn_kernel(lq1_ref, lk1_ref, lq2_ref, lk2_ref, sgc_ref, q_ref, k_ref, v_ref, o_ref,
                  kb_s, vt_s, m1, l1, a1, m2, l2, a2, *, tq, lam_init):
    qi = pl.program_id(2)
    n_chunks = k_ref.shape[1] // tq

    @pl.when(qi == 0)
    def _():
        for c in range(n_chunks):
            rows = slice(c * tq, (c + 1) * tq)
            kb_s[c] = k_ref[0, rows, :].astype(BF16)
            vt_s[c] = v_ref[0, rows, :].T.astype(BF16)

    lam = _lam(lq1_ref, lk1_ref, lq2_ref, lk2_ref, lam_init)
    q = q_ref[0].astype(F32) * (DA_HEAD_DIM ** -0.5 * LOG2E)
    lane = lax.broadcasted_iota(I32, q.shape, 1)
    q1 = jnp.where(lane < DA_HEAD_DIM, q, 0.0).astype(BF16)
    q2 = jnp.where(lane >= DA_HEAD_DIM, q, 0.0).astype(BF16)
    for m_ref, l_ref, a_ref in ((m1, l1, a1), (m2, l2, a2)):
        m_ref[...] = jnp.full(m_ref.shape, -jnp.inf, F32)
        l_ref[...] = jnp.zeros(l_ref.shape, F32)
        a_ref[...] = jnp.zeros(a_ref.shape, F32)
    key_le_query = (lax.broadcasted_iota(I32, (tq, tq), 0) <= lax.broadcasted_iota(I32, (tq, tq), 1))

    def kv_step(j, masked):
        kt = kb_s[j]
        vt = vt_s[j]
        for qz, m_ref, l_ref, a_ref in ((q1, m1, l1, a1), (q2, m2, l2, a2)):
            st = lax.dot_general(kt, qz, _NT, preferred_element_type=F32)
            if masked:
                st = jnp.where(key_le_query, st, -jnp.inf)
            m_old = m_ref[...]
            m_new = jnp.maximum(m_old, jnp.max(st, axis=0, keepdims=True))
            alpha = jnp.exp2(m_old - m_new)
            p = jnp.exp2(st - m_new)
            l_ref[...] = alpha * l_ref[...] + jnp.sum(p, axis=0, keepdims=True)
            a_ref[...] = alpha * a_ref[...] + jnp.dot(vt, p.astype(BF16), preferred_element_type=F32)
            m_ref[...] = m_new

    def body(j, carry):
        kv_step(j, False)
        return carry

    lax.fori_loop(0, qi, body, 0)
    kv_step(qi, True)
    ot = a1[...] / l1[...] - lam * (a2[...] / l2[...])
    ot = ot * lax.rsqrt(jnp.mean(ot * ot, axis=0, keepdims=True) + LN_EPS) * sgc_ref[...] * (1.0 - lam_init)
    o_ref[0] = ot.T.astype(o_ref.dtype)


def _dattn_prompt(qd, kd, vd, lams, sgc, *, tq, lam_init, act):
    b, s, w = qd.shape
    dv = DA_V_DIM
    lam_spec = pl.BlockSpec((1, DA_HEAD_DIM), lambda i, h, j: (0, 0))
    stat = pltpu.VMEM((1, tq), F32)
    acc = pltpu.VMEM((dv, tq), F32)
    return pl.pallas_call(
        functools.partial(_dattn_kernel, tq=tq, lam_init=lam_init),
        grid=(b, DA_HEADS, s // tq),
        in_specs=[lam_spec, lam_spec, lam_spec, lam_spec,
                  pl.BlockSpec((dv, 1), lambda i, h, j: (0, 0)),
                  pl.BlockSpec((1, tq, dv), lambda i, h, j: (i, j, h)),
                  pl.BlockSpec((1, s, dv), lambda i, h, j: (i, 0, h)),
                  pl.BlockSpec((1, s, dv), lambda i, h, j: (i, 0, h))],
        out_specs=pl.BlockSpec((1, tq, dv), lambda i, h, j: (i, j, h)),
        out_shape=jax.ShapeDtypeStruct((b, s, w), act),
        scratch_shapes=[pltpu.VMEM((s // tq, tq, dv), BF16), pltpu.VMEM((s // tq, dv, tq), BF16),
                        stat, stat, acc, stat, stat, acc],
        compiler_params=_params(("arbitrary", "arbitrary", "arbitrary"), VMEM_LIMIT),
    )(*lams, sgc, qd, kd, vd)


def _heads_to_rows(x):
    return jnp.concatenate([x[:, h * DA_V_DIM:(h + 1) * DA_V_DIM] for h in range(DA_HEADS)], axis=0)


def _sattn_kernel(pt_ref, lq1_ref, lk1_ref, lq2_ref, lk2_ref, sg_ref, q_ref, kn_ref, vn_ref, *rest,
                  pps, sd, lam_init):
    del pt_ref
    k_refs = rest[0:pps]
    v_refs = rest[pps:2 * pps]
    o_ref, qx_s, m_s, l_s, a_s = rest[2 * pps:]
    j = pl.program_id(1)
    nq = DA_HEADS * sd
    nrow = 2 * nq
    dv = DA_V_DIM

    @pl.when(j == 0)
    def _():
        qh = _heads_to_rows(q_ref[0].astype(F32) * (DA_HEAD_DIM ** -0.5 * LOG2E))
        lane = lax.broadcasted_iota(I32, qh.shape, 1)
        qx_s[...] = jnp.concatenate([jnp.where(lane < DA_HEAD_DIM, qh, 0.0),
                                     jnp.where(lane >= DA_HEAD_DIM, qh, 0.0)], axis=0).astype(BF16)
        m_s[...] = jnp.full(m_s.shape, -jnp.inf, F32)
        l_s[...] = jnp.zeros(l_s.shape, F32)
        a_s[...] = jnp.zeros(a_s.shape, F32)

    qx = qx_s[...]
    n_keys = k_refs[0].shape[1]
    r = lax.broadcasted_iota(I32, (nrow, n_keys), 0)
    c = lax.broadcasted_iota(I32, (nrow, n_keys), 1)
    bias = jnp.where(((r // sd) % DA_HEADS) == (c % DA_HEADS), 0.0, -jnp.inf)
    scores = [lax.dot_general(qx, k_refs[p][0].astype(BF16), _NT, preferred_element_type=F32) + bias
              for p in range(pps)]
    m_old = m_s[...]
    m_new = m_old
    for sp in scores:
        m_new = jnp.maximum(m_new, jnp.max(sp, axis=-1, keepdims=True))
    alpha = jnp.exp2(m_old - m_new)
    lsum = alpha * l_s[...]
    acc = alpha * a_s[...]
    for p, sp in enumerate(scores):
        pp = jnp.exp2(sp - m_new)
        lsum = lsum + jnp.sum(pp, axis=-1, keepdims=True)
        acc = acc + jnp.dot(pp.astype(BF16), v_refs[p][0].astype(BF16), preferred_element_type=F32)
    l_s[...] = lsum
    a_s[...] = acc
    m_s[...] = m_new

    @pl.when(j == pl.num_programs(1) - 1)
    def _():
        pad = jnp.zeros((LANES - nq, dv), F32)
        kn = jnp.concatenate([_heads_to_rows(kn_ref[0]), pad], axis=0).astype(BF16)
        vn = jnp.concatenate([_heads_to_rows(vn_ref[0]), pad], axis=0).astype(BF16)
        sn = lax.dot_general(qx, kn, _NT, preferred_element_type=F32)
        rn = lax.broadcasted_iota(I32, sn.shape, 0)
        cn = lax.broadcasted_iota(I32, sn.shape, 1)
        valid = (cn < nq) & ((cn // sd) == ((rn // sd) % DA_HEADS)) & ((cn % sd) <= (rn % sd))
        _softmax_update(jnp.where(valid, sn, -jnp.inf), vn, m_s, l_s, a_s)
        lam = _lam(lq1_ref, lk1_ref, lq2_ref, lk2_ref, lam_init)
        o = a_s[0:nq, :] / l_s[0:nq, :] - lam * (a_s[nq:nrow, :] / l_s[nq:nrow, :])
        o = _head_rms(o, sg_ref[...], lam_init)
        for hh in range(DA_HEADS):
            o_ref[0, :, hh * dv:(hh + 1) * dv] = o[hh * sd:(hh + 1) * sd, :].astype(o_ref.dtype)


def _dattn_sample(qd, kn, vn, cache_k, cache_v, page_table, lams, sg, *, pps, lam_init):
    b, sd, w = qd.shape
    n_pages = page_table.shape[1]
    n_keys, dv = cache_k.shape[1], cache_k.shape[2]
    nrow = 2 * DA_HEADS * sd
    lam_spec = pl.BlockSpec((1, DA_HEAD_DIM), lambda i, j, pt: (0, 0))
    tok = pl.BlockSpec((1, sd, w), lambda i, j, pt: (i, 0, 0))

    def page_spec(p):
        return pl.BlockSpec((1, n_keys, dv), lambda i, j, pt: (pt[i, j * pps + p], 0, 0))

    grid_spec = pltpu.PrefetchScalarGridSpec(
        num_scalar_prefetch=1,
        grid=(b, n_pages // pps),
        in_specs=[lam_spec, lam_spec, lam_spec, lam_spec,
                  pl.BlockSpec((1, dv), lambda i, j, pt: (0, 0)),
                  tok, tok, tok]
                 + [page_spec(p) for p in range(pps)] + [page_spec(p) for p in range(pps)],
        out_specs=tok,
        scratch_shapes=[pltpu.VMEM((nrow, dv), BF16), pltpu.VMEM((nrow, 1), F32), pltpu.VMEM((nrow, 1), F32),
                        pltpu.VMEM((nrow, dv), F32)],
    )
    return pl.pallas_call(
        functools.partial(_sattn_kernel, pps=pps, sd=sd, lam_init=lam_init),
        grid_spec=grid_spec,
        out_shape=jax.ShapeDtypeStruct((b, sd, w), F32),
        compiler_params=_params(("arbitrary", "arbitrary"), VMEM_LIMIT),
    )(page_table, *lams, sg, qd, kn, vn, *([cache_k] * pps), *([cache_v] * pps))


def _mix_kernel(hm_ref, ad_ref, x_ref, ga_ref, scf_ref, shf_ref, wo1_ref, wo2_ref, g1_ref, b1_ref,
                wrt_ref, brt_ref, cin_ref, x1_ref, h2_ref, ri_ref, rw_ref, cnt_ref, carry_s,
                *, bb, ts, alpha):
    m = bb * ts
    d = x_ref.shape[-1]

    @pl.when((pl.program_id(0) == 0) & (pl.program_id(1) == 0))
    def _():
        carry_s[...] = cin_ref[...]

    hm = hm_ref[...].reshape(m, ML_WIDTH).astype(BF16)
    ad = ad_ref[...].reshape(m, DA_WIDTH).astype(BF16)
    mixed = (jnp.dot(hm, wo1_ref[...], preferred_element_type=F32)
             + jnp.dot(ad, wo2_ref[...], preferred_element_type=F32))
    y = alpha * x_ref[...] + (1.0 + ga_ref[...]) * mixed.reshape(bb, ts, d)
    x1 = _layernorm_rows(y, g1_ref[...], b1_ref[...])
    x1_ref[...] = x1
    h2 = (x1 * (1.0 + scf_ref[...]) + shf_ref[...]).reshape(m, d)
    h2_ref[...] = h2

    lt = lax.dot_general(wrt_ref[...], h2, _NT, preferred_element_type=F32, precision=HIGHEST) + brt_ref[...]
    gl = lt[0:N_GROUPS]
    gmax = jnp.max(gl, axis=0, keepdims=True)
    r4 = lax.broadcasted_iota(I32, gl.shape, 0)
    gidx = jnp.min(jnp.where(gl == gmax, r4, N_GROUPS), axis=0, keepdims=True)
    gp = 1.0 / jnp.sum(jnp.exp(gl - gmax), axis=0, keepdims=True)
    epg = EXPERTS_PER_GROUP
    esel = lt[8 + (N_GROUPS - 1) * epg:8 + N_GROUPS * epg]
    for grp in range(N_GROUPS - 2, -1, -1):
        esel = jnp.where(gidx == grp, lt[8 + grp * epg:8 + (grp + 1) * epg], esel)
    r8 = lax.broadcasted_iota(I32, esel.shape, 0)
    t1 = jnp.max(esel, axis=0, keepdims=True)
    i1 = jnp.min(jnp.where(esel == t1, r8, epg), axis=0, keepdims=True)
    rest = jnp.where(r8 == i1, -jnp.inf, esel)
    t2 = jnp.max(rest, axis=0, keepdims=True)
    i2 = jnp.min(jnp.where(rest == t2, r8, epg), axis=0, keepdims=True)
    z = jnp.exp(t2 - t1)
    w1 = gp / (1.0 + z)
    w2 = gp * z / (1.0 + z)
    e0 = gidx * epg + i1
    e1 = gidx * epg + i2

    r32 = lax.broadcasted_iota(I32, (N_EXPERTS, m), 0)
    hit0 = r32 == e0
    hit1 = r32 == e1
    onehot = jnp.where(hit0, 1.0, jnp.where(hit1, 1.0, 0.0))
    before = (lax.broadcasted_iota(I32, (m, m), 0) < lax.broadcasted_iota(I32, (m, m), 1))
    prefix = jnp.dot(onehot.astype(BF16), jnp.where(before, 1.0, 0.0).astype(BF16),
                     preferred_element_type=F32) + carry_s[:, 0:1]
    rank0 = jnp.sum(jnp.where(hit0, prefix, 0.0), axis=0, keepdims=True).astype(I32)
    rank1 = jnp.sum(jnp.where(hit1, prefix, 0.0), axis=0, keepdims=True).astype(I32)
    carry_s[...] = carry_s[...] + jnp.sum(onehot, axis=1, keepdims=True)
    cnt_ref[...] = carry_s[...]
    rr = lax.broadcasted_iota(I32, (SUBLANES, m), 0)
    ri_ref[...] = jnp.where(rr == 0, e0, jnp.where(rr == 1, e1, jnp.where(rr == 2, rank0,
                            jnp.where(rr == 3, rank1, 0))))
    rw_ref[...] = jnp.where(rr == 0, w1, jnp.where(rr == 1, w2, 0.0))


def _mix(hm, ad, x, ga, scf, shf, wo1, wo2, g1, b1, wrt, brt, cin, *, bb, ts, alpha):
    b, s, d = x.shape
    m = bb * ts
    t = b * s
    ns = s // ts
    tok = lambda n: pl.BlockSpec((bb, ts, n), lambda i, j: (i, j, 0))
    per_b = pl.BlockSpec((bb, 1, d), lambda i, j: (i, 0, 0))
    const = lambda shape: pl.BlockSpec(shape, lambda i, j: (0,) * len(shape))
    lin = pl.BlockSpec((SUBLANES, m), lambda i, j: (0, i * ns + j))
    return pl.pallas_call(
        functools.partial(_mix_kernel, bb=bb, ts=ts, alpha=alpha),
        grid=(b // bb, ns),
        in_specs=[tok(ML_WIDTH), tok(DA_WIDTH), tok(d), per_b, per_b, per_b, const(wo1.shape), const(wo2.shape),
                  const(g1.shape), const(b1.shape), const(wrt.shape), const(brt.shape), const(cin.shape)],
        out_specs=(tok(d), pl.BlockSpec((m, d), lambda i, j: (i * ns + j, 0)), lin, lin, const(cin.shape)),
        out_shape=(jax.ShapeDtypeStruct((b, s, d), F32),
                   jax.ShapeDtypeStruct((t, d), F32),
                   jax.ShapeDtypeStruct((SUBLANES, t), I32),
                   jax.ShapeDtypeStruct((SUBLANES, t), F32),
                   jax.ShapeDtypeStruct(cin.shape, F32)),
        scratch_shapes=[pltpu.VMEM(cin.shape, F32)],
        compiler_params=_params(("arbitrary", "arbitrary"), VMEM_LIMIT),
    )(hm, ad, x, ga, scf, shf, wo1, wo2, g1, b1, wrt, brt, cin)


def _plan_kernel(ri_ref, cnt_ref, pos_ref, tile_ref, *, row_tile):
    cnt = cnt_ref[...]
    padded = jnp.floor((cnt + (row_tile - 1)) * (1.0 / row_tile)) * row_tile
    row = lax.broadcasted_iota(I32, cnt.shape, 0)
    ends = padded
    shift = 1
    while shift < N_EXPERTS:
        ends = ends + jnp.where(row >= shift, pltpu.roll(ends, shift, axis=0), 0.0)
        shift *= 2
    offs = (ends - padded)[:, 0:1]
    tp = ri_ref.shape[1]
    r32 = lax.broadcasted_iota(I32, (N_EXPERTS, tp), 0)
    pos0 = jnp.sum(jnp.where(r32 == ri_ref[0:1, :], offs, 0.0), axis=0, keepdims=True).astype(I32) + ri_ref[2:3, :]
    pos1 = jnp.sum(jnp.where(r32 == ri_ref[1:2, :], offs, 0.0), axis=0, keepdims=True).astype(I32) + ri_ref[3:4, :]
    rr = lax.broadcasted_iota(I32, (SUBLANES, tp), 0)
    pos_ref[...] = jnp.where(rr == 0, pos0, jnp.where(rr == 1, pos1, 0))
    nt = tile_ref.shape[1]
    first_row = (lax.broadcasted_iota(I32, (N_EXPERTS, nt), 1) * row_tile).astype(F32)
    done = jnp.sum(jnp.where(ends[:, 0:1] <= first_row, 1, 0), axis=0, keepdims=True)
    expert = jnp.minimum(done, N_EXPERTS - 1)
    used = (ends[N_EXPERTS - 1:N_EXPERTS, 0:1] * (1.0 / row_tile)).astype(I32)
    rt = lax.broadcasted_iota(I32, (SUBLANES, nt), 0)
    tile_ref[...] = jnp.where(rt == 0, expert, jnp.where(rt == 1, used, 0))


def _plan(ri, cnt, *, row_tile, n_tiles_pad):
    t = ri.shape[1]
    tp = min(t, 4096)
    return pl.pallas_call(
        functools.partial(_plan_kernel, row_tile=row_tile),
        grid=(t // tp,),
        in_specs=[pl.BlockSpec((SUBLANES, tp), lambda i: (0, i)), pl.BlockSpec(cnt.shape, lambda i: (0, 0))],
        out_specs=(pl.BlockSpec((SUBLANES, tp), lambda i: (0, i)),
                   pl.BlockSpec((SUBLANES, n_tiles_pad), lambda i: (0, 0))),
        out_shape=(jax.ShapeDtypeStruct((SUBLANES, t), I32), jax.ShapeDtypeStruct((SUBLANES, n_tiles_pad), I32)),
        compiler_params=_params(("arbitrary",)),
    )(ri, cnt)


def _scatter_kernel(pos_ref, h2_ref, xs_in_ref, xs_ref, sem, *, tm, t):
    del xs_in_ref
    base = pl.program_id(0) * tm

    def row_copy(r, p):
        return pltpu.make_async_copy(h2_ref.at[pl.ds(r, 1)], xs_ref.at[pl.ds(p, 1)], sem)

    def body(r, carry):
        row_copy(r, pos_ref[base + r]).start()
        row_copy(r, pos_ref[t + base + r]).start()
        return carry

    lax.fori_loop(0, tm, body, 0)
    for _ in range(2):
        pltpu.make_async_copy(h2_ref, xs_ref.at[pl.ds(0, tm)], sem).wait()


def _scatter(pos_flat, h2, xs0, *, tm):
    t, d = h2.shape
    grid_spec = pltpu.PrefetchScalarGridSpec(
        num_scalar_prefetch=1,
        grid=(t // tm,),
        in_specs=[pl.BlockSpec((tm, d), lambda i, pos: (i, 0)), pl.BlockSpec(memory_space=pl.ANY)],
        out_specs=pl.BlockSpec(memory_space=pl.ANY),
        scratch_shapes=[pltpu.SemaphoreType.DMA(())],
    )
    return pl.pallas_call(
        functools.partial(_scatter_kernel, tm=tm, t=t),
        grid_spec=grid_spec,
        out_shape=jax.ShapeDtypeStruct(xs0.shape, xs0.dtype),
        input_output_aliases={2: 0},
        compiler_params=_params(("arbitrary",), VMEM_LIMIT),
    )(pos_flat, h2, xs0)


def _experts_kernel(te_ref, nu_ref, x_ref, wg_ref, wu_ref, wd_ref, y_ref, wgb, wub, wdb):
    i = pl.program_id(0)

    @pl.when(i < nu_ref[0])
    def _():
        @pl.when((i == 0) | (te_ref[i] != te_ref[jnp.maximum(i - 1, 0)]))
        def _():
            wgb[...] = wg_ref[0].astype(BF16)
            wub[...] = wu_ref[0].astype(BF16)
            wdb[...] = wd_ref[0].astype(BF16)

        x = x_ref[...].astype(BF16)
        a = jnp.dot(x, wgb[...], preferred_element_type=F32)
        u = jnp.dot(x, wub[...], preferred_element_type=F32)
        act = (a * _sigmoid(a)) * u
        y_ref[...] = jnp.dot(act.astype(BF16), wdb[...], preferred_element_type=F32)

    @pl.when(i >= nu_ref[0])
    def _():
        y_ref[...] = jnp.zeros(y_ref.shape, F32)


def _experts(tile_expert, n_used, xs, w_gate, w_up, w_down, *, row_tile):
    p, d = xs.shape
    de = w_gate.shape[-1]
    row_map = lambda i, te, nu: (jnp.minimum(i, nu[0] - 1), 0)
    grid_spec = pltpu.PrefetchScalarGridSpec(
        num_scalar_prefetch=2,
        grid=(p // row_tile,),
        in_specs=[pl.BlockSpec((row_tile, d), row_map),
                  pl.BlockSpec((1, d, de), lambda i, te, nu: (te[i], 0, 0)),
                  pl.BlockSpec((1, d, de), lambda i, te, nu: (te[i], 0, 0)),
                  pl.BlockSpec((1, de, d), lambda i, te, nu: (te[i], 0, 0))],
        out_specs=pl.BlockSpec((row_tile, d), lambda i, te, nu: (i, 0)),
        scratch_shapes=[pltpu.VMEM((d, de), BF16), pltpu.VMEM((d, de), BF16), pltpu.VMEM((de, d), BF16)],
    )
    return pl.pallas_call(
        _experts_kernel,
        grid_spec=grid_spec,
        out_shape=jax.ShapeDtypeStruct((p, d), F32),
        compiler_params=_params(("arbitrary",), VMEM_LIMIT),
    )(tile_expert, n_used, xs, w_gate, w_up, w_down)


def _combine_kernel(pos_ref, x1_ref, gf_ref, rw_ref, ys_ref, g2_ref, b2_ref, o_ref, buf, sem,
                    *, bb, ts, t, alpha):
    m = bb * ts
    d = x1_ref.shape[-1]
    base = (pl.program_id(0) * pl.num_programs(1) + pl.program_id(1)) * m

    def row_copy(slot, r, p):
        return pltpu.make_async_copy(ys_ref.at[pl.ds(p, 1)], buf.at[slot, pl.ds(r, 1)], sem)

    def body(r, carry):
        row_copy(0, r, pos_ref[base + r]).start()
        row_copy(1, r, pos_ref[t + base + r]).start()
        return carry

    lax.fori_loop(0, m, body, 0)
    for slot in range(2):
        pltpu.make_async_copy(ys_ref.at[pl.ds(0, m)], buf.at[slot], sem).wait()
    wcol = jnp.concatenate([rw_ref[...], jnp.zeros((LANES - SUBLANES, m), F32)], axis=0).T
    moe = wcol[:, 0:1] * buf[0] + wcol[:, 1:2] * buf[1]
    y = alpha * x1_ref[...] + (1.0 + gf_ref[...]) * moe.reshape(bb, ts, d)
    o_ref[...] = _layernorm_rows(y, g2_ref[...], b2_ref[...])


def _combine(pos_flat, x1, gf, rw, ys, g2, b2, *, bb, ts, alpha):
    b, s, d = x1.shape
    m = bb * ts
    ns = s // ts
    grid_spec = pltpu.PrefetchScalarGridSpec(
        num_scalar_prefetch=1,
        grid=(b // bb, ns),
        in_specs=[pl.BlockSpec((bb, ts, d), lambda i, j, pos: (i, j, 0)),
                  pl.BlockSpec((bb, 1, d), lambda i, j, pos: (i, 0, 0)),
                  pl.BlockSpec((SUBLANES, m), lambda i, j, pos: (0, i * ns + j)),
                  pl.BlockSpec(memory_space=pl.ANY),
                  pl.BlockSpec((1, d), lambda i, j, pos: (0, 0)),
                  pl.BlockSpec((1, d), lambda i, j, pos: (0, 0))],
        out_specs=pl.BlockSpec((bb, ts, d), lambda i, j, pos: (i, j, 0)),
        scratch_shapes=[pltpu.VMEM((2, m, d), F32), pltpu.SemaphoreType.DMA(())],
    )
    return pl.pallas_call(
        functools.partial(_combine_kernel, bb=bb, ts=ts, t=b * s, alpha=alpha),
        grid_spec=grid_spec,
        out_shape=jax.ShapeDtypeStruct((b, s, d), F32),
        compiler_params=_params(("arbitrary", "arbitrary"), VMEM_LIMIT),
    )(pos_flat, x1, gf, rw, ys, g2, b2)


def _layer(x, mod, p, lam_init, alpha, conv_buf, c0, n0, m0, paged, *, sample):
    b, s, d = x.shape
    sh_a, sc_a, g_a, sh_f, sc_f, g_f = mod
    if sample:
        bb, ts, act = b, s, F32
    else:
        bb, ts, act = 1, min(s, 512), BF16
    ca, q, k, v, om, g, qd, kd, vd, conv_new = _proj(
        x, sc_a, sh_a, p["wa"], p["wg"], p["wb"], p["bg"], conv_buf, p["w_conv"], p["b_conv"], p["wqk"],
        bb=bb, ts=ts, act=act)

    if sample:
        chunk = LANES
        pad_rows = lambda a: jnp.pad(a, ((0, 0), (0, chunk - s), (0, 0)))
        lane = jnp.arange(GATE_LANES)
        gate_pad = jnp.where(lane < ML_HEADS, -jnp.inf, jnp.where(lane < 2 * ML_HEADS, jnp.inf, 0.0)).astype(F32)
        g_in = jnp.concatenate([g, jnp.broadcast_to(gate_pad, (b, chunk - s, GATE_LANES))], axis=1)
        hm, c1, n1, m1 = _mlstm(pad_rows(q), pad_rows(k), pad_rows(v), g_in, pad_rows(ca), pad_rows(om),
                                c0, n0, m0, p["gn_m"], p["skip_m"], chunk=chunk, act=act)
        hm = hm[:, :s]
        cache_k, cache_v, page_table = paged
        ad = _dattn_sample(qd, kd, vd, cache_k, cache_v, page_table, p["lams"], p["subln_g"],
                           pps=min(8, page_table.shape[1]), lam_init=lam_init)
    else:
        hm, c1, n1, m1 = _mlstm(q, k, v, g, ca, om, c0, n0, m0, p["gn_m"], p["skip_m"],
                                chunk=min(s, 256), act=act)
        ad = _dattn_prompt(qd, kd, vd, p["lams"], p["subln_g"].reshape(DA_V_DIM, 1), tq=min(s, 512),
                           lam_init=lam_init, act=act)

    t = b * s
    cin = jnp.zeros((N_EXPERTS, LANES), F32)
    x1, h2, ri, rw, cnt = _mix(hm, ad, x, g_a, sc_f, sh_f, p["wo1"], p["wo2"], p["ln1_g"], p["ln1_b"],
                               p["wrt"], p["brt"], cin, bb=bb, ts=ts, alpha=alpha)
    row_tile = ROW_TILE
    n_tiles = (2 * t) // row_tile + N_EXPERTS
    n_tiles_pad = -(-n_tiles // LANES) * LANES
    pos, tiles = _plan(ri, cnt, row_tile=row_tile, n_tiles_pad=n_tiles_pad)
    pos_flat = pos[0:2].reshape(2 * t)
    xs0 = jnp.zeros((n_tiles * row_tile, d), F32)
    xs = _scatter(pos_flat, h2, xs0, tm=min(t, 1024))
    ys = _experts(tiles[0, :n_tiles], tiles[1, 0:1], xs, p["w_gate"], p["w_up"], p["w_down"], row_tile=row_tile)
    cts = ts if sample else min(s, 256)
    y = _combine(pos_flat, x1, g_f, rw, ys, p["ln2_g"], p["ln2_b"], bb=bb, ts=cts, alpha=alpha)
    return y, kd, vd, c1, n1, m1, conv_new


def _layer_params(l, w_in, w_conv, b_conv, w_mq, w_mk, b_i, b_f, gn_m, skip_m, lam_q1, lam_k1, lam_q2, lam_k2,
                  subln_g, w_out, ln1_g, ln1_b, w_rg, b_rg, w_re, b_re, w_gate, w_up, w_down, ln2_g, ln2_b):
    w3 = 3 * ML_WIDTH
    n_gate = 2 * ML_HEADS
    wi = w_in[l]
    d = wi.shape[0]
    wg = jnp.zeros((d, GATE_LANES), F32).at[:, :n_gate].set(wi[:, w3:w3 + n_gate])
    bg = jnp.zeros((1, GATE_LANES), F32).at[0, :ML_HEADS].set(b_i[l]).at[0, ML_HEADS:n_gate].set(b_f[l])
    wrt = jnp.zeros((ROUTER_ROWS, d), F32).at[:N_GROUPS].set(w_rg[l].T).at[8:].set(w_re[l].T)
    brt = jnp.zeros((ROUTER_ROWS, 1), F32).at[:N_GROUPS, 0].set(b_rg[l]).at[8:, 0].set(b_re[l])
    return {
        "wa": wi[:, :w3].astype(BF16),
        "wg": wg.astype(BF16),
        "wb": wi[:, w3 + n_gate:].astype(BF16),
        "bg": bg,
        "w_conv": w_conv[l],
        "b_conv": b_conv[l][None, :],
        "wqk": jnp.concatenate([w_mq[l], w_mk[l]], axis=-1).astype(BF16),
        "gn_m": gn_m[l], "skip_m": skip_m[l],
        "lams": (lam_q1[l][None, :], lam_k1[l][None, :], lam_q2[l][None, :], lam_k2[l][None, :]),
        "subln_g": subln_g[l][None, :],
        "wo1": w_out[l][:ML_WIDTH].astype(BF16),
        "wo2": w_out[l][ML_WIDTH:].astype(BF16),
        "ln1_g": ln1_g[l][None, :], "ln1_b": ln1_b[l][None, :],
        "wrt": wrt, "brt": brt,
        "w_gate": w_gate[l], "w_up": w_up[l], "w_down": w_down[l],
        "ln2_g": ln2_g[l][None, :], "ln2_b": ln2_b[l][None, :],
    }


def kernel(x_prompt, x_sample, cache_k, cache_v, state_C, state_n, state_m, state_conv, page_table, c_prompt, c_sample, w_ada, b_ada, w_in, w_conv, b_conv, w_mq, w_mk, b_i, b_f, gn_m, skip_m, lam_q1, lam_k1, lam_q2, lam_k2, subln_g, w_out, ln1_g, ln1_b, w_rg, b_rg, w_re, b_re, w_gate, w_up, w_down, ln2_g, ln2_b):
    depth = w_ada.shape[0]
    bp, sp, d = x_prompt.shape
    bs, ss, _ = x_sample.shape
    alpha = (2 * depth) ** 0.25
    yp, ys = x_prompt, x_sample
    outs_p = [[] for _ in range(6)]
    outs_s = [[] for _ in range(6)]
    c_all = jnp.concatenate([c_prompt, c_sample], axis=0)
    for l in range(depth):
        p = _layer_params(l, w_in, w_conv, b_conv, w_mq, w_mk, b_i, b_f, gn_m, skip_m, lam_q1, lam_k1, lam_q2,
                          lam_k2, subln_g, w_out, ln1_g, ln1_b, w_rg, b_rg, w_re, b_re, w_gate, w_up, w_down,
                          ln2_g, ln2_b)
        lam_init = 0.8 - 0.6 * math.exp(-0.3 * l)
        mod = _ada(c_all, w_ada[l], b_ada[l][None, :])
        mod_p = tuple(mod[:bp, None, i * d:(i + 1) * d] for i in range(6))
        mod_s = tuple(mod[bp:, None, i * d:(i + 1) * d] for i in range(6))
        h, hd = ML_HEADS, ML_HEAD_DIM
        res_p = _layer(yp, mod_p, p, lam_init, alpha,
                       jnp.zeros((bp, CONV_W - 1, ML_WIDTH), F32), jnp.zeros((bp, h, hd, hd), F32),
                       jnp.zeros((bp, h, hd), F32), jnp.zeros((bp, 1, h), F32), None, sample=False)
        n_pool, page = cache_k.shape[1], cache_k.shape[2]
        paged = (cache_k[l].reshape(n_pool, page * DA_HEADS, DA_V_DIM),
                 cache_v[l].reshape(n_pool, page * DA_HEADS, DA_V_DIM), page_table)
        res_s = _layer(ys, mod_s, p, lam_init, alpha, state_conv[l], state_C[l], state_n[l],
                       state_m[l][:, None, :], paged, sample=True)
        yp, ys = res_p[0], res_s[0]
        for outs, res, nb, ns in ((outs_p, res_p, bp, sp), (outs_s, res_s, bs, ss)):
            outs[0].append(res[1].reshape(nb, ns, DA_HEADS, 2 * DA_HEAD_DIM))
            outs[1].append(res[2].reshape(nb, ns, DA_HEADS, DA_V_DIM))
            outs[2].append(res[3])
            outs[3].append(res[4])
            outs[4].append(res[5].reshape(nb, h))
            outs[5].append(res[6])
    return (yp, ys, *(jnp.stack(o) for o in outs_p), *(jnp.stack(o) for o in outs_s))
```

```python
import functools
import math

import jax
import jax.numpy as jnp
from jax import lax
from jax.experimental import pallas as pl
from jax.experimental.pallas import tpu as pltpu

F32 = jnp.float32
BF16 = jnp.bfloat16
I32 = jnp.int32
HIGHEST = lax.Precision.HIGHEST

LN_EPS = 1e-5
ML_HEADS = 4
ML_HEAD_DIM = 128
ML_WIDTH = ML_HEADS * ML_HEAD_DIM
CONV_W = 4
DA_HEADS = 4
DA_HEAD_DIM = 64
DA_V_DIM = 2 * DA_HEAD_DIM
DA_WIDTH = DA_HEADS * DA_V_DIM
N_GROUPS = 4
EXPERTS_PER_GROUP = 8
N_EXPERTS = N_GROUPS * EXPERTS_PER_GROUP
GATE_LANES = 128
ROUTER_ROWS = 8 + N_EXPERTS
SUBLANES = 8
LANES = 128
ROW_TILE = 512
SAMPLE_ROW_TILE = 128
PAGES_PER_STEP = 16
RUN_SHIFT = 3
RUN_ALIGN = 1 << RUN_SHIFT
TABLE_N = N_EXPERTS
TABLE_LOCAL = 2 * N_EXPERTS
VMEM_LIMIT = 56 * 2 ** 20

_NT = (((1,), (1,)), ((), ()))
LOG2E = 1.4426950408889634


def _params(sem, vmem=None):
    return pltpu.CompilerParams(dimension_semantics=sem, vmem_limit_bytes=vmem)


def _sigmoid(x):
    return jax.nn.sigmoid(x)


def _log_sigmoid(x):
    return jnp.minimum(x, 0.0) - jnp.log1p(jnp.exp(-jnp.abs(x)))


def _ceil_to(x, k):
    return jnp.floor((x + (k - 1)) * (1.0 / k)) * k


def _cumsum_rows(x):
    n = x.shape[0]
    row = lax.broadcasted_iota(I32, x.shape, 0)
    shift = 1
    while shift < n:
        x = x + jnp.where(row >= shift, pltpu.roll(x, shift, axis=0), 0.0)
        shift *= 2
    return x


def _cumsum_lanes(x, n):
    lane = lax.broadcasted_iota(I32, x.shape, 1)
    shift = 1
    while shift < n:
        x = x + jnp.where(lane >= shift, pltpu.roll(x, shift, axis=1), 0.0)
        shift *= 2
    return x


def _layernorm_rows(y, g, b):
    mu = jnp.mean(y, axis=-1, keepdims=True)
    d = y - mu
    var = jnp.mean(d * d, axis=-1, keepdims=True)
    return d * lax.rsqrt(var + LN_EPS) * g + b


def _ada_kernel(c_ref, w_ref, b_ref, o_ref):
    c = c_ref[...]
    s = c * _sigmoid(c)
    o_ref[...] = jnp.dot(s, w_ref[...], preferred_element_type=F32, precision=HIGHEST) + b_ref[...]


def _ada(c, w, b):
    bc, d = c.shape
    n = w.shape[1]
    tn = 512
    return pl.pallas_call(
        _ada_kernel,
        grid=(n // tn,),
        in_specs=[pl.BlockSpec((bc, d), lambda j: (0, 0)),
                  pl.BlockSpec((d, tn), lambda j: (0, j)),
                  pl.BlockSpec((1, tn), lambda j: (0, j))],
        out_specs=pl.BlockSpec((bc, tn), lambda j: (0, j)),
        out_shape=jax.ShapeDtypeStruct((bc, n), F32),
        compiler_params=_params(("arbitrary",)),
    )(c, w, b)


def _proj_kernel(x_ref, sc_ref, sh_ref, wa_ref, wg_ref, wb_ref, bg_ref, cbuf_ref, wconv_ref, bconv_ref, wqk_ref,
                 ca_ref, q_ref, k_ref, v_ref, om_ref, g_ref, qd_ref, kd_ref, vd_ref, kdn_ref, vdn_ref, cnew_ref,
                 ext_ref, *, bb, ts):
    si = pl.program_id(1)
    m = bb * ts
    d = x_ref.shape[-1]
    h = (x_ref[...] * (1.0 + sc_ref[...]) + sh_ref[...]).reshape(m, d).astype(BF16)
    pa = jnp.dot(h, wa_ref[...], preferred_element_type=F32)
    pb = jnp.dot(h, wb_ref[...], preferred_element_type=F32)
    g = jnp.dot(h, wg_ref[...], preferred_element_type=F32) + bg_ref[...]
    g_ref[...] = g.reshape(bb, ts, GATE_LANES)
    w = ML_WIDTH
    v_ref[...] = pa[:, w:2 * w].reshape(bb, ts, w).astype(v_ref.dtype)
    om_ref[...] = pa[:, 2 * w:3 * w].reshape(bb, ts, w)
    qd_ref[...] = pb[:, 0:w].reshape(bb, ts, w).astype(qd_ref.dtype)
    kd = pb[:, w:2 * w]
    vd = pb[:, 2 * w:3 * w]
    kd_ref[...] = kd.reshape(bb, ts, w).astype(kd_ref.dtype)
    vd_ref[...] = vd.reshape(bb, ts, w).astype(vd_ref.dtype)
    for hh in range(DA_HEADS):
        cols = slice(hh * DA_V_DIM, (hh + 1) * DA_V_DIM)
        kdn_ref[:, pl.ds(hh, ts, stride=DA_HEADS), :] = kd[:, cols].reshape(bb, ts, DA_V_DIM)
        vdn_ref[:, pl.ds(hh, ts, stride=DA_HEADS), :] = vd[:, cols].reshape(bb, ts, DA_V_DIM)

    @pl.when(si == 0)
    def _():
        ext_ref[:, 5:8, :] = cbuf_ref[...]

    @pl.when(si > 0)
    def _():
        ext_ref[:, 0:8, :] = ext_ref[:, ts:ts + 8, :]

    ext_ref[:, 8:8 + ts, :] = pa[:, 0:w].reshape(bb, ts, w)
    y = bconv_ref[...]
    for j in range(CONV_W):
        y = y + wconv_ref[j:j + 1, :] * ext_ref[:, 5 + j:5 + j + ts, :]
    ca = y * _sigmoid(y)
    ca_ref[...] = ca
    cnew_ref[...] = ext_ref[:, ts + 5:ts + 8, :]

    ca2 = ca.reshape(m, w)
    hd = ML_HEAD_DIM
    for hh in range(ML_HEADS):
        qk = jnp.dot(ca2[:, hh * hd:(hh + 1) * hd].astype(BF16), wqk_ref[hh], preferred_element_type=F32)
        q_ref[:, :, hh * hd:(hh + 1) * hd] = (qk[:, 0:hd] * (hd ** -0.5)).reshape(bb, ts, hd).astype(q_ref.dtype)
        k_ref[:, :, hh * hd:(hh + 1) * hd] = qk[:, hd:2 * hd].reshape(bb, ts, hd).astype(k_ref.dtype)


def _proj(x, sc, sh, wa, wg, wb, bg, cbuf, wconv, bconv, wqk, *, bb, ts, act):
    b, s, d = x.shape
    w = ML_WIDTH
    grid = (b // bb, s // ts)
    tok = lambda n: pl.BlockSpec((bb, ts, n), lambda i, j: (i, j, 0))
    per_b = lambda r, n: pl.BlockSpec((bb, r, n), lambda i, j: (i, 0, 0))
    const = lambda shape: pl.BlockSpec(shape, lambda i, j: (0,) * len(shape))
    out_shape = (
        jax.ShapeDtypeStruct((b, s, w), F32),
        jax.ShapeDtypeStruct((b, s, w), act),
        jax.ShapeDtypeStruct((b, s, w), act),
        jax.ShapeDtypeStruct((b, s, w), act),
        jax.ShapeDtypeStruct((b, s, w), F32),
        jax.ShapeDtypeStruct((b, s, GATE_LANES), F32),
        jax.ShapeDtypeStruct((b, s, w), act),
        jax.ShapeDtypeStruct((b, s, w), act),
        jax.ShapeDtypeStruct((b, s, w), act),
        jax.ShapeDtypeStruct((b, s * DA_HEADS, DA_V_DIM), F32),
        jax.ShapeDtypeStruct((b, s * DA_HEADS, DA_V_DIM), F32),
        jax.ShapeDtypeStruct((b, CONV_W - 1, w), F32),
    )
    cache_rows = pl.BlockSpec((bb, ts * DA_HEADS, DA_V_DIM), lambda i, j: (i, j, 0))
    out_specs = (tok(w), tok(w), tok(w), tok(w), tok(w), tok(GATE_LANES), tok(w), tok(w), tok(w),
                 cache_rows, cache_rows, per_b(CONV_W - 1, w))
    return pl.pallas_call(
        functools.partial(_proj_kernel, bb=bb, ts=ts),
        grid=grid,
        in_specs=[tok(d), per_b(1, d), per_b(1, d), const(wa.shape), const(wg.shape), const(wb.shape),
                  const(bg.shape), per_b(CONV_W - 1, w), const(wconv.shape), const(bconv.shape),
                  const(wqk.shape)],
        out_specs=out_specs,
        out_shape=out_shape,
        scratch_shapes=[pltpu.VMEM((bb, ts + 8, w), F32)],
        compiler_params=_params(("arbitrary", "arbitrary"), VMEM_LIMIT),
    )(x, sc, sh, wa, wg, wb, bg, cbuf, wconv, bconv, wqk)


def _mlstm_kernel(q_ref, k_ref, v_ref, g_ref, ca_ref, om_ref, c0_ref, n0_ref, m0_ref, gn_ref, skip_ref,
                  hm_ref, c1_ref, n1_ref, m1_ref, c_s, n_s, m_s, *, chunk):
    si = pl.program_id(1)
    ln = chunk
    hd = ML_HEAD_DIM

    @pl.when(si == 0)
    def _():
        c_s[...] = c0_ref[0]
        n_s[...] = n0_ref[0]
        m_s[...] = m0_ref[0]

    g = g_ref[0]
    row = lax.broadcasted_iota(I32, (ln, GATE_LANES), 0)
    bc = _log_sigmoid(g)
    shift = 1
    while shift < ln:
        bc = bc + jnp.where(row >= shift, pltpu.roll(bc, shift, axis=0), 0.0)
        shift *= 2
    g_t = g.T
    bc_t = bc.T
    causal = lax.broadcasted_iota(I32, (ln, ln), 0) >= lax.broadcasted_iota(I32, (ln, ln), 1)

    for hh in range(ML_HEADS):
        cols = slice(hh * hd, (hh + 1) * hd)
        qb = q_ref[0, :, cols].astype(BF16)
        kf = k_ref[0, :, cols].astype(F32)
        kb = kf.astype(BF16)
        vb = v_ref[0, :, cols].astype(BF16)
        b_col = bc[:, ML_HEADS + hh:ML_HEADS + hh + 1]
        i_col = g[:, hh:hh + 1]
        b_row = bc_t[ML_HEADS + hh:ML_HEADS + hh + 1, :]
        i_row = g_t[hh:hh + 1, :]
        m_prev = m_s[:, hh:hh + 1]
        log_d = jnp.where(causal, b_col - b_row + i_row, -jnp.inf)
        inter = b_col + m_prev
        m_t = jnp.maximum(inter, jnp.max(log_d, axis=-1, keepdims=True))
        w_inter = jnp.exp(inter - m_t)
        s = lax.dot_general(qb, kb, _NT, preferred_element_type=F32) * jnp.exp(log_d - m_t)
        c_old = c_s[hh]
        n_old = n_s[hh:hh + 1, :]
        num = (w_inter * jnp.dot(qb, c_old.astype(BF16), preferred_element_type=F32)
               + jnp.dot(s.astype(BF16), vb, preferred_element_type=F32))
        den = (w_inter * jnp.sum(qb.astype(F32) * n_old, axis=-1, keepdims=True)
               + jnp.sum(s, axis=-1, keepdims=True))
        hc = num / jnp.maximum(jnp.abs(den), jnp.exp(-m_t))
        m_new = m_t[ln - 1:ln, :]
        b_last = b_col[ln - 1:ln, :]
        w_state = jnp.exp(b_last + m_prev - m_new)
        kw = jnp.exp(b_last - b_col + i_col - m_new) * kf
        c_s[hh] = w_state * c_old + jnp.dot(kw.T.astype(BF16), vb, preferred_element_type=F32)
        n_s[hh:hh + 1, :] = w_state * n_old + jnp.sum(kw, axis=0, keepdims=True)
        m_s[:, hh:hh + 1] = m_new
        mu = jnp.mean(hc, axis=-1, keepdims=True)
        dlt = hc - mu
        var = jnp.mean(dlt * dlt, axis=-1, keepdims=True)
        hn = dlt * lax.rsqrt(var + LN_EPS) * gn_ref[hh:hh + 1, :]
        out = (hn + skip_ref[hh:hh + 1, :] * ca_ref[0, :, cols]) * _sigmoid(om_ref[0, :, cols])
        hm_ref[0, :, cols] = out.astype(hm_ref.dtype)

    @pl.when(si == pl.num_programs(1) - 1)
    def _():
        c1_ref[0] = c_s[...]
        n1_ref[0] = n_s[...]
        m1_ref[0] = m_s[...]


def _mlstm(q, k, v, g, ca, om, c0, n0, m0, gn, skip, *, chunk, act):
    b, s, w = q.shape
    h, hd = ML_HEADS, ML_HEAD_DIM
    tok = lambda n: pl.BlockSpec((1, chunk, n), lambda i, j: (i, j, 0))
    c_spec = pl.BlockSpec((1, h, hd, hd), lambda i, j: (i, 0, 0, 0))
    n_spec = pl.BlockSpec((1, h, hd), lambda i, j: (i, 0, 0))
    m_spec = pl.BlockSpec((1, 1, h), lambda i, j: (i, 0, 0))
    hw_spec = pl.BlockSpec((h, hd), lambda i, j: (0, 0))
    return pl.pallas_call(
        functools.partial(_mlstm_kernel, chunk=chunk),
        grid=(b, s // chunk),
        in_specs=[tok(w), tok(w), tok(w), tok(GATE_LANES), tok(w), tok(w), c_spec, n_spec, m_spec,
                  hw_spec, hw_spec],
        out_specs=(tok(w), c_spec, n_spec, m_spec),
        out_shape=(jax.ShapeDtypeStruct((b, s, w), act),
                   jax.ShapeDtypeStruct((b, h, hd, hd), F32),
                   jax.ShapeDtypeStruct((b, h, hd), F32),
                   jax.ShapeDtypeStruct((b, 1, h), F32)),
        scratch_shapes=[pltpu.VMEM((h, hd, hd), F32), pltpu.VMEM((h, hd), F32), pltpu.VMEM((1, h), F32)],
        compiler_params=_params(("arbitrary", "arbitrary"), VMEM_LIMIT),
    )(q, k, v, g, ca, om, c0, n0, m0, gn, skip)


def _lam(lq1_ref, lk1_ref, lq2_ref, lk2_ref, lam_init):
    a = jnp.sum(lq1_ref[...] * lk1_ref[...], axis=-1, keepdims=True)
    b = jnp.sum(lq2_ref[...] * lk2_ref[...], axis=-1, keepdims=True)
    return jnp.exp(a) - jnp.exp(b) + lam_init


def _head_rms(o, sg, lam_init):
    return o * lax.rsqrt(jnp.mean(o * o, axis=-1, keepdims=True) + LN_EPS) * sg * (1.0 - lam_init)


def _softmax_update(s, vt, m_ref, l_ref, a_ref):
    m_old = m_ref[...]
    m_new = jnp.maximum(m_old, jnp.max(s, axis=-1, keepdims=True))
    alpha = jnp.exp2(m_old - m_new)
    p = jnp.exp2(s - m_new)
    l_ref[...] = alpha * l_ref[...] + jnp.sum(p, axis=-1, keepdims=True)
    a_ref[...] = alpha * a_ref[...] + jnp.dot(p.astype(BF16), vt, preferred_element_type=F32)
    m_ref[...] = m_new


def _dattn_kernel(lq1_ref, lk1_ref, lq2_ref, lk2_ref, sgc_ref, q_ref, k_ref, v_ref, o_ref,
                  vt_s, m1, l1, a1, m2, l2, a2, *, tq, lam_init):
    qi = pl.program_id(2)
    n_chunks = k_ref.shape[1] // tq

    @pl.when(qi == 0)
    def _():
        for c in range(n_chunks):
            vt_s[c] = v_ref[0, c * tq:(c + 1) * tq, :].astype(F32).T.astype(BF16)

    lam = _lam(lq1_ref, lk1_ref, lq2_ref, lk2_ref, lam_init)
    q = q_ref[0].astype(F32) * (DA_HEAD_DIM ** -0.5 * LOG2E)
    lane = lax.broadcasted_iota(I32, q.shape, 1)
    q1 = jnp.where(lane < DA_HEAD_DIM, q, 0.0).astype(BF16)
    q2 = jnp.where(lane >= DA_HEAD_DIM, q, 0.0).astype(BF16)
    for m_ref, l_ref, a_ref in ((m1, l1, a1), (m2, l2, a2)):
        m_ref[...] = jnp.full(m_ref.shape, -jnp.inf, F32)
        l_ref[...] = jnp.zeros(l_ref.shape, F32)
        a_ref[...] = jnp.zeros(a_ref.shape, F32)
    key_le_query = (lax.broadcasted_iota(I32, (tq, tq), 0) <= lax.broadcasted_iota(I32, (tq, tq), 1))

    def kv_step(j, masked):
        kt = k_ref[0, pl.ds(pl.multiple_of(j * tq, tq), tq), :].astype(BF16)
        vt = vt_s[j]
        for qz, m_ref, l_ref, a_ref in ((q1, m1, l1, a1), (q2, m2, l2, a2)):
            st = lax.dot_general(kt, qz, _NT, preferred_element_type=F32)
            if masked:
                st = jnp.where(key_le_query, st, -jnp.inf)
            m_old = m_ref[...]
            m_new = jnp.maximum(m_old, jnp.max(st, axis=0, keepdims=True))
            alpha = jnp.exp2(m_old - m_new)
            p = jnp.exp2(st - m_new)
            l_ref[...] = alpha * l_ref[...] + jnp.sum(p, axis=0, keepdims=True)
            a_ref[...] = alpha * a_ref[...] + jnp.dot(vt, p.astype(BF16), preferred_element_type=F32)
            m_ref[...] = m_new

    def body(j, carry):
        kv_step(j, False)
        return carry

    lax.fori_loop(0, qi, body, 0)
    kv_step(qi, True)
    ot = a1[...] / l1[...] - lam * (a2[...] / l2[...])
    ot = ot * lax.rsqrt(jnp.mean(ot * ot, axis=0, keepdims=True) + LN_EPS) * sgc_ref[...] * (1.0 - lam_init)
    o_ref[0] = ot.T.astype(o_ref.dtype)


def _dattn_prompt(qd, kd, vd, lams, sgc, *, tq, lam_init, act):
    b, s, w = qd.shape
    dv = DA_V_DIM
    lam_spec = pl.BlockSpec((1, DA_HEAD_DIM), lambda i, h, j: (0, 0))
    stat = pltpu.VMEM((1, tq), F32)
    acc = pltpu.VMEM((dv, tq), F32)
    return pl.pallas_call(
        functools.partial(_dattn_kernel, tq=tq, lam_init=lam_init),
        grid=(b, DA_HEADS, s // tq),
        in_specs=[lam_spec, lam_spec, lam_spec, lam_spec,
                  pl.BlockSpec((dv, 1), lambda i, h, j: (0, 0)),
                  pl.BlockSpec((1, tq, dv), lambda i, h, j: (i, j, h)),
                  pl.BlockSpec((1, s, dv), lambda i, h, j: (i, 0, h)),
                  pl.BlockSpec((1, s, dv), lambda i, h, j: (i, 0, h))],
        out_specs=pl.BlockSpec((1, tq, dv), lambda i, h, j: (i, j, h)),
        out_shape=jax.ShapeDtypeStruct((b, s, w), act),
        scratch_shapes=[pltpu.VMEM((s // tq, dv, tq), BF16), stat, stat, acc, stat, stat, acc],
        compiler_params=_params(("arbitrary", "arbitrary", "arbitrary"), VMEM_LIMIT),
    )(*lams, sgc, qd, kd, vd)


def _sattn_kernel(pt_ref, lq1_ref, lk1_ref, lq2_ref, lk2_ref, sg_ref, q_ref, kn_ref, vn_ref, *rest,
                  pps, sd, lam_init):
    del pt_ref
    k_refs = rest[0:pps]
    v_refs = rest[pps:2 * pps]
    o_ref, qx_s, m_s, l_s, a_s = rest[2 * pps:]
    j = pl.program_id(1)
    nq = DA_HEADS * sd
    nrow = 2 * nq
    dv = DA_V_DIM

    @pl.when(j == 0)
    def _():
        q = q_ref[0].astype(F32) * (DA_HEAD_DIM ** -0.5 * LOG2E)
        qh = jnp.concatenate([q[:, hh * dv:(hh + 1) * dv] for hh in range(DA_HEADS)], axis=0)
        lane = lax.broadcasted_iota(I32, qh.shape, 1)
        qx_s[...] = jnp.concatenate([jnp.where(lane < DA_HEAD_DIM, qh, 0.0),
                                     jnp.where(lane >= DA_HEAD_DIM, qh, 0.0)], axis=0).astype(BF16)
        m_s[...] = jnp.full(m_s.shape, -jnp.inf, F32)
        l_s[...] = jnp.zeros(l_s.shape, F32)
        a_s[...] = jnp.zeros(a_s.shape, F32)

    qx = qx_s[...]
    n_keys = k_refs[0].shape[1]
    r = lax.broadcasted_iota(I32, (nrow, n_keys), 0)
    c = lax.broadcasted_iota(I32, (nrow, n_keys), 1)
    bias = jnp.where(((r // sd) % DA_HEADS) == (c % DA_HEADS), 0.0, -jnp.inf)
    scores = [lax.dot_general(qx, k_refs[p][0].astype(BF16), _NT, preferred_element_type=F32) + bias
              for p in range(pps)]
    m_old = m_s[...]
    m_new = m_old
    for sp in scores:
        m_new = jnp.maximum(m_new, jnp.max(sp, axis=-1, keepdims=True))
    alpha = jnp.exp2(m_old - m_new)
    lsum = alpha * l_s[...]
    acc = alpha * a_s[...]
    for p, sp in enumerate(scores):
        pp = jnp.exp2(sp - m_new)
        lsum = lsum + jnp.sum(pp, axis=-1, keepdims=True)
        acc = acc + jnp.dot(pp.astype(BF16), v_refs[p][0].astype(BF16), preferred_element_type=F32)
    l_s[...] = lsum
    a_s[...] = acc
    m_s[...] = m_new

    @pl.when(j == pl.num_programs(1) - 1)
    def _():
        pad = jnp.zeros((LANES - nq, dv), F32)
        kn = jnp.concatenate([kn_ref[0], pad], axis=0).astype(BF16)
        vn = jnp.concatenate([vn_ref[0], pad], axis=0).astype(BF16)
        sn = lax.dot_general(qx, kn, _NT, preferred_element_type=F32)
        rn = lax.broadcasted_iota(I32, sn.shape, 0)
        cn = lax.broadcasted_iota(I32, sn.shape, 1)
        valid = (cn < nq) & ((cn % DA_HEADS) == ((rn // sd) % DA_HEADS)) & ((cn // DA_HEADS) <= (rn % sd))
        _softmax_update(jnp.where(valid, sn, -jnp.inf), vn, m_s, l_s, a_s)
        lam = _lam(lq1_ref, lk1_ref, lq2_ref, lk2_ref, lam_init)
        o = a_s[0:nq, :] / l_s[0:nq, :] - lam * (a_s[nq:nrow, :] / l_s[nq:nrow, :])
        o = _head_rms(o, sg_ref[...], lam_init)
        for hh in range(DA_HEADS):
            o_ref[0, :, hh * dv:(hh + 1) * dv] = o[hh * sd:(hh + 1) * sd, :].astype(o_ref.dtype)


def _dattn_sample(qd, kn, vn, cache_k, cache_v, page_table, lams, sg, *, pps, lam_init):
    b, sd, w = qd.shape
    n_pages = page_table.shape[1]
    n_keys, dv = cache_k.shape[1], cache_k.shape[2]
    h = DA_HEADS
    lam_spec = pl.BlockSpec((1, DA_HEAD_DIM), lambda i, j, pt: (0, 0))
    tok = pl.BlockSpec((1, sd, w), lambda i, j, pt: (i, 0, 0))
    new_rows = pl.BlockSpec((1, sd * h, dv), lambda i, j, pt: (i, 0, 0))

    nrow = 2 * h * sd

    def page_spec(p):
        return pl.BlockSpec((1, n_keys, dv), lambda i, j, pt: (pt[i, j * pps + p], 0, 0))

    grid_spec = pltpu.PrefetchScalarGridSpec(
        num_scalar_prefetch=1,
        grid=(b, n_pages // pps),
        in_specs=[lam_spec, lam_spec, lam_spec, lam_spec,
                  pl.BlockSpec((1, dv), lambda i, j, pt: (0, 0)),
                  tok, new_rows, new_rows]
                 + [page_spec(p) for p in range(pps)] + [page_spec(p) for p in range(pps)],
        out_specs=tok,
        scratch_shapes=[pltpu.VMEM((nrow, dv), BF16), pltpu.VMEM((nrow, 1), F32), pltpu.VMEM((nrow, 1), F32),
                        pltpu.VMEM((nrow, dv), F32)],
    )
    return pl.pallas_call(
        functools.partial(_sattn_kernel, pps=pps, sd=sd, lam_init=lam_init),
        grid_spec=grid_spec,
        out_shape=jax.ShapeDtypeStruct((b, sd, w), F32),
        compiler_params=_params(("arbitrary", "arbitrary"), VMEM_LIMIT),
    )(page_table, *lams, sg, qd, kn, vn, *([cache_k] * pps), *([cache_v] * pps))


def _mix_kernel(hm_ref, ad_ref, x_ref, ga_ref, scf_ref, shf_ref, wo1_ref, wo2_ref, g1_ref, b1_ref,
                wrt_ref, brt_ref, x1_ref, h2_ref, ri_ref, rw_ref, cnt_ref, cb_ref, cn_ref, carry_s, carry_row_s,
                *, bb, ts, alpha):
    m = bb * ts
    d = x_ref.shape[-1]

    @pl.when((pl.program_id(0) == 0) & (pl.program_id(1) == 0))
    def _():
        carry_s[...] = jnp.zeros(carry_s.shape, F32)
        carry_row_s[...] = jnp.zeros(carry_row_s.shape, F32)

    hm = hm_ref[...].reshape(m, ML_WIDTH).astype(BF16)
    ad = ad_ref[...].reshape(m, DA_WIDTH).astype(BF16)
    mixed = (jnp.dot(hm, wo1_ref[...], preferred_element_type=F32)
             + jnp.dot(ad, wo2_ref[...], preferred_element_type=F32))
    y = alpha * x_ref[...] + (1.0 + ga_ref[...]) * mixed.reshape(bb, ts, d)
    x1 = _layernorm_rows(y, g1_ref[...], b1_ref[...])
    x1_ref[...] = x1
    h2 = (x1 * (1.0 + scf_ref[...]) + shf_ref[...]).reshape(m, d)
    h2_ref[...] = h2

    lt = lax.dot_general(wrt_ref[...], h2, _NT, preferred_element_type=F32, precision=HIGHEST) + brt_ref[...]
    gl = lt[0:N_GROUPS]
    gmax = jnp.max(gl, axis=0, keepdims=True)
    r4 = lax.broadcasted_iota(I32, gl.shape, 0)
    gidx = jnp.min(jnp.where(gl == gmax, r4, N_GROUPS), axis=0, keepdims=True)
    gp = 1.0 / jnp.sum(jnp.exp(gl - gmax), axis=0, keepdims=True)
    epg = EXPERTS_PER_GROUP
    esel = lt[8 + (N_GROUPS - 1) * epg:8 + N_GROUPS * epg]
    for grp in range(N_GROUPS - 2, -1, -1):
        esel = jnp.where(gidx == grp, lt[8 + grp * epg:8 + (grp + 1) * epg], esel)
    r8 = lax.broadcasted_iota(I32, esel.shape, 0)
    t1 = jnp.max(esel, axis=0, keepdims=True)
    i1 = jnp.min(jnp.where(esel == t1, r8, epg), axis=0, keepdims=True)
    rest = jnp.where(r8 == i1, -jnp.inf, esel)
    t2 = jnp.max(rest, axis=0, keepdims=True)
    i2 = jnp.min(jnp.where(rest == t2, r8, epg), axis=0, keepdims=True)
    z = jnp.exp(t2 - t1)
    w1 = gp / (1.0 + z)
    w2 = gp * z / (1.0 + z)
    e0 = gidx * epg + i1
    e1 = gidx * epg + i2

    r32 = lax.broadcasted_iota(I32, (N_EXPERTS, m), 0)
    hit0 = r32 == e0
    hit1 = r32 == e1
    onehot = jnp.where(hit0, 1.0, jnp.where(hit1, 1.0, 0.0))
    onehot_b = onehot.astype(BF16)
    before = (lax.broadcasted_iota(I32, (m, m), 0) < lax.broadcasted_iota(I32, (m, m), 1))
    prefix = jnp.dot(onehot_b, jnp.where(before, 1.0, 0.0).astype(BF16), preferred_element_type=F32)
    cnt_col = jnp.sum(onehot, axis=1, keepdims=True)
    run = jnp.broadcast_to(_ceil_to(cnt_col, RUN_ALIGN), (N_EXPERTS, LANES))
    start = _cumsum_rows(run) - run
    slot = prefix + start[:, 0:1]
    slot0 = jnp.sum(jnp.where(hit0, slot, 0.0), axis=0, keepdims=True)
    slot1 = jnp.sum(jnp.where(hit1, slot, 0.0), axis=0, keepdims=True)
    rr = lax.broadcasted_iota(I32, (SUBLANES, m), 0)
    ri_ref[...] = jnp.where(rr == 0, slot0, jnp.where(rr == 1, slot1, 0.0)).astype(I32)
    rw_ref[...] = jnp.where(rr == 0, w1, jnp.where(rr == 1, w2, jnp.where(rr == 2, slot0,
                            jnp.where(rr == 3, slot1, 0.0))))
    padded_hot = jnp.concatenate([onehot_b, jnp.zeros((LANES - N_EXPERTS, m), BF16)], axis=0)
    cnt_row = lax.dot_general(jnp.ones((SUBLANES, m), BF16), padded_hot, _NT, preferred_element_type=F32)
    cb_ref[...] = carry_row_s[...]
    cn_ref[...] = cnt_row
    carry_row_s[...] = carry_row_s[...] + _ceil_to(cnt_row, RUN_ALIGN)
    carry_s[...] = carry_s[...] + run
    cnt_ref[...] = carry_s[...]


def _mix(hm, ad, x, ga, scf, shf, wo1, wo2, g1, b1, wrt, brt, *, bb, ts, alpha):
    b, s, d = x.shape
    m = bb * ts
    t = b * s
    ns = s // ts
    n_tok_tiles = t // m
    tok = lambda n: pl.BlockSpec((bb, ts, n), lambda i, j: (i, j, 0))
    per_b = pl.BlockSpec((bb, 1, d), lambda i, j: (i, 0, 0))
    const = lambda shape: pl.BlockSpec(shape, lambda i, j: (0,) * len(shape))
    lin = pl.BlockSpec((SUBLANES, m), lambda i, j: (0, i * ns + j))
    per_tile = pl.BlockSpec((SUBLANES, LANES), lambda i, j: (i * ns + j, 0))
    cnt_shape = (N_EXPERTS, LANES)
    return pl.pallas_call(
        functools.partial(_mix_kernel, bb=bb, ts=ts, alpha=alpha),
        grid=(b // bb, ns),
        in_specs=[tok(ML_WIDTH), tok(DA_WIDTH), tok(d), per_b, per_b, per_b, const(wo1.shape), const(wo2.shape),
                  const(g1.shape), const(b1.shape), const(wrt.shape), const(brt.shape)],
        out_specs=(tok(d), pl.BlockSpec((m, d), lambda i, j: (i * ns + j, 0)), lin, lin, const(cnt_shape),
                   per_tile, per_tile),
        out_shape=(jax.ShapeDtypeStruct((b, s, d), F32),
                   jax.ShapeDtypeStruct((t, d), F32),
                   jax.ShapeDtypeStruct((SUBLANES, t), I32),
                   jax.ShapeDtypeStruct((SUBLANES, t), F32),
                   jax.ShapeDtypeStruct(cnt_shape, F32),
                   jax.ShapeDtypeStruct((n_tok_tiles * SUBLANES, LANES), F32),
                   jax.ShapeDtypeStruct((n_tok_tiles * SUBLANES, LANES), F32)),
        scratch_shapes=[pltpu.VMEM(cnt_shape, F32), pltpu.VMEM((SUBLANES, LANES), F32)],
        compiler_params=_params(("arbitrary", "arbitrary"), VMEM_LIMIT),
    )(hm, ad, x, ga, scf, shf, wo1, wo2, g1, b1, wrt, brt)


def _plan_kernel(cnt_ref, cb_ref, cn_ref, tab_ref, tile_ref, tail_ref, *, row_tile):
    padded = _ceil_to(cnt_ref[...], row_tile)
    ends = _cumsum_rows(padded)
    nt = tile_ref.shape[1]
    first_row = (lax.broadcasted_iota(I32, (N_EXPERTS, nt), 1) * row_tile).astype(F32)
    done = jnp.sum(jnp.where(ends[:, 0:1] <= first_row, 1, 0), axis=0, keepdims=True)
    expert = jnp.minimum(done, N_EXPERTS - 1)
    used = (ends[N_EXPERTS - 1:N_EXPERTS, 0:1] * (1.0 / row_tile)).astype(I32)
    rt = lax.broadcasted_iota(I32, (SUBLANES, nt), 0)
    tile_ref[...] = jnp.where(rt == 0, expert, jnp.where(rt == 1, used, 0))
    n_rows = cb_ref.shape[0]
    cb = cb_ref[...]
    cn = cn_ref[...]
    lane8 = lax.broadcasted_iota(I32, (SUBLANES, LANES), 1)
    total = cb[n_rows - SUBLANES:n_rows, :] + _ceil_to(cn[n_rows - SUBLANES:n_rows, :], RUN_ALIGN)
    live8 = lane8 < N_EXPERTS
    region = jnp.where(live8, _ceil_to(total, row_tile), 0.0)
    offs8 = _cumsum_lanes(region, N_EXPERTS) - region
    offs = offs8[0:1, :]
    tail_ref[...] = (jnp.where(live8, offs8 + total, 0.0)
                     + pltpu.roll(jnp.where(live8, (region - total) * (1.0 / RUN_ALIGN), 0.0), TABLE_N, axis=1)
                     ).astype(I32)
    run = _ceil_to(cn, RUN_ALIGN)
    local = _cumsum_lanes(run, N_EXPERTS) - run
    live = lax.broadcasted_iota(I32, cb.shape, 1) < N_EXPERTS
    tab = (jnp.where(live, cb + offs, 0.0)
           + pltpu.roll(jnp.where(live, cn, 0.0), TABLE_N, axis=1)
           + pltpu.roll(jnp.where(live, local, 0.0), TABLE_LOCAL, axis=1))
    tab_ref[...] = tab.astype(I32)


def _plan(cnt, cb, cn, *, row_tile, n_tiles_pad):
    full = lambda a: pl.BlockSpec(a.shape, lambda i: (0, 0))
    return pl.pallas_call(
        functools.partial(_plan_kernel, row_tile=row_tile),
        grid=(1,),
        in_specs=[full(cnt), full(cb), full(cn)],
        out_specs=(full(cb), pl.BlockSpec((SUBLANES, n_tiles_pad), lambda i: (0, 0)),
                   pl.BlockSpec((SUBLANES, LANES), lambda i: (0, 0))),
        out_shape=(jax.ShapeDtypeStruct(cb.shape, I32), jax.ShapeDtypeStruct((SUBLANES, n_tiles_pad), I32),
                   jax.ShapeDtypeStruct((SUBLANES, LANES), I32)),
        compiler_params=_params(("arbitrary",)),
    )(cnt, cb, cn)


def _local_rows(m):
    return 2 * m + N_EXPERTS * RUN_ALIGN


def _move_runs(tab_ref, tile, chunk_copy):
    base = tile * LANES

    def per_expert(e, total):
        first = tab_ref[base + e]
        n_chunks = (tab_ref[base + TABLE_N + e] + (RUN_ALIGN - 1)) >> RUN_SHIFT
        local = tab_ref[base + TABLE_LOCAL + e]

        def per_chunk(c, carry):
            off = c * RUN_ALIGN
            chunk_copy(pl.multiple_of(local + off, RUN_ALIGN), pl.multiple_of(first + off, RUN_ALIGN)).start()
            return carry

        lax.fori_loop(0, n_chunks, per_chunk, 0)
        return total + n_chunks

    total = lax.fori_loop(0, N_EXPERTS, per_expert, 0)

    def wait_one(c, carry):
        chunk_copy(0, 0).wait()
        return carry

    lax.fori_loop(0, total, wait_one, 0)


def _scatter_kernel(tab_ref, tail_ref, nu_ref, ri_ref, h2_ref, xs_ref, xl_s, zero_s, sem, zsem, *, row_tile):
    m = h2_ref.shape[0]
    lc = xl_s.shape[0]

    @pl.when(pl.program_id(0) == 0)
    def _():
        zero_s[...] = jnp.zeros(zero_s.shape, F32)

        def zero_chunk(row):
            return pltpu.make_async_copy(zero_s, xs_ref.at[pl.ds(pl.multiple_of(row, RUN_ALIGN), RUN_ALIGN)], zsem)

        def span(first, n_chunks):
            def body(c, carry):
                zero_chunk(first + c * RUN_ALIGN).start()
                return carry
            lax.fori_loop(0, n_chunks, body, 0)
            return n_chunks

        def per_expert(e, total):
            return total + span(tail_ref[e], tail_ref[TABLE_N + e])

        total = lax.fori_loop(0, N_EXPERTS, per_expert, 0)
        used_rows = nu_ref[0] * row_tile
        total = total + span(used_rows, (xs_ref.shape[0] - used_rows) >> RUN_SHIFT)

        def wait_one(c, carry):
            zero_chunk(0).wait()
            return carry

        lax.fori_loop(0, total, wait_one, 0)

    k = lax.broadcasted_iota(I32, (lc, m), 0)
    pick = jnp.where(k == ri_ref[0:1, :], 1.0, jnp.where(k == ri_ref[1:2, :], 1.0, 0.0)).astype(BF16)
    xl_s[...] = jnp.dot(pick, h2_ref[...].astype(BF16), preferred_element_type=F32)

    def chunk_copy(local_row, sorted_row):
        return pltpu.make_async_copy(xl_s.at[pl.ds(local_row, RUN_ALIGN)], xs_ref.at[pl.ds(sorted_row, RUN_ALIGN)],
                                     sem)

    _move_runs(tab_ref, pl.program_id(0), chunk_copy)


def _scatter(tab_flat, tail_flat, n_used, ri, h2, *, m, row_tile, n_tiles):
    t, d = h2.shape
    grid_spec = pltpu.PrefetchScalarGridSpec(
        num_scalar_prefetch=3,
        grid=(t // m,),
        in_specs=[pl.BlockSpec((SUBLANES, m), lambda i, tab, tail, nu: (0, i)),
                  pl.BlockSpec((m, d), lambda i, tab, tail, nu: (i, 0))],
        out_specs=pl.BlockSpec(memory_space=pl.ANY),
        scratch_shapes=[pltpu.VMEM((_local_rows(m), d), F32), pltpu.VMEM((RUN_ALIGN, d), F32),
                        pltpu.SemaphoreType.DMA(()), pltpu.SemaphoreType.DMA(())],
    )
    return pl.pallas_call(
        functools.partial(_scatter_kernel, row_tile=row_tile),
        grid_spec=grid_spec,
        out_shape=jax.ShapeDtypeStruct((n_tiles * row_tile, d), F32),
        compiler_params=_params(("arbitrary",), VMEM_LIMIT),
    )(tab_flat, tail_flat, n_used, ri, h2)


def _experts_kernel(te_ref, nu_ref, x_ref, wg_ref, wu_ref, wd_ref, y_ref, wgb, wub, wdb):
    i = pl.program_id(0)

    @pl.when(i < nu_ref[0])
    def _():
        @pl.when((i == 0) | (te_ref[i] != te_ref[jnp.maximum(i - 1, 0)]))
        def _():
            wgb[...] = wg_ref[0].astype(BF16)
            wub[...] = wu_ref[0].astype(BF16)
            wdb[...] = wd_ref[0].astype(BF16)

        x = x_ref[...].astype(BF16)
        a = jnp.dot(x, wgb[...], preferred_element_type=F32)
        u = jnp.dot(x, wub[...], preferred_element_type=F32)
        act = (a * _sigmoid(a)) * u
        y_ref[...] = jnp.dot(act.astype(BF16), wdb[...], preferred_element_type=F32)

    @pl.when(i >= nu_ref[0])
    def _():
        y_ref[...] = jnp.zeros(y_ref.shape, F32)


def _experts(tile_expert, n_used, xs, w_gate, w_up, w_down, *, row_tile):
    p, d = xs.shape
    de = w_gate.shape[-1]
    row_map = lambda i, te, nu: (jnp.maximum(jnp.minimum(i, nu[0] - 1), 0), 0)
    grid_spec = pltpu.PrefetchScalarGridSpec(
        num_scalar_prefetch=2,
        grid=(p // row_tile,),
        in_specs=[pl.BlockSpec((row_tile, d), row_map),
                  pl.BlockSpec((1, d, de), lambda i, te, nu: (te[i], 0, 0)),
                  pl.BlockSpec((1, d, de), lambda i, te, nu: (te[i], 0, 0)),
                  pl.BlockSpec((1, de, d), lambda i, te, nu: (te[i], 0, 0))],
        out_specs=pl.BlockSpec((row_tile, d), lambda i, te, nu: (i, 0)),
        scratch_shapes=[pltpu.VMEM((d, de), BF16), pltpu.VMEM((d, de), BF16), pltpu.VMEM((de, d), BF16)],
    )
    return pl.pallas_call(
        _experts_kernel,
        grid_spec=grid_spec,
        out_shape=jax.ShapeDtypeStruct((p, d), F32),
        compiler_params=_params(("arbitrary",), VMEM_LIMIT),
    )(tile_expert, n_used, xs, w_gate, w_up, w_down)


def _combine_kernel(tab_ref, x1_ref, gf_ref, rw_ref, ys_ref, g2_ref, b2_ref, o_ref, yl_s, sem,
                    *, bb, ts, alpha):
    m = bb * ts
    d = x1_ref.shape[-1]
    lc = yl_s.shape[0]
    tile = pl.program_id(0) * pl.num_programs(1) + pl.program_id(1)

    @pl.when(tile == 0)
    def _():
        yl_s[...] = jnp.zeros(yl_s.shape, F32)

    def chunk_copy(local_row, sorted_row):
        return pltpu.make_async_copy(ys_ref.at[pl.ds(sorted_row, RUN_ALIGN)], yl_s.at[pl.ds(local_row, RUN_ALIGN)],
                                     sem)

    _move_runs(tab_ref, tile, chunk_copy)
    cols = jnp.concatenate([rw_ref[...], jnp.zeros((LANES - SUBLANES, m), F32)], axis=0).T
    kl = lax.broadcasted_iota(I32, (m, lc), 1)
    weights = (jnp.where(kl == cols[:, 2:3].astype(I32), cols[:, 0:1], 0.0)
               + jnp.where(kl == cols[:, 3:4].astype(I32), cols[:, 1:2], 0.0))
    moe = jnp.dot(weights.astype(BF16), yl_s[...].astype(BF16), preferred_element_type=F32)
    y = alpha * x1_ref[...] + (1.0 + gf_ref[...]) * moe.reshape(bb, ts, d)
    o_ref[...] = _layernorm_rows(y, g2_ref[...], b2_ref[...])


def _combine(tab_flat, x1, gf, rw, ys, g2, b2, *, bb, ts, alpha):
    b, s, d = x1.shape
    m = bb * ts
    ns = s // ts
    grid_spec = pltpu.PrefetchScalarGridSpec(
        num_scalar_prefetch=1,
        grid=(b // bb, ns),
        in_specs=[pl.BlockSpec((bb, ts, d), lambda i, j, tab: (i, j, 0)),
                  pl.BlockSpec((bb, 1, d), lambda i, j, tab: (i, 0, 0)),
                  pl.BlockSpec((SUBLANES, m), lambda i, j, tab: (0, i * ns + j)),
                  pl.BlockSpec(memory_space=pl.ANY),
                  pl.BlockSpec((1, d), lambda i, j, tab: (0, 0)),
                  pl.BlockSpec((1, d), lambda i, j, tab: (0, 0))],
        out_specs=pl.BlockSpec((bb, ts, d), lambda i, j, tab: (i, j, 0)),
        scratch_shapes=[pltpu.VMEM((_local_rows(m), d), F32), pltpu.SemaphoreType.DMA(())],
    )
    return pl.pallas_call(
        functools.partial(_combine_kernel, bb=bb, ts=ts, alpha=alpha),
        grid_spec=grid_spec,
        out_shape=jax.ShapeDtypeStruct((b, s, d), F32),
        compiler_params=_params(("arbitrary", "arbitrary"), VMEM_LIMIT),
    )(tab_flat, x1, gf, rw, ys, g2, b2)


def _layer(x, mod, p, lam_init, alpha, conv_buf, c0, n0, m0, paged, *, sample):
    b, s, d = x.shape
    sh_a, sc_a, g_a, sh_f, sc_f, g_f = mod
    if sample:
        bb, ts, act = b, s, F32
    else:
        bb, ts, act = 1, min(s, 512), BF16
    ca, q, k, v, om, g, qd, kd, vd, kdn, vdn, conv_new = _proj(
        x, sc_a, sh_a, p["wa"], p["wg"], p["wb"], p["bg"], conv_buf, p["w_conv"], p["b_conv"], p["wqk"],
        bb=bb, ts=ts, act=act)

    if sample:
        chunk = LANES
        pad_rows = lambda a: jnp.pad(a, ((0, 0), (0, chunk - s), (0, 0)))
        lane = jnp.arange(GATE_LANES)
        gate_pad = jnp.where(lane < ML_HEADS, -jnp.inf, jnp.where(lane < 2 * ML_HEADS, jnp.inf, 0.0)).astype(F32)
        g_in = jnp.concatenate([g, jnp.broadcast_to(gate_pad, (b, chunk - s, GATE_LANES))], axis=1)
        hm, c1, n1, m1 = _mlstm(pad_rows(q), pad_rows(k), pad_rows(v), g_in, pad_rows(ca), pad_rows(om),
                                c0, n0, m0, p["gn_m"], p["skip_m"], chunk=chunk, act=act)
        hm = hm[:, :s]
        cache_k, cache_v, page_table = paged
        ad = _dattn_sample(qd, kdn, vdn, cache_k, cache_v, page_table, p["lams"], p["subln_g"],
                           pps=min(PAGES_PER_STEP, page_table.shape[1]), lam_init=lam_init)
    else:
        hm, c1, n1, m1 = _mlstm(q, k, v, g, ca, om, c0, n0, m0, p["gn_m"], p["skip_m"],
                                chunk=min(s, 256), act=act)
        ad = _dattn_prompt(qd, kd, vd, p["lams"], p["subln_g"].reshape(DA_V_DIM, 1), tq=min(s, 512),
                           lam_init=lam_init, act=act)

    t = b * s
    x1, h2, ri, rw, cnt, cb, cn = _mix(hm, ad, x, g_a, sc_f, sh_f, p["wo1"], p["wo2"], p["ln1_g"], p["ln1_b"],
                                       p["wrt"], p["brt"], bb=bb, ts=ts, alpha=alpha)
    row_tile = SAMPLE_ROW_TILE if sample else ROW_TILE
    n_tok_tiles = t // (bb * ts)
    n_tiles = -(-(2 * t + N_EXPERTS * (RUN_ALIGN - 1) * n_tok_tiles) // row_tile) + N_EXPERTS
    n_tiles_pad = -(-n_tiles // LANES) * LANES
    tab, tiles, tail = _plan(cnt, cb, cn, row_tile=row_tile, n_tiles_pad=n_tiles_pad)
    tab_flat = tab[::SUBLANES].reshape(-1)
    n_used = tiles[1, 0:1]
    xs = _scatter(tab_flat, tail[0], n_used, ri, h2, m=bb * ts, row_tile=row_tile, n_tiles=n_tiles)
    ys = _experts(tiles[0, :n_tiles], n_used, xs, p["w_gate"], p["w_up"], p["w_down"], row_tile=row_tile)
    y = _combine(tab_flat, x1, g_f, rw, ys, p["ln2_g"], p["ln2_b"], bb=bb, ts=ts, alpha=alpha)
    return y, kdn, vdn, c1, n1, m1, conv_new


def _layer_params(l, w_in, w_conv, b_conv, w_mq, w_mk, b_i, b_f, gn_m, skip_m, lam_q1, lam_k1, lam_q2, lam_k2,
                  subln_g, w_out, ln1_g, ln1_b, w_rg, b_rg, w_re, b_re, w_gate, w_up, w_down, ln2_g, ln2_b):
    w3 = 3 * ML_WIDTH
    n_gate = 2 * ML_HEADS
    wi = w_in[l]
    d = wi.shape[0]
    wg = jnp.zeros((d, GATE_LANES), F32).at[:, :n_gate].set(wi[:, w3:w3 + n_gate])
    bg = jnp.zeros((1, GATE_LANES), F32).at[0, :ML_HEADS].set(b_i[l]).at[0, ML_HEADS:n_gate].set(b_f[l])
    wrt = jnp.zeros((ROUTER_ROWS, d), F32).at[:N_GROUPS].set(w_rg[l].T).at[8:].set(w_re[l].T)
    brt = jnp.zeros((ROUTER_ROWS, 1), F32).at[:N_GROUPS, 0].set(b_rg[l]).at[8:, 0].set(b_re[l])
    return {
        "wa": wi[:, :w3].astype(BF16),
        "wg": wg.astype(BF16),
        "wb": wi[:, w3 + n_gate:].astype(BF16),
        "bg": bg,
        "w_conv": w_conv[l],
        "b_conv": b_conv[l][None, :],
        "wqk": jnp.concatenate([w_mq[l], w_mk[l]], axis=-1).astype(BF16),
        "gn_m": gn_m[l], "skip_m": skip_m[l],
        "lams": (lam_q1[l][None, :], lam_k1[l][None, :], lam_q2[l][None, :], lam_k2[l][None, :]),
        "subln_g": subln_g[l][None, :],
        "wo1": w_out[l][:ML_WIDTH].astype(BF16),
        "wo2": w_out[l][ML_WIDTH:].astype(BF16),
        "ln1_g": ln1_g[l][None, :], "ln1_b": ln1_b[l][None, :],
        "wrt": wrt, "brt": brt,
        "w_gate": w_gate[l], "w_up": w_up[l], "w_down": w_down[l],
        "ln2_g": ln2_g[l][None, :], "ln2_b": ln2_b[l][None, :],
    }


def kernel(x_prompt, x_sample, cache_k, cache_v, state_C, state_n, state_m, state_conv, page_table, c_prompt, c_sample, w_ada, b_ada, w_in, w_conv, b_conv, w_mq, w_mk, b_i, b_f, gn_m, skip_m, lam_q1, lam_k1, lam_q2, lam_k2, subln_g, w_out, ln1_g, ln1_b, w_rg, b_rg, w_re, b_re, w_gate, w_up, w_down, ln2_g, ln2_b):
    depth = w_ada.shape[0]
    bp, sp, d = x_prompt.shape
    bs, ss, _ = x_sample.shape
    alpha = (2 * depth) ** 0.25
    yp, ys = x_prompt, x_sample
    outs_p = [[] for _ in range(6)]
    outs_s = [[] for _ in range(6)]
    c_all = jnp.concatenate([c_prompt, c_sample], axis=0)
    for l in range(depth):
        p = _layer_params(l, w_in, w_conv, b_conv, w_mq, w_mk, b_i, b_f, gn_m, skip_m, lam_q1, lam_k1, lam_q2,
                          lam_k2, subln_g, w_out, ln1_g, ln1_b, w_rg, b_rg, w_re, b_re, w_gate, w_up, w_down,
                          ln2_g, ln2_b)
        lam_init = 0.8 - 0.6 * math.exp(-0.3 * l)
        mod = _ada(c_all, w_ada[l], b_ada[l][None, :])
        mod_p = tuple(mod[:bp, None, i * d:(i + 1) * d] for i in range(6))
        mod_s = tuple(mod[bp:, None, i * d:(i + 1) * d] for i in range(6))
        h, hd = ML_HEADS, ML_HEAD_DIM
        res_p = _layer(yp, mod_p, p, lam_init, alpha,
                       jnp.zeros((bp, CONV_W - 1, ML_WIDTH), F32), jnp.zeros((bp, h, hd, hd), F32),
                       jnp.zeros((bp, h, hd), F32), jnp.zeros((bp, 1, h), F32), None, sample=False)
        n_pool, page = cache_k.shape[1], cache_k.shape[2]
        paged = (cache_k[l].reshape(n_pool, page * DA_HEADS, DA_V_DIM),
                 cache_v[l].reshape(n_pool, page * DA_HEADS, DA_V_DIM), page_table)
        res_s = _layer(ys, mod_s, p, lam_init, alpha, state_conv[l], state_C[l], state_n[l],
                       state_m[l][:, None, :], paged, sample=True)
        yp, ys = res_p[0], res_s[0]
        for outs, res, nb, ns in ((outs_p, res_p, bp, sp), (outs_s, res_s, bs, ss)):
            outs[0].append(res[1].reshape(nb, ns, DA_HEADS, 2 * DA_HEAD_DIM))
            outs[1].append(res[2].reshape(nb, ns, DA_HEADS, DA_V_DIM))
            outs[2].append(res[3])
            outs[3].append(res[4])
            outs[4].append(res[5].reshape(nb, h))
            outs[5].append(res[6])
    return (yp, ys, *(jnp.stack(o) for o in outs_p), *(jnp.stack(o) for o in outs_s))
```

```python
import functools
import math

import jax
import jax.numpy as jnp
from jax import lax
from jax.experimental import pallas as pl
from jax.experimental.pallas import tpu as pltpu

F32 = jnp.float32
BF16 = jnp.bfloat16
I32 = jnp.int32
HIGHEST = lax.Precision.HIGHEST

LN_EPS = 1e-5
ML_HEADS = 4
ML_HEAD_DIM = 128
ML_WIDTH = ML_HEADS * ML_HEAD_DIM
CONV_W = 4
DA_HEADS = 4
DA_HEAD_DIM = 64
DA_V_DIM = 2 * DA_HEAD_DIM
DA_WIDTH = DA_HEADS * DA_V_DIM
N_GROUPS = 4
EXPERTS_PER_GROUP = 8
N_EXPERTS = N_GROUPS * EXPERTS_PER_GROUP
GATE_LANES = 128
ROUTER_ROWS = 8 + N_EXPERTS
SUBLANES = 8
LANES = 128
ROW_TILE = 512
SAMPLE_ROW_TILE = 128
ATTN_TQ = 512
ATTN_TK = 512
PAGES_PER_STEP = 16
RUN_SHIFT = 3
RUN_ALIGN = 1 << RUN_SHIFT
TABLE_N = N_EXPERTS
TABLE_LOCAL = 2 * N_EXPERTS
VMEM_LIMIT = 56 * 2 ** 20

_NT = (((1,), (1,)), ((), ()))
LOG2E = 1.4426950408889634


def _params(sem, vmem=None):
    return pltpu.CompilerParams(dimension_semantics=sem, vmem_limit_bytes=vmem)


def _sigmoid(x):
    return jax.nn.sigmoid(x)


def _log_sigmoid(x):
    return jnp.minimum(x, 0.0) - jnp.log1p(jnp.exp(-jnp.abs(x)))


def _ceil_to(x, k):
    return jnp.floor((x + (k - 1)) * (1.0 / k)) * k


def _cumsum_rows(x):
    n = x.shape[0]
    row = lax.broadcasted_iota(I32, x.shape, 0)
    shift = 1
    while shift < n:
        x = x + jnp.where(row >= shift, pltpu.roll(x, shift, axis=0), 0.0)
        shift *= 2
    return x


def _cumsum_lanes(x, n):
    lane = lax.broadcasted_iota(I32, x.shape, 1)
    shift = 1
    while shift < n:
        x = x + jnp.where(lane >= shift, pltpu.roll(x, shift, axis=1), 0.0)
        shift *= 2
    return x


def _layernorm_rows(y, g, b):
    mu = jnp.mean(y, axis=-1, keepdims=True)
    d = y - mu
    var = jnp.mean(d * d, axis=-1, keepdims=True)
    return d * lax.rsqrt(var + LN_EPS) * g + b


def _ada_kernel(c_ref, w_ref, b_ref, o_ref):
    c = c_ref[...]
    s = c * _sigmoid(c)
    o_ref[...] = jnp.dot(s, w_ref[...], preferred_element_type=F32, precision=HIGHEST) + b_ref[...]


def _ada(c, w, b):
    bc, d = c.shape
    n = w.shape[1]
    tn = 512
    return pl.pallas_call(
        _ada_kernel,
        grid=(n // tn,),
        in_specs=[pl.BlockSpec((bc, d), lambda j: (0, 0)),
                  pl.BlockSpec((d, tn), lambda j: (0, j)),
                  pl.BlockSpec((1, tn), lambda j: (0, j))],
        out_specs=pl.BlockSpec((bc, tn), lambda j: (0, j)),
        out_shape=jax.ShapeDtypeStruct((bc, n), F32),
        compiler_params=_params(("arbitrary",)),
    )(c, w, b)


def _proj_kernel(x_ref, sc_ref, sh_ref, wa_ref, wg_ref, wb_ref, bg_ref, cbuf_ref, wconv_ref, bconv_ref, wqk_ref,
                 ca_ref, q_ref, k_ref, v_ref, om_ref, g_ref, qd_ref, kd_ref, vd_ref, kdn_ref, vdn_ref, cnew_ref,
                 ext_ref, *, bb, ts):
    si = pl.program_id(1)
    m = bb * ts
    d = x_ref.shape[-1]
    h = (x_ref[...] * (1.0 + sc_ref[...]) + sh_ref[...]).reshape(m, d).astype(BF16)
    pa = jnp.dot(h, wa_ref[...], preferred_element_type=F32)
    pb = jnp.dot(h, wb_ref[...], preferred_element_type=F32)
    g = jnp.dot(h, wg_ref[...], preferred_element_type=F32) + bg_ref[...]
    g_ref[...] = g.reshape(bb, ts, GATE_LANES)
    w = ML_WIDTH
    v_ref[...] = pa[:, w:2 * w].reshape(bb, ts, w).astype(v_ref.dtype)
    om_ref[...] = pa[:, 2 * w:3 * w].reshape(bb, ts, w)
    qd_ref[...] = pb[:, 0:w].reshape(bb, ts, w).astype(qd_ref.dtype)
    kd = pb[:, w:2 * w]
    vd = pb[:, 2 * w:3 * w]
    kd_ref[...] = kd.reshape(bb, ts, w).astype(kd_ref.dtype)
    vd_ref[...] = vd.reshape(bb, ts, w).astype(vd_ref.dtype)
    for hh in range(DA_HEADS):
        cols = slice(hh * DA_V_DIM, (hh + 1) * DA_V_DIM)
        kdn_ref[:, pl.ds(hh, ts, stride=DA_HEADS), :] = kd[:, cols].reshape(bb, ts, DA_V_DIM)
        vdn_ref[:, pl.ds(hh, ts, stride=DA_HEADS), :] = vd[:, cols].reshape(bb, ts, DA_V_DIM)

    @pl.when(si == 0)
    def _():
        ext_ref[:, 5:8, :] = cbuf_ref[...]

    @pl.when(si > 0)
    def _():
        ext_ref[:, 0:8, :] = ext_ref[:, ts:ts + 8, :]

    ext_ref[:, 8:8 + ts, :] = pa[:, 0:w].reshape(bb, ts, w)
    y = bconv_ref[...]
    for j in range(CONV_W):
        y = y + wconv_ref[j:j + 1, :] * ext_ref[:, 5 + j:5 + j + ts, :]
    ca = y * _sigmoid(y)
    ca_ref[...] = ca
    cnew_ref[...] = ext_ref[:, ts + 5:ts + 8, :]

    ca2 = ca.reshape(m, w)
    hd = ML_HEAD_DIM
    for hh in range(ML_HEADS):
        qk = jnp.dot(ca2[:, hh * hd:(hh + 1) * hd].astype(BF16), wqk_ref[hh], preferred_element_type=F32)
        q_ref[:, :, hh * hd:(hh + 1) * hd] = (qk[:, 0:hd] * (hd ** -0.5)).reshape(bb, ts, hd).astype(q_ref.dtype)
        k_ref[:, :, hh * hd:(hh + 1) * hd] = qk[:, hd:2 * hd].reshape(bb, ts, hd).astype(k_ref.dtype)


def _proj(x, sc, sh, wa, wg, wb, bg, cbuf, wconv, bconv, wqk, *, bb, ts, act):
    b, s, d = x.shape
    w = ML_WIDTH
    grid = (b // bb, s // ts)
    tok = lambda n: pl.BlockSpec((bb, ts, n), lambda i, j: (i, j, 0))
    per_b = lambda r, n: pl.BlockSpec((bb, r, n), lambda i, j: (i, 0, 0))
    const = lambda shape: pl.BlockSpec(shape, lambda i, j: (0,) * len(shape))
    out_shape = (
        jax.ShapeDtypeStruct((b, s, w), F32),
        jax.ShapeDtypeStruct((b, s, w), act),
        jax.ShapeDtypeStruct((b, s, w), act),
        jax.ShapeDtypeStruct((b, s, w), act),
        jax.ShapeDtypeStruct((b, s, w), F32),
        jax.ShapeDtypeStruct((b, s, GATE_LANES), F32),
        jax.ShapeDtypeStruct((b, s, w), act),
        jax.ShapeDtypeStruct((b, s, w), act),
        jax.ShapeDtypeStruct((b, s, w), act),
        jax.ShapeDtypeStruct((b, s * DA_HEADS, DA_V_DIM), F32),
        jax.ShapeDtypeStruct((b, s * DA_HEADS, DA_V_DIM), F32),
        jax.ShapeDtypeStruct((b, CONV_W - 1, w), F32),
    )
    cache_rows = pl.BlockSpec((bb, ts * DA_HEADS, DA_V_DIM), lambda i, j: (i, j, 0))
    out_specs = (tok(w), tok(w), tok(w), tok(w), tok(w), tok(GATE_LANES), tok(w), tok(w), tok(w),
                 cache_rows, cache_rows, per_b(CONV_W - 1, w))
    return pl.pallas_call(
        functools.partial(_proj_kernel, bb=bb, ts=ts),
        grid=grid,
        in_specs=[tok(d), per_b(1, d), per_b(1, d), const(wa.shape), const(wg.shape), const(wb.shape),
                  const(bg.shape), per_b(CONV_W - 1, w), const(wconv.shape), const(bconv.shape),
                  const(wqk.shape)],
        out_specs=out_specs,
        out_shape=out_shape,
        scratch_shapes=[pltpu.VMEM((bb, ts + 8, w), F32)],
        compiler_params=_params(("arbitrary", "arbitrary"), VMEM_LIMIT),
    )(x, sc, sh, wa, wg, wb, bg, cbuf, wconv, bconv, wqk)


def _mlstm_kernel(q_ref, k_ref, v_ref, g_ref, ca_ref, om_ref, c0_ref, n0_ref, m0_ref, gn_ref, skip_ref,
                  hm_ref, c1_ref, n1_ref, m1_ref, c_s, n_s, m_s, *, chunk):
    si = pl.program_id(1)
    ln = chunk
    hd = ML_HEAD_DIM

    @pl.when(si == 0)
    def _():
        c_s[...] = c0_ref[0]
        n_s[...] = n0_ref[0]
        m_s[...] = m0_ref[0]

    g = g_ref[0]
    row = lax.broadcasted_iota(I32, (ln, GATE_LANES), 0)
    bc = _log_sigmoid(g)
    shift = 1
    while shift < ln:
        bc = bc + jnp.where(row >= shift, pltpu.roll(bc, shift, axis=0), 0.0)
        shift *= 2
    g_t = g.T
    bc_t = bc.T
    causal = lax.broadcasted_iota(I32, (ln, ln), 0) >= lax.broadcasted_iota(I32, (ln, ln), 1)

    for hh in range(ML_HEADS):
        cols = slice(hh * hd, (hh + 1) * hd)
        qb = q_ref[0, :, cols].astype(BF16)
        kf = k_ref[0, :, cols].astype(F32)
        kb = kf.astype(BF16)
        vb = v_ref[0, :, cols].astype(BF16)
        b_col = bc[:, ML_HEADS + hh:ML_HEADS + hh + 1]
        i_col = g[:, hh:hh + 1]
        b_row = bc_t[ML_HEADS + hh:ML_HEADS + hh + 1, :]
        i_row = g_t[hh:hh + 1, :]
        m_prev = m_s[:, hh:hh + 1]
        log_d = jnp.where(causal, b_col - b_row + i_row, -jnp.inf)
        inter = b_col + m_prev
        m_t = jnp.maximum(inter, jnp.max(log_d, axis=-1, keepdims=True))
        w_inter = jnp.exp(inter - m_t)
        s = lax.dot_general(qb, kb, _NT, preferred_element_type=F32) * jnp.exp(log_d - m_t)
        c_old = c_s[hh]
        n_old = n_s[hh:hh + 1, :]
        num = (w_inter * jnp.dot(qb, c_old.astype(BF16), preferred_element_type=F32)
               + jnp.dot(s.astype(BF16), vb, preferred_element_type=F32))
        den = (w_inter * jnp.sum(qb.astype(F32) * n_old, axis=-1, keepdims=True)
               + jnp.sum(s, axis=-1, keepdims=True))
        hc = num / jnp.maximum(jnp.abs(den), jnp.exp(-m_t))
        m_new = m_t[ln - 1:ln, :]
        b_last = b_col[ln - 1:ln, :]
        w_state = jnp.exp(b_last + m_prev - m_new)
        kw = jnp.exp(b_last - b_col + i_col - m_new) * kf
        c_s[hh] = w_state * c_old + jnp.dot(kw.T.astype(BF16), vb, preferred_element_type=F32)
        n_s[hh:hh + 1, :] = w_state * n_old + jnp.sum(kw, axis=0, keepdims=True)
        m_s[:, hh:hh + 1] = m_new
        mu = jnp.mean(hc, axis=-1, keepdims=True)
        dlt = hc - mu
        var = jnp.mean(dlt * dlt, axis=-1, keepdims=True)
        hn = dlt * lax.rsqrt(var + LN_EPS) * gn_ref[hh:hh + 1, :]
        out = (hn + skip_ref[hh:hh + 1, :] * ca_ref[0, :, cols]) * _sigmoid(om_ref[0, :, cols])
        hm_ref[0, :, cols] = out.astype(hm_ref.dtype)

    @pl.when(si == pl.num_programs(1) - 1)
    def _():
        c1_ref[0] = c_s[...]
        n1_ref[0] = n_s[...]
        m1_ref[0] = m_s[...]


def _mlstm(q, k, v, g, ca, om, c0, n0, m0, gn, skip, *, chunk, act):
    b, s, w = q.shape
    h, hd = ML_HEADS, ML_HEAD_DIM
    tok = lambda n: pl.BlockSpec((1, chunk, n), lambda i, j: (i, j, 0))
    c_spec = pl.BlockSpec((1, h, hd, hd), lambda i, j: (i, 0, 0, 0))
    n_spec = pl.BlockSpec((1, h, hd), lambda i, j: (i, 0, 0))
    m_spec = pl.BlockSpec((1, 1, h), lambda i, j: (i, 0, 0))
    hw_spec = pl.BlockSpec((h, hd), lambda i, j: (0, 0))
    return pl.pallas_call(
        functools.partial(_mlstm_kernel, chunk=chunk),
        grid=(b, s // chunk),
        in_specs=[tok(w), tok(w), tok(w), tok(GATE_LANES), tok(w), tok(w), c_spec, n_spec, m_spec,
                  hw_spec, hw_spec],
        out_specs=(tok(w), c_spec, n_spec, m_spec),
        out_shape=(jax.ShapeDtypeStruct((b, s, w), act),
                   jax.ShapeDtypeStruct((b, h, hd, hd), F32),
                   jax.ShapeDtypeStruct((b, h, hd), F32),
                   jax.ShapeDtypeStruct((b, 1, h), F32)),
        scratch_shapes=[pltpu.VMEM((h, hd, hd), F32), pltpu.VMEM((h, hd), F32), pltpu.VMEM((1, h), F32)],
        compiler_params=_params(("arbitrary", "arbitrary"), VMEM_LIMIT),
    )(q, k, v, g, ca, om, c0, n0, m0, gn, skip)


def _lam(lq1_ref, lk1_ref, lq2_ref, lk2_ref, lam_init):
    a = jnp.sum(lq1_ref[...] * lk1_ref[...], axis=-1, keepdims=True)
    b = jnp.sum(lq2_ref[...] * lk2_ref[...], axis=-1, keepdims=True)
    return jnp.exp(a) - jnp.exp(b) + lam_init


def _head_rms(o, sg, lam_init):
    return o * lax.rsqrt(jnp.mean(o * o, axis=-1, keepdims=True) + LN_EPS) * sg * (1.0 - lam_init)


def _softmax_update(s, vt, m_ref, l_ref, a_ref):
    m_old = m_ref[...]
    m_new = jnp.maximum(m_old, jnp.max(s, axis=-1, keepdims=True))
    alpha = jnp.exp2(m_old - m_new)
    p = jnp.exp2(s - m_new)
    l_ref[...] = alpha * l_ref[...] + jnp.sum(p, axis=-1, keepdims=True)
    a_ref[...] = alpha * a_ref[...] + jnp.dot(p.astype(BF16), vt, preferred_element_type=F32)
    m_ref[...] = m_new


def _dattn_kernel(lq1_ref, lk1_ref, lq2_ref, lk2_ref, sgc_ref, q_ref, k_ref, v_ref, o_ref,
                  vt_s, s00, s01, s10, s11, p00, p01, p10, p11, al00, al01, al10, al11,
                  m1, l1, a1, m2, l2, a2, *, tq, tk, lam_init):
    qi = pl.program_id(2)
    n_chunks = k_ref.shape[1] // tk
    assert tq == tk
    s_s = ((s00, s01), (s10, s11))
    p_s = ((p00, p01), (p10, p11))
    al_s = ((al00, al01), (al10, al11))

    @pl.when(qi == 0)
    def _():
        for c in range(n_chunks):
            vt_s[c] = v_ref[0, c * tk:(c + 1) * tk, :].astype(F32).T.astype(BF16)

    lam = _lam(lq1_ref, lk1_ref, lq2_ref, lk2_ref, lam_init)
    q = q_ref[0].astype(F32) * (DA_HEAD_DIM ** -0.5 * LOG2E)
    lane = lax.broadcasted_iota(I32, q.shape, 1)
    q1 = jnp.where(lane < DA_HEAD_DIM, q, 0.0).astype(BF16)
    q2 = jnp.where(lane >= DA_HEAD_DIM, q, 0.0).astype(BF16)
    for m_ref, l_ref, a_ref in ((m1, l1, a1), (m2, l2, a2)):
        m_ref[...] = jnp.full(m_ref.shape, -jnp.inf, F32)
        l_ref[...] = jnp.zeros(l_ref.shape, F32)
        a_ref[...] = jnp.zeros(a_ref.shape, F32)
    key_minus_query = (lax.broadcasted_iota(I32, (tk, tq), 0) - lax.broadcasted_iota(I32, (tk, tq), 1))
    maps = ((0, q1, m1, l1, a1), (1, q2, m2, l2, a2))
    for mp in range(2):
        p_s[1][mp][...] = jnp.zeros((tk, tq), BF16)
        al_s[1][mp][...] = jnp.ones((1, tq), F32)

    def score_tile(j, slot):
        kt = k_ref[0, pl.ds(pl.multiple_of(j * tk, tk), tk), :].astype(BF16)
        for mp, qz, _, _, _ in maps:
            s_s[slot][mp][...] = lax.dot_general(kt, qz, _NT, preferred_element_type=F32)

    def value_tile(j, slot):
        vt = vt_s[j]
        for mp, _, _, _, a_ref in maps:
            a_ref[...] = (al_s[slot][mp][...] * a_ref[...]
                          + jnp.dot(vt, p_s[slot][mp][...], preferred_element_type=F32))

    def softmax_tile(slot, diag):
        for mp, _, m_ref, l_ref, _ in maps:
            st = s_s[slot][mp][...]
            if diag is not None:
                st = jnp.where(key_minus_query <= -diag, st, -jnp.inf)
            m_old = m_ref[...]
            m_new = jnp.maximum(m_old, jnp.max(st, axis=0, keepdims=True))
            alpha = jnp.exp2(m_old - m_new)
            p = jnp.exp2(st - m_new)
            l_ref[...] = alpha * l_ref[...] + jnp.sum(p, axis=0, keepdims=True)
            p_s[slot][mp][...] = p.astype(BF16)
            al_s[slot][mp][...] = alpha
            m_ref[...] = m_new

    def stage(j, slot, diag):
        value_tile(jnp.maximum(j - 1, 0), 1 - slot)
        softmax_tile(slot, diag)
        if diag is None:
            score_tile(j + 1, 1 - slot)
        else:
            value_tile(j, slot)

    def by_parity(j, diag):
        for slot in range(2):
            @pl.when((j & 1) == slot)
            def _():
                stage(j, slot, diag)

    def body(j, carry):
        by_parity(j, None)
        return carry

    score_tile(0, 0)
    lax.fori_loop(0, qi, body, 0)
    by_parity(qi, 0)
    ot = a1[...] / l1[...] - lam * (a2[...] / l2[...])
    ot = ot * lax.rsqrt(jnp.mean(ot * ot, axis=0, keepdims=True) + LN_EPS) * sgc_ref[...] * (1.0 - lam_init)
    o_ref[0] = ot.T.astype(o_ref.dtype)


def _dattn_prompt(qd, kd, vd, lams, sgc, *, tq, tk, lam_init, act):
    b, s, w = qd.shape
    dv = DA_V_DIM
    lam_spec = pl.BlockSpec((1, DA_HEAD_DIM), lambda i, h, j: (0, 0))
    stat = pltpu.VMEM((1, tq), F32)
    acc = pltpu.VMEM((dv, tq), F32)
    return pl.pallas_call(
        functools.partial(_dattn_kernel, tq=tq, tk=tk, lam_init=lam_init),
        grid=(b, DA_HEADS, s // tq),
        in_specs=[lam_spec, lam_spec, lam_spec, lam_spec,
                  pl.BlockSpec((dv, 1), lambda i, h, j: (0, 0)),
                  pl.BlockSpec((1, tq, dv), lambda i, h, j: (i, j, h)),
                  pl.BlockSpec((1, s, dv), lambda i, h, j: (i, 0, h)),
                  pl.BlockSpec((1, s, dv), lambda i, h, j: (i, 0, h))],
        out_specs=pl.BlockSpec((1, tq, dv), lambda i, h, j: (i, j, h)),
        out_shape=jax.ShapeDtypeStruct((b, s, w), act),
        scratch_shapes=[pltpu.VMEM((s // tk, dv, tk), BF16),
                        *([pltpu.VMEM((tk, tq), F32)] * 4),
                        *([pltpu.VMEM((tk, tq), BF16)] * 4),
                        stat, stat, stat, stat,
                        stat, stat, acc, stat, stat, acc],
        compiler_params=_params(("arbitrary", "arbitrary", "arbitrary"), VMEM_LIMIT),
    )(*lams, sgc, qd, kd, vd)


def _sattn_kernel(pt_ref, lq1_ref, lk1_ref, lq2_ref, lk2_ref, sg_ref, q_ref, kn_ref, vn_ref, *rest,
                  pps, sd, lam_init):
    del pt_ref
    k_refs = rest[0:pps]
    v_refs = rest[pps:2 * pps]
    o_ref, qx_s, m_s, l_s, a_s = rest[2 * pps:]
    j = pl.program_id(1)
    nq = DA_HEADS * sd
    nrow = 2 * nq
    dv = DA_V_DIM

    @pl.when(j == 0)
    def _():
        q = q_ref[0].astype(F32) * (DA_HEAD_DIM ** -0.5 * LOG2E)
        qh = jnp.concatenate([q[:, hh * dv:(hh + 1) * dv] for hh in range(DA_HEADS)], axis=0)
        lane = lax.broadcasted_iota(I32, qh.shape, 1)
        qx_s[...] = jnp.concatenate([jnp.where(lane < DA_HEAD_DIM, qh, 0.0),
                                     jnp.where(lane >= DA_HEAD_DIM, qh, 0.0)], axis=0).astype(BF16)
        m_s[...] = jnp.full(m_s.shape, -jnp.inf, F32)
        l_s[...] = jnp.zeros(l_s.shape, F32)
        a_s[...] = jnp.zeros(a_s.shape, F32)

    qx = qx_s[...]
    n_keys = k_refs[0].shape[1]
    r = lax.broadcasted_iota(I32, (nrow, n_keys), 0)
    c = lax.broadcasted_iota(I32, (nrow, n_keys), 1)
    bias = jnp.where(((r // sd) % DA_HEADS) == (c % DA_HEADS), 0.0, -jnp.inf)
    scores = [lax.dot_general(qx, k_refs[p][0].astype(BF16), _NT, preferred_element_type=F32) + bias
              for p in range(pps)]
    m_old = m_s[...]
    m_new = m_old
    for sp in scores:
        m_new = jnp.maximum(m_new, jnp.max(sp, axis=-1, keepdims=True))
    alpha = jnp.exp2(m_old - m_new)
    lsum = alpha * l_s[...]
    acc = alpha * a_s[...]
    for p, sp in enumerate(scores):
        pp = jnp.exp2(sp - m_new)
        lsum = lsum + jnp.sum(pp, axis=-1, keepdims=True)
        acc = acc + jnp.dot(pp.astype(BF16), v_refs[p][0].astype(BF16), preferred_element_type=F32)
    l_s[...] = lsum
    a_s[...] = acc
    m_s[...] = m_new

    @pl.when(j == pl.num_programs(1) - 1)
    def _():
        pad = jnp.zeros((LANES - nq, dv), F32)
        kn = jnp.concatenate([kn_ref[0], pad], axis=0).astype(BF16)
        vn = jnp.concatenate([vn_ref[0], pad], axis=0).astype(BF16)
        sn = lax.dot_general(qx, kn, _NT, preferred_element_type=F32)
        rn = lax.broadcasted_iota(I32, sn.shape, 0)
        cn = lax.broadcasted_iota(I32, sn.shape, 1)
        valid = (cn < nq) & ((cn % DA_HEADS) == ((rn // sd) % DA_HEADS)) & ((cn // DA_HEADS) <= (rn % sd))
        _softmax_update(jnp.where(valid, sn, -jnp.inf), vn, m_s, l_s, a_s)
        lam = _lam(lq1_ref, lk1_ref, lq2_ref, lk2_ref, lam_init)
        o = a_s[0:nq, :] / l_s[0:nq, :] - lam * (a_s[nq:nrow, :] / l_s[nq:nrow, :])
        o = _head_rms(o, sg_ref[...], lam_init)
        for hh in range(DA_HEADS):
            o_ref[0, :, hh * dv:(hh + 1) * dv] = o[hh * sd:(hh + 1) * sd, :].astype(o_ref.dtype)


def _dattn_sample(qd, kn, vn, cache_k, cache_v, page_table, lams, sg, *, pps, lam_init):
    b, sd, w = qd.shape
    n_pages = page_table.shape[1]
    n_keys, dv = cache_k.shape[1], cache_k.shape[2]
    h = DA_HEADS
    lam_spec = pl.BlockSpec((1, DA_HEAD_DIM), lambda i, j, pt: (0, 0))
    tok = pl.BlockSpec((1, sd, w), lambda i, j, pt: (i, 0, 0))
    new_rows = pl.BlockSpec((1, sd * h, dv), lambda i, j, pt: (i, 0, 0))

    nrow = 2 * h * sd

    def page_spec(p):
        return pl.BlockSpec((1, n_keys, dv), lambda i, j, pt: (pt[i, j * pps + p], 0, 0))

    grid_spec = pltpu.PrefetchScalarGridSpec(
        num_scalar_prefetch=1,
        grid=(b, n_pages // pps),
        in_specs=[lam_spec, lam_spec, lam_spec, lam_spec,
                  pl.BlockSpec((1, dv), lambda i, j, pt: (0, 0)),
                  tok, new_rows, new_rows]
                 + [page_spec(p) for p in range(pps)] + [page_spec(p) for p in range(pps)],
        out_specs=tok,
        scratch_shapes=[pltpu.VMEM((nrow, dv), BF16), pltpu.VMEM((nrow, 1), F32), pltpu.VMEM((nrow, 1), F32),
                        pltpu.VMEM((nrow, dv), F32)],
    )
    return pl.pallas_call(
        functools.partial(_sattn_kernel, pps=pps, sd=sd, lam_init=lam_init),
        grid_spec=grid_spec,
        out_shape=jax.ShapeDtypeStruct((b, sd, w), F32),
        compiler_params=_params(("arbitrary", "arbitrary"), VMEM_LIMIT),
    )(page_table, *lams, sg, qd, kn, vn, *([cache_k] * pps), *([cache_v] * pps))


def _mix_kernel(hm_ref, ad_ref, x_ref, ga_ref, scf_ref, shf_ref, wo1_ref, wo2_ref, g1_ref, b1_ref,
                wrt_ref, brt_ref, x1_ref, h2_ref, ri_ref, rw_ref, cnt_ref, cb_ref, cn_ref, carry_s, carry_row_s,
                *, bb, ts, alpha):
    m = bb * ts
    d = x_ref.shape[-1]

    @pl.when((pl.program_id(0) == 0) & (pl.program_id(1) == 0))
    def _():
        carry_s[...] = jnp.zeros(carry_s.shape, F32)
        carry_row_s[...] = jnp.zeros(carry_row_s.shape, F32)

    hm = hm_ref[...].reshape(m, ML_WIDTH).astype(BF16)
    ad = ad_ref[...].reshape(m, DA_WIDTH).astype(BF16)
    mixed = (jnp.dot(hm, wo1_ref[...], preferred_element_type=F32)
             + jnp.dot(ad, wo2_ref[...], preferred_element_type=F32))
    y = alpha * x_ref[...] + (1.0 + ga_ref[...]) * mixed.reshape(bb, ts, d)
    x1 = _layernorm_rows(y, g1_ref[...], b1_ref[...])
    x1_ref[...] = x1
    h2 = (x1 * (1.0 + scf_ref[...]) + shf_ref[...]).reshape(m, d)
    h2_ref[...] = h2

    lt = lax.dot_general(wrt_ref[...], h2, _NT, preferred_element_type=F32, precision=HIGHEST) + brt_ref[...]
    gl = lt[0:N_GROUPS]
    gmax = jnp.max(gl, axis=0, keepdims=True)
    r4 = lax.broadcasted_iota(I32, gl.shape, 0)
    gidx = jnp.min(jnp.where(gl == gmax, r4, N_GROUPS), axis=0, keepdims=True)
    gp = 1.0 / jnp.sum(jnp.exp(gl - gmax), axis=0, keepdims=True)
    epg = EXPERTS_PER_GROUP
    esel = lt[8 + (N_GROUPS - 1) * epg:8 + N_GROUPS * epg]
    for grp in range(N_GROUPS - 2, -1, -1):
        esel = jnp.where(gidx == grp, lt[8 + grp * epg:8 + (grp + 1) * epg], esel)
    r8 = lax.broadcasted_iota(I32, esel.shape, 0)
    t1 = jnp.max(esel, axis=0, keepdims=True)
    i1 = jnp.min(jnp.where(esel == t1, r8, epg), axis=0, keepdims=True)
    rest = jnp.where(r8 == i1, -jnp.inf, esel)
    t2 = jnp.max(rest, axis=0, keepdims=True)
    i2 = jnp.min(jnp.where(rest == t2, r8, epg), axis=0, keepdims=True)
    z = jnp.exp(t2 - t1)
    w1 = gp / (1.0 + z)
    w2 = gp * z / (1.0 + z)
    e0 = gidx * epg + i1
    e1 = gidx * epg + i2

    r32 = lax.broadcasted_iota(I32, (N_EXPERTS, m), 0)
    hit0 = r32 == e0
    hit1 = r32 == e1
    onehot = jnp.where(hit0, 1.0, jnp.where(hit1, 1.0, 0.0))
    onehot_b = onehot.astype(BF16)
    before = (lax.broadcasted_iota(I32, (m, m), 0) < lax.broadcasted_iota(I32, (m, m), 1))
    prefix = jnp.dot(onehot_b, jnp.where(before, 1.0, 0.0).astype(BF16), preferred_element_type=F32)
    cnt_col = jnp.sum(onehot, axis=1, keepdims=True)
    run = jnp.broadcast_to(_ceil_to(cnt_col, RUN_ALIGN), (N_EXPERTS, LANES))
    start = _cumsum_rows(run) - run
    slot = prefix + start[:, 0:1]
    slot0 = jnp.sum(jnp.where(hit0, slot, 0.0), axis=0, keepdims=True)
    slot1 = jnp.sum(jnp.where(hit1, slot, 0.0), axis=0, keepdims=True)
    rr = lax.broadcasted_iota(I32, (SUBLANES, m), 0)
    ri_ref[...] = jnp.where(rr == 0, slot0, jnp.where(rr == 1, slot1, 0.0)).astype(I32)
    rw_ref[...] = jnp.where(rr == 0, w1, jnp.where(rr == 1, w2, jnp.where(rr == 2, slot0,
                            jnp.where(rr == 3, slot1, 0.0))))
    padded_hot = jnp.concatenate([onehot_b, jnp.zeros((LANES - N_EXPERTS, m), BF16)], axis=0)
    cnt_row = lax.dot_general(jnp.ones((SUBLANES, m), BF16), padded_hot, _NT, preferred_element_type=F32)
    cb_ref[...] = carry_row_s[...]
    cn_ref[...] = cnt_row
    carry_row_s[...] = carry_row_s[...] + _ceil_to(cnt_row, RUN_ALIGN)
    carry_s[...] = carry_s[...] + run
    cnt_ref[...] = carry_s[...]


def _mix(hm, ad, x, ga, scf, shf, wo1, wo2, g1, b1, wrt, brt, *, bb, ts, alpha):
    b, s, d = x.shape
    m = bb * ts
    t = b * s
    ns = s // ts
    n_tok_tiles = t // m
    tok = lambda n: pl.BlockSpec((bb, ts, n), lambda i, j: (i, j, 0))
    per_b = pl.BlockSpec((bb, 1, d), lambda i, j: (i, 0, 0))
    const = lambda shape: pl.BlockSpec(shape, lambda i, j: (0,) * len(shape))
    lin = pl.BlockSpec((SUBLANES, m), lambda i, j: (0, i * ns + j))
    per_tile = pl.BlockSpec((SUBLANES, LANES), lambda i, j: (i * ns + j, 0))
    cnt_shape = (N_EXPERTS, LANES)
    return pl.pallas_call(
        functools.partial(_mix_kernel, bb=bb, ts=ts, alpha=alpha),
        grid=(b // bb, ns),
        in_specs=[tok(ML_WIDTH), tok(DA_WIDTH), tok(d), per_b, per_b, per_b, const(wo1.shape), const(wo2.shape),
                  const(g1.shape), const(b1.shape), const(wrt.shape), const(brt.shape)],
        out_specs=(tok(d), pl.BlockSpec((m, d), lambda i, j: (i * ns + j, 0)), lin, lin, const(cnt_shape),
                   per_tile, per_tile),
        out_shape=(jax.ShapeDtypeStruct((b, s, d), F32),
                   jax.ShapeDtypeStruct((t, d), F32),
                   jax.ShapeDtypeStruct((SUBLANES, t), I32),
                   jax.ShapeDtypeStruct((SUBLANES, t), F32),
                   jax.ShapeDtypeStruct(cnt_shape, F32),
                   jax.ShapeDtypeStruct((n_tok_tiles * SUBLANES, LANES), F32),
                   jax.ShapeDtypeStruct((n_tok_tiles * SUBLANES, LANES), F32)),
        scratch_shapes=[pltpu.VMEM(cnt_shape, F32), pltpu.VMEM((SUBLANES, LANES), F32)],
        compiler_params=_params(("arbitrary", "arbitrary"), VMEM_LIMIT),
    )(hm, ad, x, ga, scf, shf, wo1, wo2, g1, b1, wrt, brt)


def _plan_kernel(cnt_ref, cb_ref, cn_ref, tab_ref, tile_ref, tail_ref, *, row_tile):
    padded = _ceil_to(cnt_ref[...], row_tile)
    ends = _cumsum_rows(padded)
    nt = tile_ref.shape[1]
    first_row = (lax.broadcasted_iota(I32, (N_EXPERTS, nt), 1) * row_tile).astype(F32)
    done = jnp.sum(jnp.where(ends[:, 0:1] <= first_row, 1, 0), axis=0, keepdims=True)
    expert = jnp.minimum(done, N_EXPERTS - 1)
    used = (ends[N_EXPERTS - 1:N_EXPERTS, 0:1] * (1.0 / row_tile)).astype(I32)
    rt = lax.broadcasted_iota(I32, (SUBLANES, nt), 0)
    tile_ref[...] = jnp.where(rt == 0, expert, jnp.where(rt == 1, used, 0))
    n_rows = cb_ref.shape[0]
    cb = cb_ref[...]
    cn = cn_ref[...]
    lane8 = lax.broadcasted_iota(I32, (SUBLANES, LANES), 1)
    total = cb[n_rows - SUBLANES:n_rows, :] + _ceil_to(cn[n_rows - SUBLANES:n_rows, :], RUN_ALIGN)
    live8 = lane8 < N_EXPERTS
    region = jnp.where(live8, _ceil_to(total, row_tile), 0.0)
    offs8 = _cumsum_lanes(region, N_EXPERTS) - region
    offs = offs8[0:1, :]
    tail_ref[...] = (jnp.where(live8, offs8 + total, 0.0)
                     + pltpu.roll(jnp.where(live8, (region - total) * (1.0 / RUN_ALIGN), 0.0), TABLE_N, axis=1)
                     ).astype(I32)
    run = _ceil_to(cn, RUN_ALIGN)
    local = _cumsum_lanes(run, N_EXPERTS) - run
    live = lax.broadcasted_iota(I32, cb.shape, 1) < N_EXPERTS
    tab = (jnp.where(live, cb + offs, 0.0)
           + pltpu.roll(jnp.where(live, cn, 0.0), TABLE_N, axis=1)
           + pltpu.roll(jnp.where(live, local, 0.0), TABLE_LOCAL, axis=1))
    tab_ref[...] = tab.astype(I32)


def _plan(cnt, cb, cn, *, row_tile, n_tiles_pad):
    full = lambda a: pl.BlockSpec(a.shape, lambda i: (0, 0))
    return pl.pallas_call(
        functools.partial(_plan_kernel, row_tile=row_tile),
        grid=(1,),
        in_specs=[full(cnt), full(cb), full(cn)],
        out_specs=(full(cb), pl.BlockSpec((SUBLANES, n_tiles_pad), lambda i: (0, 0)),
                   pl.BlockSpec((SUBLANES, LANES), lambda i: (0, 0))),
        out_shape=(jax.ShapeDtypeStruct(cb.shape, I32), jax.ShapeDtypeStruct((SUBLANES, n_tiles_pad), I32),
                   jax.ShapeDtypeStruct((SUBLANES, LANES), I32)),
        compiler_params=_params(("arbitrary",)),
    )(cnt, cb, cn)


def _local_rows(m):
    return 2 * m + N_EXPERTS * RUN_ALIGN


def _move_runs(tab_ref, tile, chunk_copy):
    base = tile * LANES

    def per_expert(e, total):
        first = tab_ref[base + e]
        n_chunks = (tab_ref[base + TABLE_N + e] + (RUN_ALIGN - 1)) >> RUN_SHIFT
        local = tab_ref[base + TABLE_LOCAL + e]

        def per_chunk(c, carry):
            off = c * RUN_ALIGN
            chunk_copy(pl.multiple_of(local + off, RUN_ALIGN), pl.multiple_of(first + off, RUN_ALIGN)).start()
            return carry

        lax.fori_loop(0, n_chunks, per_chunk, 0)
        return total + n_chunks

    total = lax.fori_loop(0, N_EXPERTS, per_expert, 0)

    def wait_one(c, carry):
        chunk_copy(0, 0).wait()
        return carry

    lax.fori_loop(0, total, wait_one, 0)


def _scatter_kernel(tab_ref, tail_ref, nu_ref, ri_ref, h2_ref, xs_ref, xl_s, zero_s, sem, zsem, *, row_tile):
    m = h2_ref.shape[0]
    lc = xl_s.shape[0]

    @pl.when(pl.program_id(0) == 0)
    def _():
        zero_s[...] = jnp.zeros(zero_s.shape, F32)

        def zero_chunk(row):
            return pltpu.make_async_copy(zero_s, xs_ref.at[pl.ds(pl.multiple_of(row, RUN_ALIGN), RUN_ALIGN)], zsem)

        def span(first, n_chunks):
            def body(c, carry):
                zero_chunk(first + c * RUN_ALIGN).start()
                return carry
            lax.fori_loop(0, n_chunks, body, 0)
            return n_chunks

        def per_expert(e, total):
            return total + span(tail_ref[e], tail_ref[TABLE_N + e])

        total = lax.fori_loop(0, N_EXPERTS, per_expert, 0)
        used_rows = nu_ref[0] * row_tile
        total = total + span(used_rows, (xs_ref.shape[0] - used_rows) >> RUN_SHIFT)

        def wait_one(c, carry):
            zero_chunk(0).wait()
            return carry

        lax.fori_loop(0, total, wait_one, 0)

    k = lax.broadcasted_iota(I32, (lc, m), 0)
    pick = jnp.where(k == ri_ref[0:1, :], 1.0, jnp.where(k == ri_ref[1:2, :], 1.0, 0.0)).astype(BF16)
    xl_s[...] = jnp.dot(pick, h2_ref[...].astype(BF16), preferred_element_type=F32)

    def chunk_copy(local_row, sorted_row):
        return pltpu.make_async_copy(xl_s.at[pl.ds(local_row, RUN_ALIGN)], xs_ref.at[pl.ds(sorted_row, RUN_ALIGN)],
                                     sem)

    _move_runs(tab_ref, pl.program_id(0), chunk_copy)


def _scatter(tab_flat, tail_flat, n_used, ri, h2, *, m, row_tile, n_tiles):
    t, d = h2.shape
    grid_spec = pltpu.PrefetchScalarGridSpec(
        num_scalar_prefetch=3,
        grid=(t // m,),
        in_specs=[pl.BlockSpec((SUBLANES, m), lambda i, tab, tail, nu: (0, i)),
                  pl.BlockSpec((m, d), lambda i, tab, tail, nu: (i, 0))],
        out_specs=pl.BlockSpec(memory_space=pl.ANY),
        scratch_shapes=[pltpu.VMEM((_local_rows(m), d), F32), pltpu.VMEM((RUN_ALIGN, d), F32),
                        pltpu.SemaphoreType.DMA(()), pltpu.SemaphoreType.DMA(())],
    )
    return pl.pallas_call(
        functools.partial(_scatter_kernel, row_tile=row_tile),
        grid_spec=grid_spec,
        out_shape=jax.ShapeDtypeStruct((n_tiles * row_tile, d), F32),
        compiler_params=_params(("arbitrary",), VMEM_LIMIT),
    )(tab_flat, tail_flat, n_used, ri, h2)


def _experts_kernel(te_ref, nu_ref, x_ref, wg_ref, wu_ref, wd_ref, y_ref, wgb, wub, wdb):
    i = pl.program_id(0)

    @pl.when(i < nu_ref[0])
    def _():
        @pl.when((i == 0) | (te_ref[i] != te_ref[jnp.maximum(i - 1, 0)]))
        def _():
            wgb[...] = wg_ref[0].astype(BF16)
            wub[...] = wu_ref[0].astype(BF16)
            wdb[...] = wd_ref[0].astype(BF16)

        x = x_ref[...].astype(BF16)
        a = jnp.dot(x, wgb[...], preferred_element_type=F32)
        u = jnp.dot(x, wub[...], preferred_element_type=F32)
        act = (a * _sigmoid(a)) * u
        y_ref[...] = jnp.dot(act.astype(BF16), wdb[...], preferred_element_type=F32)

    @pl.when(i >= nu_ref[0])
    def _():
        y_ref[...] = jnp.zeros(y_ref.shape, F32)


def _experts(tile_expert, n_used, xs, w_gate, w_up, w_down, *, row_tile):
    p, d = xs.shape
    de = w_gate.shape[-1]
    row_map = lambda i, te, nu: (jnp.maximum(jnp.minimum(i, nu[0] - 1), 0), 0)
    grid_spec = pltpu.PrefetchScalarGridSpec(
        num_scalar_prefetch=2,
        grid=(p // row_tile,),
        in_specs=[pl.BlockSpec((row_tile, d), row_map),
                  pl.BlockSpec((1, d, de), lambda i, te, nu: (te[i], 0, 0)),
                  pl.BlockSpec((1, d, de), lambda i, te, nu: (te[i], 0, 0)),
                  pl.BlockSpec((1, de, d), lambda i, te, nu: (te[i], 0, 0))],
        out_specs=pl.BlockSpec((row_tile, d), lambda i, te, nu: (i, 0)),
        scratch_shapes=[pltpu.VMEM((d, de), BF16), pltpu.VMEM((d, de), BF16), pltpu.VMEM((de, d), BF16)],
    )
    return pl.pallas_call(
        _experts_kernel,
        grid_spec=grid_spec,
        out_shape=jax.ShapeDtypeStruct((p, d), F32),
        compiler_params=_params(("arbitrary",), VMEM_LIMIT),
    )(tile_expert, n_used, xs, w_gate, w_up, w_down)


def _combine_kernel(tab_ref, x1_ref, gf_ref, rw_ref, ys_ref, g2_ref, b2_ref, o_ref, yl_s, sem,
                    *, bb, ts, alpha):
    m = bb * ts
    d = x1_ref.shape[-1]
    lc = yl_s.shape[0]
    tile = pl.program_id(0) * pl.num_programs(1) + pl.program_id(1)

    @pl.when(tile == 0)
    def _():
        yl_s[...] = jnp.zeros(yl_s.shape, F32)

    def chunk_copy(local_row, sorted_row):
        return pltpu.make_async_copy(ys_ref.at[pl.ds(sorted_row, RUN_ALIGN)], yl_s.at[pl.ds(local_row, RUN_ALIGN)],
                                     sem)

    _move_runs(tab_ref, tile, chunk_copy)
    cols = jnp.concatenate([rw_ref[...], jnp.zeros((LANES - SUBLANES, m), F32)], axis=0).T
    kl = lax.broadcasted_iota(I32, (m, lc), 1)
    weights = (jnp.where(kl == cols[:, 2:3].astype(I32), cols[:, 0:1], 0.0)
               + jnp.where(kl == cols[:, 3:4].astype(I32), cols[:, 1:2], 0.0))
    moe = jnp.dot(weights.astype(BF16), yl_s[...].astype(BF16), preferred_element_type=F32)
    y = alpha * x1_ref[...] + (1.0 + gf_ref[...]) * moe.reshape(bb, ts, d)
    o_ref[...] = _layernorm_rows(y, g2_ref[...], b2_ref[...])


def _combine(tab_flat, x1, gf, rw, ys, g2, b2, *, bb, ts, alpha):
    b, s, d = x1.shape
    m = bb * ts
    ns = s // ts
    grid_spec = pltpu.PrefetchScalarGridSpec(
        num_scalar_prefetch=1,
        grid=(b // bb, ns),
        in_specs=[pl.BlockSpec((bb, ts, d), lambda i, j, tab: (i, j, 0)),
                  pl.BlockSpec((bb, 1, d), lambda i, j, tab: (i, 0, 0)),
                  pl.BlockSpec((SUBLANES, m), lambda i, j, tab: (0, i * ns + j)),
                  pl.BlockSpec(memory_space=pl.ANY),
                  pl.BlockSpec((1, d), lambda i, j, tab: (0, 0)),
                  pl.BlockSpec((1, d), lambda i, j, tab: (0, 0))],
        out_specs=pl.BlockSpec((bb, ts, d), lambda i, j, tab: (i, j, 0)),
        scratch_shapes=[pltpu.VMEM((_local_rows(m), d), F32), pltpu.SemaphoreType.DMA(())],
    )
    return pl.pallas_call(
        functools.partial(_combine_kernel, bb=bb, ts=ts, alpha=alpha),
        grid_spec=grid_spec,
        out_shape=jax.ShapeDtypeStruct((b, s, d), F32),
        compiler_params=_params(("arbitrary", "arbitrary"), VMEM_LIMIT),
    )(tab_flat, x1, gf, rw, ys, g2, b2)


def _layer(x, mod, p, lam_init, alpha, conv_buf, c0, n0, m0, paged, *, sample):
    b, s, d = x.shape
    sh_a, sc_a, g_a, sh_f, sc_f, g_f = mod
    if sample:
        bb, ts, act = b, s, F32
    else:
        bb, ts, act = 1, min(s, 512), BF16
    ca, q, k, v, om, g, qd, kd, vd, kdn, vdn, conv_new = _proj(
        x, sc_a, sh_a, p["wa"], p["wg"], p["wb"], p["bg"], conv_buf, p["w_conv"], p["b_conv"], p["wqk"],
        bb=bb, ts=ts, act=act)

    if sample:
        chunk = LANES
        pad_rows = lambda a: jnp.pad(a, ((0, 0), (0, chunk - s), (0, 0)))
        lane = jnp.arange(GATE_LANES)
        gate_pad = jnp.where(lane < ML_HEADS, -jnp.inf, jnp.where(lane < 2 * ML_HEADS, jnp.inf, 0.0)).astype(F32)
        g_in = jnp.concatenate([g, jnp.broadcast_to(gate_pad, (b, chunk - s, GATE_LANES))], axis=1)
        hm, c1, n1, m1 = _mlstm(pad_rows(q), pad_rows(k), pad_rows(v), g_in, pad_rows(ca), pad_rows(om),
                                c0, n0, m0, p["gn_m"], p["skip_m"], chunk=chunk, act=act)
        hm = hm[:, :s]
        cache_k, cache_v, page_table = paged
        ad = _dattn_sample(qd, kdn, vdn, cache_k, cache_v, page_table, p["lams"], p["subln_g"],
                           pps=min(PAGES_PER_STEP, page_table.shape[1]), lam_init=lam_init)
    else:
        hm, c1, n1, m1 = _mlstm(q, k, v, g, ca, om, c0, n0, m0, p["gn_m"], p["skip_m"],
                                chunk=min(s, 256), act=act)
        ad = _dattn_prompt(qd, kd, vd, p["lams"], p["subln_g"].reshape(DA_V_DIM, 1), tq=min(s, ATTN_TQ),
                           tk=min(s, ATTN_TK), lam_init=lam_init, act=act)

    t = b * s
    x1, h2, ri, rw, cnt, cb, cn = _mix(hm, ad, x, g_a, sc_f, sh_f, p["wo1"], p["wo2"], p["ln1_g"], p["ln1_b"],
                                       p["wrt"], p["brt"], bb=bb, ts=ts, alpha=alpha)
    row_tile = SAMPLE_ROW_TILE if sample else ROW_TILE
    n_tok_tiles = t // (bb * ts)
    n_tiles = -(-(2 * t + N_EXPERTS * (RUN_ALIGN - 1) * n_tok_tiles) // row_tile) + N_EXPERTS
    n_tiles_pad = -(-n_tiles // LANES) * LANES
    tab, tiles, tail = _plan(cnt, cb, cn, row_tile=row_tile, n_tiles_pad=n_tiles_pad)
    tab_flat = tab[::SUBLANES].reshape(-1)
    n_used = tiles[1, 0:1]
    xs = _scatter(tab_flat, tail[0], n_used, ri, h2, m=bb * ts, row_tile=row_tile, n_tiles=n_tiles)
    ys = _experts(tiles[0, :n_tiles], n_used, xs, p["w_gate"], p["w_up"], p["w_down"], row_tile=row_tile)
    y = _combine(tab_flat, x1, g_f, rw, ys, p["ln2_g"], p["ln2_b"], bb=bb, ts=ts, alpha=alpha)
    return y, kdn, vdn, c1, n1, m1, conv_new


def _layer_params(l, w_in, w_conv, b_conv, w_mq, w_mk, b_i, b_f, gn_m, skip_m, lam_q1, lam_k1, lam_q2, lam_k2,
                  subln_g, w_out, ln1_g, ln1_b, w_rg, b_rg, w_re, b_re, w_gate, w_up, w_down, ln2_g, ln2_b):
    w3 = 3 * ML_WIDTH
    n_gate = 2 * ML_HEADS
    wi = w_in[l]
    d = wi.shape[0]
    wg = jnp.zeros((d, GATE_LANES), F32).at[:, :n_gate].set(wi[:, w3:w3 + n_gate])
    bg = jnp.zeros((1, GATE_LANES), F32).at[0, :ML_HEADS].set(b_i[l]).at[0, ML_HEADS:n_gate].set(b_f[l])
    wrt = jnp.zeros((ROUTER_ROWS, d), F32).at[:N_GROUPS].set(w_rg[l].T).at[8:].set(w_re[l].T)
    brt = jnp.zeros((ROUTER_ROWS, 1), F32).at[:N_GROUPS, 0].set(b_rg[l]).at[8:, 0].set(b_re[l])
    return {
        "wa": wi[:, :w3].astype(BF16),
        "wg": wg.astype(BF16),
        "wb": wi[:, w3 + n_gate:].astype(BF16),
        "bg": bg,
        "w_conv": w_conv[l],
        "b_conv": b_conv[l][None, :],
        "wqk": jnp.concatenate([w_mq[l], w_mk[l]], axis=-1).astype(BF16),
        "gn_m": gn_m[l], "skip_m": skip_m[l],
        "lams": (lam_q1[l][None, :], lam_k1[l][None, :], lam_q2[l][None, :], lam_k2[l][None, :]),
        "subln_g": subln_g[l][None, :],
        "wo1": w_out[l][:ML_WIDTH].astype(BF16),
        "wo2": w_out[l][ML_WIDTH:].astype(BF16),
        "ln1_g": ln1_g[l][None, :], "ln1_b": ln1_b[l][None, :],
        "wrt": wrt, "brt": brt,
        "w_gate": w_gate[l], "w_up": w_up[l], "w_down": w_down[l],
        "ln2_g": ln2_g[l][None, :], "ln2_b": ln2_b[l][None, :],
    }


def kernel(x_prompt, x_sample, cache_k, cache_v, state_C, state_n, state_m, state_conv, page_table, c_prompt, c_sample, w_ada, b_ada, w_in, w_conv, b_conv, w_mq, w_mk, b_i, b_f, gn_m, skip_m, lam_q1, lam_k1, lam_q2, lam_k2, subln_g, w_out, ln1_g, ln1_b, w_rg, b_rg, w_re, b_re, w_gate, w_up, w_down, ln2_g, ln2_b):
    depth = w_ada.shape[0]
    bp, sp, d = x_prompt.shape
    bs, ss, _ = x_sample.shape
    alpha = (2 * depth) ** 0.25
    yp, ys = x_prompt, x_sample
    outs_p = [[] for _ in range(6)]
    outs_s = [[] for _ in range(6)]
    c_all = jnp.concatenate([c_prompt, c_sample], axis=0)
    for l in range(depth):
        p = _layer_params(l, w_in, w_conv, b_conv, w_mq, w_mk, b_i, b_f, gn_m, skip_m, lam_q1, lam_k1, lam_q2,
                          lam_k2, subln_g, w_out, ln1_g, ln1_b, w_rg, b_rg, w_re, b_re, w_gate, w_up, w_down,
                          ln2_g, ln2_b)
        lam_init = 0.8 - 0.6 * math.exp(-0.3 * l)
        mod = _ada(c_all, w_ada[l], b_ada[l][None, :])
        mod_p = tuple(mod[:bp, None, i * d:(i + 1) * d] for i in range(6))
        mod_s = tuple(mod[bp:, None, i * d:(i + 1) * d] for i in range(6))
        h, hd = ML_HEADS, ML_HEAD_DIM
        res_p = _layer(yp, mod_p, p, lam_init, alpha,
                       jnp.zeros((bp, CONV_W - 1, ML_WIDTH), F32), jnp.zeros((bp, h, hd, hd), F32),
                       jnp.zeros((bp, h, hd), F32), jnp.zeros((bp, 1, h), F32), None, sample=False)
        n_pool, page = cache_k.shape[1], cache_k.shape[2]
        paged = (cache_k[l].reshape(n_pool, page * DA_HEADS, DA_V_DIM),
                 cache_v[l].reshape(n_pool, page * DA_HEADS, DA_V_DIM), page_table)
        res_s = _layer(ys, mod_s, p, lam_init, alpha, state_conv[l], state_C[l], state_n[l],
                       state_m[l][:, None, :], paged, sample=True)
        yp, ys = res_p[0], res_s[0]
        for outs, res, nb, ns in ((outs_p, res_p, bp, sp), (outs_s, res_s, bs, ss)):
            outs[0].append(res[1].reshape(nb, ns, DA_HEADS, 2 * DA_HEAD_DIM))
            outs[1].append(res[2].reshape(nb, ns, DA_HEADS, DA_V_DIM))
            outs[2].append(res[3])
            outs[3].append(res[4])
            outs[4].append(res[5].reshape(nb, h))
            outs[5].append(res[6])
    return (yp, ys, *(jnp.stack(o) for o in outs_p), *(jnp.stack(o) for o in outs_s))
```

```python
import functools
import math

import jax
import jax.numpy as jnp
from jax import lax
from jax.experimental import pallas as pl
from jax.experimental.pallas import tpu as pltpu

F32 = jnp.float32
BF16 = jnp.bfloat16
I32 = jnp.int32
HIGHEST = lax.Precision.HIGHEST

LN_EPS = 1e-5
ML_HEADS = 4
ML_HEAD_DIM = 128
ML_WIDTH = ML_HEADS * ML_HEAD_DIM
CONV_W = 4
DA_HEADS = 4
DA_HEAD_DIM = 64
DA_V_DIM = 2 * DA_HEAD_DIM
DA_WIDTH = DA_HEADS * DA_V_DIM
N_GROUPS = 4
EXPERTS_PER_GROUP = 8
N_EXPERTS = N_GROUPS * EXPERTS_PER_GROUP
GATE_LANES = 128
ROUTER_ROWS = 8 + N_EXPERTS
SUBLANES = 8
LANES = 128
ROW_TILE = 512
SAMPLE_ROW_TILE = 128
ATTN_TQ = 512
ATTN_TK = 512
PAGES_PER_STEP = 16
RUN_SHIFT = 3
RUN_ALIGN = 1 << RUN_SHIFT
TABLE_N = N_EXPERTS
TABLE_LOCAL = 2 * N_EXPERTS
VMEM_LIMIT = 56 * 2 ** 20

_NT = (((1,), (1,)), ((), ()))
LOG2E = 1.4426950408889634


def _params(sem, vmem=None):
    return pltpu.CompilerParams(dimension_semantics=sem, vmem_limit_bytes=vmem)


def _sigmoid(x):
    return jax.nn.sigmoid(x)


def _log_sigmoid(x):
    return jnp.minimum(x, 0.0) - jnp.log1p(jnp.exp(-jnp.abs(x)))


def _ceil_to(x, k):
    return jnp.floor((x + (k - 1)) * (1.0 / k)) * k


def _cumsum_rows(x):
    n = x.shape[0]
    row = lax.broadcasted_iota(I32, x.shape, 0)
    shift = 1
    while shift < n:
        x = x + jnp.where(row >= shift, pltpu.roll(x, shift, axis=0), 0.0)
        shift *= 2
    return x


def _cumsum_lanes(x, n):
    lane = lax.broadcasted_iota(I32, x.shape, 1)
    shift = 1
    while shift < n:
        x = x + jnp.where(lane >= shift, pltpu.roll(x, shift, axis=1), 0.0)
        shift *= 2
    return x


def _layernorm_rows(y, g, b):
    mu = jnp.mean(y, axis=-1, keepdims=True)
    d = y - mu
    var = jnp.mean(d * d, axis=-1, keepdims=True)
    return d * lax.rsqrt(var + LN_EPS) * g + b


def _ada_kernel(c_ref, w_ref, b_ref, o_ref):
    c = c_ref[...]
    s = c * _sigmoid(c)
    o_ref[...] = jnp.dot(s, w_ref[...], preferred_element_type=F32, precision=HIGHEST) + b_ref[...]


def _ada(c, w, b):
    bc, d = c.shape
    n = w.shape[1]
    tn = 512
    return pl.pallas_call(
        _ada_kernel,
        grid=(n // tn,),
        in_specs=[pl.BlockSpec((bc, d), lambda j: (0, 0)),
                  pl.BlockSpec((d, tn), lambda j: (0, j)),
                  pl.BlockSpec((1, tn), lambda j: (0, j))],
        out_specs=pl.BlockSpec((bc, tn), lambda j: (0, j)),
        out_shape=jax.ShapeDtypeStruct((bc, n), F32),
        compiler_params=_params(("arbitrary",)),
    )(c, w, b)


def _proj_kernel(x_ref, sc_ref, sh_ref, wa_ref, wg_ref, wb_ref, bg_ref, cbuf_ref, wconv_ref, bconv_ref, wqk_ref,
                 ca_ref, q_ref, k_ref, v_ref, om_ref, g_ref, qd_ref, kd_ref, vd_ref, kdn_ref, vdn_ref, cnew_ref,
                 ext_ref, *, bb, ts):
    si = pl.program_id(1)
    m = bb * ts
    d = x_ref.shape[-1]
    h = (x_ref[...] * (1.0 + sc_ref[...]) + sh_ref[...]).reshape(m, d).astype(BF16)
    pa = jnp.dot(h, wa_ref[...], preferred_element_type=F32)
    pb = jnp.dot(h, wb_ref[...], preferred_element_type=F32)
    g = jnp.dot(h, wg_ref[...], preferred_element_type=F32) + bg_ref[...]
    g_ref[...] = g.reshape(bb, ts, GATE_LANES)
    w = ML_WIDTH
    v_ref[...] = pa[:, w:2 * w].reshape(bb, ts, w).astype(v_ref.dtype)
    om_ref[...] = pa[:, 2 * w:3 * w].reshape(bb, ts, w)
    qd_ref[...] = pb[:, 0:w].reshape(bb, ts, w).astype(qd_ref.dtype)
    kd = pb[:, w:2 * w]
    vd = pb[:, 2 * w:3 * w]
    kd_ref[...] = kd.reshape(bb, ts, w).astype(kd_ref.dtype)
    vd_ref[...] = vd.reshape(bb, ts, w).astype(vd_ref.dtype)
    for hh in range(DA_HEADS):
        cols = slice(hh * DA_V_DIM, (hh + 1) * DA_V_DIM)
        kdn_ref[:, pl.ds(hh, ts, stride=DA_HEADS), :] = kd[:, cols].reshape(bb, ts, DA_V_DIM)
        vdn_ref[:, pl.ds(hh, ts, stride=DA_HEADS), :] = vd[:, cols].reshape(bb, ts, DA_V_DIM)

    @pl.when(si == 0)
    def _():
        ext_ref[:, 5:8, :] = cbuf_ref[...]

    @pl.when(si > 0)
    def _():
        ext_ref[:, 0:8, :] = ext_ref[:, ts:ts + 8, :]

    ext_ref[:, 8:8 + ts, :] = pa[:, 0:w].reshape(bb, ts, w)
    y = bconv_ref[...]
    for j in range(CONV_W):
        y = y + wconv_ref[j:j + 1, :] * ext_ref[:, 5 + j:5 + j + ts, :]
    ca = y * _sigmoid(y)
    ca_ref[...] = ca
    cnew_ref[...] = ext_ref[:, ts + 5:ts + 8, :]

    ca2 = ca.reshape(m, w)
    hd = ML_HEAD_DIM
    for hh in range(ML_HEADS):
        qk = jnp.dot(ca2[:, hh * hd:(hh + 1) * hd].astype(BF16), wqk_ref[hh], preferred_element_type=F32)
        q_ref[:, :, hh * hd:(hh + 1) * hd] = (qk[:, 0:hd] * (hd ** -0.5)).reshape(bb, ts, hd).astype(q_ref.dtype)
        k_ref[:, :, hh * hd:(hh + 1) * hd] = qk[:, hd:2 * hd].reshape(bb, ts, hd).astype(k_ref.dtype)


def _proj(x, sc, sh, wa, wg, wb, bg, cbuf, wconv, bconv, wqk, *, bb, ts, act):
    b, s, d = x.shape
    w = ML_WIDTH
    grid = (b // bb, s // ts)
    tok = lambda n: pl.BlockSpec((bb, ts, n), lambda i, j: (i, j, 0))
    per_b = lambda r, n: pl.BlockSpec((bb, r, n), lambda i, j: (i, 0, 0))
    const = lambda shape: pl.BlockSpec(shape, lambda i, j: (0,) * len(shape))
    out_shape = (
        jax.ShapeDtypeStruct((b, s, w), F32),
        jax.ShapeDtypeStruct((b, s, w), act),
        jax.ShapeDtypeStruct((b, s, w), act),
        jax.ShapeDtypeStruct((b, s, w), act),
        jax.ShapeDtypeStruct((b, s, w), F32),
        jax.ShapeDtypeStruct((b, s, GATE_LANES), F32),
        jax.ShapeDtypeStruct((b, s, w), act),
        jax.ShapeDtypeStruct((b, s, w), act),
        jax.ShapeDtypeStruct((b, s, w), act),
        jax.ShapeDtypeStruct((b, s * DA_HEADS, DA_V_DIM), F32),
        jax.ShapeDtypeStruct((b, s * DA_HEADS, DA_V_DIM), F32),
        jax.ShapeDtypeStruct((b, CONV_W - 1, w), F32),
    )
    cache_rows = pl.BlockSpec((bb, ts * DA_HEADS, DA_V_DIM), lambda i, j: (i, j, 0))
    out_specs = (tok(w), tok(w), tok(w), tok(w), tok(w), tok(GATE_LANES), tok(w), tok(w), tok(w),
                 cache_rows, cache_rows, per_b(CONV_W - 1, w))
    return pl.pallas_call(
        functools.partial(_proj_kernel, bb=bb, ts=ts),
        grid=grid,
        in_specs=[tok(d), per_b(1, d), per_b(1, d), const(wa.shape), const(wg.shape), const(wb.shape),
                  const(bg.shape), per_b(CONV_W - 1, w), const(wconv.shape), const(bconv.shape),
                  const(wqk.shape)],
        out_specs=out_specs,
        out_shape=out_shape,
        scratch_shapes=[pltpu.VMEM((bb, ts + 8, w), F32)],
        compiler_params=_params(("arbitrary", "arbitrary"), VMEM_LIMIT),
    )(x, sc, sh, wa, wg, wb, bg, cbuf, wconv, bconv, wqk)


def _mlstm_kernel(q_ref, k_ref, v_ref, g_ref, ca_ref, om_ref, c0_ref, n0_ref, m0_ref, gn_ref, skip_ref,
                  hm_ref, c1_ref, n1_ref, m1_ref, c_s, n_s, m_s, *, chunk):
    si = pl.program_id(1)
    ln = chunk
    hd = ML_HEAD_DIM

    @pl.when(si == 0)
    def _():
        c_s[...] = c0_ref[0]
        n_s[...] = n0_ref[0]
        m_s[...] = m0_ref[0]

    g = g_ref[0]
    row = lax.broadcasted_iota(I32, (ln, GATE_LANES), 0)
    bc = _log_sigmoid(g)
    shift = 1
    while shift < ln:
        bc = bc + jnp.where(row >= shift, pltpu.roll(bc, shift, axis=0), 0.0)
        shift *= 2
    g_t = g.T
    bc_t = bc.T
    causal = lax.broadcasted_iota(I32, (ln, ln), 0) >= lax.broadcasted_iota(I32, (ln, ln), 1)

    for hh in range(ML_HEADS):
        cols = slice(hh * hd, (hh + 1) * hd)
        qb = q_ref[0, :, cols].astype(BF16)
        kf = k_ref[0, :, cols].astype(F32)
        kb = kf.astype(BF16)
        vb = v_ref[0, :, cols].astype(BF16)
        b_col = bc[:, ML_HEADS + hh:ML_HEADS + hh + 1]
        i_col = g[:, hh:hh + 1]
        b_row = bc_t[ML_HEADS + hh:ML_HEADS + hh + 1, :]
        i_row = g_t[hh:hh + 1, :]
        m_prev = m_s[:, hh:hh + 1]
        log_d = jnp.where(causal, b_col - b_row + i_row, -jnp.inf)
        inter = b_col + m_prev
        m_t = jnp.maximum(inter, jnp.max(log_d, axis=-1, keepdims=True))
        w_inter = jnp.exp(inter - m_t)
        s = lax.dot_general(qb, kb, _NT, preferred_element_type=F32) * jnp.exp(log_d - m_t)
        c_old = c_s[hh]
        n_old = n_s[hh:hh + 1, :]
        num = (w_inter * jnp.dot(qb, c_old.astype(BF16), preferred_element_type=F32)
               + jnp.dot(s.astype(BF16), vb, preferred_element_type=F32))
        den = (w_inter * jnp.sum(qb.astype(F32) * n_old, axis=-1, keepdims=True)
               + jnp.sum(s, axis=-1, keepdims=True))
        hc = num / jnp.maximum(jnp.abs(den), jnp.exp(-m_t))
        m_new = m_t[ln - 1:ln, :]
        b_last = b_col[ln - 1:ln, :]
        w_state = jnp.exp(b_last + m_prev - m_new)
        kw = jnp.exp(b_last - b_col + i_col - m_new) * kf
        c_s[hh] = w_state * c_old + jnp.dot(kw.T.astype(BF16), vb, preferred_element_type=F32)
        n_s[hh:hh + 1, :] = w_state * n_old + jnp.sum(kw, axis=0, keepdims=True)
        m_s[:, hh:hh + 1] = m_new
        mu = jnp.mean(hc, axis=-1, keepdims=True)
        dlt = hc - mu
        var = jnp.mean(dlt * dlt, axis=-1, keepdims=True)
        hn = dlt * lax.rsqrt(var + LN_EPS) * gn_ref[hh:hh + 1, :]
        out = (hn + skip_ref[hh:hh + 1, :] * ca_ref[0, :, cols]) * _sigmoid(om_ref[0, :, cols])
        hm_ref[0, :, cols] = out.astype(hm_ref.dtype)

    @pl.when(si == pl.num_programs(1) - 1)
    def _():
        c1_ref[0] = c_s[...]
        n1_ref[0] = n_s[...]
        m1_ref[0] = m_s[...]


def _mlstm(q, k, v, g, ca, om, c0, n0, m0, gn, skip, *, chunk, act):
    b, s, w = q.shape
    h, hd = ML_HEADS, ML_HEAD_DIM
    tok = lambda n: pl.BlockSpec((1, chunk, n), lambda i, j: (i, j, 0))
    c_spec = pl.BlockSpec((1, h, hd, hd), lambda i, j: (i, 0, 0, 0))
    n_spec = pl.BlockSpec((1, h, hd), lambda i, j: (i, 0, 0))
    m_spec = pl.BlockSpec((1, 1, h), lambda i, j: (i, 0, 0))
    hw_spec = pl.BlockSpec((h, hd), lambda i, j: (0, 0))
    return pl.pallas_call(
        functools.partial(_mlstm_kernel, chunk=chunk),
        grid=(b, s // chunk),
        in_specs=[tok(w), tok(w), tok(w), tok(GATE_LANES), tok(w), tok(w), c_spec, n_spec, m_spec,
                  hw_spec, hw_spec],
        out_specs=(tok(w), c_spec, n_spec, m_spec),
        out_shape=(jax.ShapeDtypeStruct((b, s, w), act),
                   jax.ShapeDtypeStruct((b, h, hd, hd), F32),
                   jax.ShapeDtypeStruct((b, h, hd), F32),
                   jax.ShapeDtypeStruct((b, 1, h), F32)),
        scratch_shapes=[pltpu.VMEM((h, hd, hd), F32), pltpu.VMEM((h, hd), F32), pltpu.VMEM((1, h), F32)],
        compiler_params=_params(("arbitrary", "arbitrary"), VMEM_LIMIT),
    )(q, k, v, g, ca, om, c0, n0, m0, gn, skip)


def _lam(lq1_ref, lk1_ref, lq2_ref, lk2_ref, lam_init):
    a = jnp.sum(lq1_ref[...] * lk1_ref[...], axis=-1, keepdims=True)
    b = jnp.sum(lq2_ref[...] * lk2_ref[...], axis=-1, keepdims=True)
    return jnp.exp(a) - jnp.exp(b) + lam_init


def _head_rms(o, sg, lam_init):
    return o * lax.rsqrt(jnp.mean(o * o, axis=-1, keepdims=True) + LN_EPS) * sg * (1.0 - lam_init)


def _softmax_update(s, vt, m_ref, l_ref, a_ref):
    m_old = m_ref[...]
    m_new = jnp.maximum(m_old, jnp.max(s, axis=-1, keepdims=True))
    alpha = jnp.exp2(m_old - m_new)
    p = jnp.exp2(s - m_new)
    l_ref[...] = alpha * l_ref[...] + jnp.sum(p, axis=-1, keepdims=True)
    a_ref[...] = alpha * a_ref[...] + jnp.dot(p.astype(BF16), vt, preferred_element_type=F32)
    m_ref[...] = m_new


def _dattn_kernel(lq1_ref, lk1_ref, lq2_ref, lk2_ref, sgc_ref, q_ref, k_ref, v_ref, o_ref,
                  vt_s, s00, s01, s10, s11, x00, x01, x10, x11, m1, l1, a1, m2, l2, a2, *, tq, tk, lam_init):
    qi = pl.program_id(2)
    n_chunks = k_ref.shape[1] // tk
    assert tq == tk
    s_s = ((s00, s01), (s10, s11))
    x_s = ((x00, x01), (x10, x11))

    @pl.when(qi == 0)
    def _():
        for c in range(n_chunks):
            vt_s[c] = v_ref[0, c * tk:(c + 1) * tk, :].astype(F32).T.astype(BF16)

    lam = _lam(lq1_ref, lk1_ref, lq2_ref, lk2_ref, lam_init)
    q = q_ref[0].astype(F32) * (DA_HEAD_DIM ** -0.5 * LOG2E)
    lane = lax.broadcasted_iota(I32, q.shape, 1)
    q1 = jnp.where(lane < DA_HEAD_DIM, q, 0.0).astype(BF16)
    q2 = jnp.where(lane >= DA_HEAD_DIM, q, 0.0).astype(BF16)
    for m_ref, l_ref, a_ref in ((m1, l1, a1), (m2, l2, a2)):
        m_ref[...] = jnp.full(m_ref.shape, -jnp.inf, F32)
        l_ref[...] = jnp.zeros(l_ref.shape, F32)
        a_ref[...] = jnp.zeros(a_ref.shape, F32)
    key_minus_query = (lax.broadcasted_iota(I32, (tk, tq), 0) - lax.broadcasted_iota(I32, (tk, tq), 1))
    maps = ((0, q1, m1, l1, a1), (1, q2, m2, l2, a2))

    def score_tile(j, slot, limit):
        kt = k_ref[0, pl.ds(pl.multiple_of(j * tk, tk), tk), :].astype(BF16)
        for mp, qz, _, _, _ in maps:
            st = lax.dot_general(kt, qz, _NT, preferred_element_type=F32)
            if limit is not None:
                st = jnp.where(key_minus_query <= limit, st, -jnp.inf)
            s_s[slot][mp][...] = st
            x_s[slot][mp][...] = jnp.max(st, axis=0, keepdims=True)

    def consume_tile(j, slot):
        vt = vt_s[j]
        for mp, _, m_ref, l_ref, a_ref in maps:
            m_old = m_ref[...]
            m_new = jnp.maximum(m_old, x_s[slot][mp][...])
            alpha = jnp.exp2(m_old - m_new)
            p = jnp.exp2(s_s[slot][mp][...] - m_new)
            l_ref[...] = alpha * l_ref[...] + jnp.sum(p, axis=0, keepdims=True)
            a_ref[...] = alpha * a_ref[...] + jnp.dot(vt, p.astype(BF16), preferred_element_type=F32)
            m_ref[...] = m_new

    def by_parity(j, next_limit, have_next=True):
        for slot in range(2):
            @pl.when((j & 1) == slot)
            def _():
                if have_next:
                    score_tile(j + 1, 1 - slot, next_limit)
                consume_tile(j, slot)

    def body(j, carry):
        by_parity(j, None)
        return carry

    score_tile(0, 0, jnp.where(qi == 0, 0, tk))
    lax.fori_loop(0, qi - 1, body, 0)

    @pl.when(qi > 0)
    def _():
        by_parity(qi - 1, 0)

    by_parity(qi, None, have_next=False)
    ot = a1[...] / l1[...] - lam * (a2[...] / l2[...])
    ot = ot * lax.rsqrt(jnp.mean(ot * ot, axis=0, keepdims=True) + LN_EPS) * sgc_ref[...] * (1.0 - lam_init)
    o_ref[0] = ot.T.astype(o_ref.dtype)


def _dattn_prompt(qd, kd, vd, lams, sgc, *, tq, tk, lam_init, act):
    b, s, w = qd.shape
    dv = DA_V_DIM
    lam_spec = pl.BlockSpec((1, DA_HEAD_DIM), lambda i, h, j: (0, 0))
    stat = pltpu.VMEM((1, tq), F32)
    acc = pltpu.VMEM((dv, tq), F32)
    return pl.pallas_call(
        functools.partial(_dattn_kernel, tq=tq, tk=tk, lam_init=lam_init),
        grid=(b, DA_HEADS, s // tq),
        in_specs=[lam_spec, lam_spec, lam_spec, lam_spec,
                  pl.BlockSpec((dv, 1), lambda i, h, j: (0, 0)),
                  pl.BlockSpec((1, tq, dv), lambda i, h, j: (i, j, h)),
                  pl.BlockSpec((1, s, dv), lambda i, h, j: (i, 0, h)),
                  pl.BlockSpec((1, s, dv), lambda i, h, j: (i, 0, h))],
        out_specs=pl.BlockSpec((1, tq, dv), lambda i, h, j: (i, j, h)),
        out_shape=jax.ShapeDtypeStruct((b, s, w), act),
        scratch_shapes=[pltpu.VMEM((s // tk, dv, tk), BF16),
                        *([pltpu.VMEM((tk, tq), F32)] * 4),
                        stat, stat, stat, stat,
                        stat, stat, acc, stat, stat, acc],
        compiler_params=_params(("arbitrary", "arbitrary", "arbitrary"), VMEM_LIMIT),
    )(*lams, sgc, qd, kd, vd)


def _sattn_kernel(pt_ref, lq1_ref, lk1_ref, lq2_ref, lk2_ref, sg_ref, q_ref, kn_ref, vn_ref, *rest,
                  pps, sd, lam_init):
    del pt_ref
    k_refs = rest[0:pps]
    v_refs = rest[pps:2 * pps]
    o_ref, qx_s, m_s, l_s, a_s = rest[2 * pps:]
    j = pl.program_id(1)
    nq = DA_HEADS * sd
    nrow = 2 * nq
    dv = DA_V_DIM

    @pl.when(j == 0)
    def _():
        q = q_ref[0].astype(F32) * (DA_HEAD_DIM ** -0.5 * LOG2E)
        qh = jnp.concatenate([q[:, hh * dv:(hh + 1) * dv] for hh in range(DA_HEADS)], axis=0)
        lane = lax.broadcasted_iota(I32, qh.shape, 1)
        qx_s[...] = jnp.concatenate([jnp.where(lane < DA_HEAD_DIM, qh, 0.0),
                                     jnp.where(lane >= DA_HEAD_DIM, qh, 0.0)], axis=0).astype(BF16)
        m_s[...] = jnp.full(m_s.shape, -jnp.inf, F32)
        l_s[...] = jnp.zeros(l_s.shape, F32)
        a_s[...] = jnp.zeros(a_s.shape, F32)

    qx = qx_s[...]
    n_keys = k_refs[0].shape[1]
    r = lax.broadcasted_iota(I32, (nrow, n_keys), 0)
    c = lax.broadcasted_iota(I32, (nrow, n_keys), 1)
    bias = jnp.where(((r // sd) % DA_HEADS) == (c % DA_HEADS), 0.0, -jnp.inf)
    scores = [lax.dot_general(qx, k_refs[p][0].astype(BF16), _NT, preferred_element_type=F32) + bias
              for p in range(pps)]
    m_old = m_s[...]
    m_new = m_old
    for sp in scores:
        m_new = jnp.maximum(m_new, jnp.max(sp, axis=-1, keepdims=True))
    alpha = jnp.exp2(m_old - m_new)
    lsum = alpha * l_s[...]
    acc = alpha * a_s[...]
    for p, sp in enumerate(scores):
        pp = jnp.exp2(sp - m_new)
        lsum = lsum + jnp.sum(pp, axis=-1, keepdims=True)
        acc = acc + jnp.dot(pp.astype(BF16), v_refs[p][0].astype(BF16), preferred_element_type=F32)
    l_s[...] = lsum
    a_s[...] = acc
    m_s[...] = m_new

    @pl.when(j == pl.num_programs(1) - 1)
    def _():
        pad = jnp.zeros((LANES - nq, dv), F32)
        kn = jnp.concatenate([kn_ref[0], pad], axis=0).astype(BF16)
        vn = jnp.concatenate([vn_ref[0], pad], axis=0).astype(BF16)
        sn = lax.dot_general(qx, kn, _NT, preferred_element_type=F32)
        rn = lax.broadcasted_iota(I32, sn.shape, 0)
        cn = lax.broadcasted_iota(I32, sn.shape, 1)
        valid = (cn < nq) & ((cn % DA_HEADS) == ((rn // sd) % DA_HEADS)) & ((cn // DA_HEADS) <= (rn % sd))
        _softmax_update(jnp.where(valid, sn, -jnp.inf), vn, m_s, l_s, a_s)
        lam = _lam(lq1_ref, lk1_ref, lq2_ref, lk2_ref, lam_init)
        o = a_s[0:nq, :] / l_s[0:nq, :] - lam * (a_s[nq:nrow, :] / l_s[nq:nrow, :])
        o = _head_rms(o, sg_ref[...], lam_init)
        for hh in range(DA_HEADS):
            o_ref[0, :, hh * dv:(hh + 1) * dv] = o[hh * sd:(hh + 1) * sd, :].astype(o_ref.dtype)


def _dattn_sample(qd, kn, vn, cache_k, cache_v, page_table, lams, sg, *, pps, lam_init):
    b, sd, w = qd.shape
    n_pages = page_table.shape[1]
    n_keys, dv = cache_k.shape[1], cache_k.shape[2]
    h = DA_HEADS
    lam_spec = pl.BlockSpec((1, DA_HEAD_DIM), lambda i, j, pt: (0, 0))
    tok = pl.BlockSpec((1, sd, w), lambda i, j, pt: (i, 0, 0))
    new_rows = pl.BlockSpec((1, sd * h, dv), lambda i, j, pt: (i, 0, 0))

    nrow = 2 * h * sd

    def page_spec(p):
        return pl.BlockSpec((1, n_keys, dv), lambda i, j, pt: (pt[i, j * pps + p], 0, 0))

    grid_spec = pltpu.PrefetchScalarGridSpec(
        num_scalar_prefetch=1,
        grid=(b, n_pages // pps),
        in_specs=[lam_spec, lam_spec, lam_spec, lam_spec,
                  pl.BlockSpec((1, dv), lambda i, j, pt: (0, 0)),
                  tok, new_rows, new_rows]
                 + [page_spec(p) for p in range(pps)] + [page_spec(p) for p in range(pps)],
        out_specs=tok,
        scratch_shapes=[pltpu.VMEM((nrow, dv), BF16), pltpu.VMEM((nrow, 1), F32), pltpu.VMEM((nrow, 1), F32),
                        pltpu.VMEM((nrow, dv), F32)],
    )
    return pl.pallas_call(
        functools.partial(_sattn_kernel, pps=pps, sd=sd, lam_init=lam_init),
        grid_spec=grid_spec,
        out_shape=jax.ShapeDtypeStruct((b, sd, w), F32),
        compiler_params=_params(("arbitrary", "arbitrary"), VMEM_LIMIT),
    )(page_table, *lams, sg, qd, kn, vn, *([cache_k] * pps), *([cache_v] * pps))


def _mix_kernel(hm_ref, ad_ref, x_ref, ga_ref, scf_ref, shf_ref, wo1_ref, wo2_ref, g1_ref, b1_ref,
                wrt_ref, brt_ref, x1_ref, h2_ref, ri_ref, rw_ref, cnt_ref, cb_ref, cn_ref, carry_s, carry_row_s,
                *, bb, ts, alpha):
    m = bb * ts
    d = x_ref.shape[-1]

    @pl.when((pl.program_id(0) == 0) & (pl.program_id(1) == 0))
    def _():
        carry_s[...] = jnp.zeros(carry_s.shape, F32)
        carry_row_s[...] = jnp.zeros(carry_row_s.shape, F32)

    hm = hm_ref[...].reshape(m, ML_WIDTH).astype(BF16)
    ad = ad_ref[...].reshape(m, DA_WIDTH).astype(BF16)
    mixed = (jnp.dot(hm, wo1_ref[...], preferred_element_type=F32)
             + jnp.dot(ad, wo2_ref[...], preferred_element_type=F32))
    y = alpha * x_ref[...] + (1.0 + ga_ref[...]) * mixed.reshape(bb, ts, d)
    x1 = _layernorm_rows(y, g1_ref[...], b1_ref[...])
    x1_ref[...] = x1
    h2 = (x1 * (1.0 + scf_ref[...]) + shf_ref[...]).reshape(m, d)
    h2_ref[...] = h2

    lt = lax.dot_general(wrt_ref[...], h2, _NT, preferred_element_type=F32, precision=HIGHEST) + brt_ref[...]
    gl = lt[0:N_GROUPS]
    gmax = jnp.max(gl, axis=0, keepdims=True)
    r4 = lax.broadcasted_iota(I32, gl.shape, 0)
    gidx = jnp.min(jnp.where(gl == gmax, r4, N_GROUPS), axis=0, keepdims=True)
    gp = 1.0 / jnp.sum(jnp.exp(gl - gmax), axis=0, keepdims=True)
    epg = EXPERTS_PER_GROUP
    esel = lt[8 + (N_GROUPS - 1) * epg:8 + N_GROUPS * epg]
    for grp in range(N_GROUPS - 2, -1, -1):
        esel = jnp.where(gidx == grp, lt[8 + grp * epg:8 + (grp + 1) * epg], esel)
    r8 = lax.broadcasted_iota(I32, esel.shape, 0)
    t1 = jnp.max(esel, axis=0, keepdims=True)
    i1 = jnp.min(jnp.where(esel == t1, r8, epg), axis=0, keepdims=True)
    rest = jnp.where(r8 == i1, -jnp.inf, esel)
    t2 = jnp.max(rest, axis=0, keepdims=True)
    i2 = jnp.min(jnp.where(rest == t2, r8, epg), axis=0, keepdims=True)
    z = jnp.exp(t2 - t1)
    w1 = gp / (1.0 + z)
    w2 = gp * z / (1.0 + z)
    e0 = gidx * epg + i1
    e1 = gidx * epg + i2

    r32 = lax.broadcasted_iota(I32, (N_EXPERTS, m), 0)
    hit0 = r32 == e0
    hit1 = r32 == e1
    onehot = jnp.where(hit0, 1.0, jnp.where(hit1, 1.0, 0.0))
    onehot_b = onehot.astype(BF16)
    before = (lax.broadcasted_iota(I32, (m, m), 0) < lax.broadcasted_iota(I32, (m, m), 1))
    prefix = jnp.dot(onehot_b, jnp.where(before, 1.0, 0.0).astype(BF16), preferred_element_type=F32)
    cnt_col = jnp.sum(onehot, axis=1, keepdims=True)
    run = jnp.broadcast_to(_ceil_to(cnt_col, RUN_ALIGN), (N_EXPERTS, LANES))
    start = _cumsum_rows(run) - run
    slot = prefix + start[:, 0:1]
    slot0 = jnp.sum(jnp.where(hit0, slot, 0.0), axis=0, keepdims=True)
    slot1 = jnp.sum(jnp.where(hit1, slot, 0.0), axis=0, keepdims=True)
    rr = lax.broadcasted_iota(I32, (SUBLANES, m), 0)
    ri_ref[...] = jnp.where(rr == 0, slot0, jnp.where(rr == 1, slot1, 0.0)).astype(I32)
    rw_ref[...] = jnp.where(rr == 0, w1, jnp.where(rr == 1, w2, jnp.where(rr == 2, slot0,
                            jnp.where(rr == 3, slot1, 0.0))))
    padded_hot = jnp.concatenate([onehot_b, jnp.zeros((LANES - N_EXPERTS, m), BF16)], axis=0)
    cnt_row = lax.dot_general(jnp.ones((SUBLANES, m), BF16), padded_hot, _NT, preferred_element_type=F32)
    cb_ref[...] = carry_row_s[...]
    cn_ref[...] = cnt_row
    carry_row_s[...] = carry_row_s[...] + _ceil_to(cnt_row, RUN_ALIGN)
    carry_s[...] = carry_s[...] + run
    cnt_ref[...] = carry_s[...]


def _mix(hm, ad, x, ga, scf, shf, wo1, wo2, g1, b1, wrt, brt, *, bb, ts, alpha):
    b, s, d = x.shape
    m = bb * ts
    t = b * s
    ns = s // ts
    n_tok_tiles = t // m
    tok = lambda n: pl.BlockSpec((bb, ts, n), lambda i, j: (i, j, 0))
    per_b = pl.BlockSpec((bb, 1, d), lambda i, j: (i, 0, 0))
    const = lambda shape: pl.BlockSpec(shape, lambda i, j: (0,) * len(shape))
    lin = pl.BlockSpec((SUBLANES, m), lambda i, j: (0, i * ns + j))
    per_tile = pl.BlockSpec((SUBLANES, LANES), lambda i, j: (i * ns + j, 0))
    cnt_shape = (N_EXPERTS, LANES)
    return pl.pallas_call(
        functools.partial(_mix_kernel, bb=bb, ts=ts, alpha=alpha),
        grid=(b // bb, ns),
        in_specs=[tok(ML_WIDTH), tok(DA_WIDTH), tok(d), per_b, per_b, per_b, const(wo1.shape), const(wo2.shape),
                  const(g1.shape), const(b1.shape), const(wrt.shape), const(brt.shape)],
        out_specs=(tok(d), pl.BlockSpec((m, d), lambda i, j: (i * ns + j, 0)), lin, lin, const(cnt_shape),
                   per_tile, per_tile),
        out_shape=(jax.ShapeDtypeStruct((b, s, d), F32),
                   jax.ShapeDtypeStruct((t, d), F32),
                   jax.ShapeDtypeStruct((SUBLANES, t), I32),
                   jax.ShapeDtypeStruct((SUBLANES, t), F32),
                   jax.ShapeDtypeStruct(cnt_shape, F32),
                   jax.ShapeDtypeStruct((n_tok_tiles * SUBLANES, LANES), F32),
                   jax.ShapeDtypeStruct((n_tok_tiles * SUBLANES, LANES), F32)),
        scratch_shapes=[pltpu.VMEM(cnt_shape, F32), pltpu.VMEM((SUBLANES, LANES), F32)],
        compiler_params=_params(("arbitrary", "arbitrary"), VMEM_LIMIT),
    )(hm, ad, x, ga, scf, shf, wo1, wo2, g1, b1, wrt, brt)


def _plan_kernel(cnt_ref, cb_ref, cn_ref, tab_ref, tile_ref, tail_ref, *, row_tile):
    padded = _ceil_to(cnt_ref[...], row_tile)
    ends = _cumsum_rows(padded)
    nt = tile_ref.shape[1]
    first_row = (lax.broadcasted_iota(I32, (N_EXPERTS, nt), 1) * row_tile).astype(F32)
    done = jnp.sum(jnp.where(ends[:, 0:1] <= first_row, 1, 0), axis=0, keepdims=True)
    expert = jnp.minimum(done, N_EXPERTS - 1)
    used = (ends[N_EXPERTS - 1:N_EXPERTS, 0:1] * (1.0 / row_tile)).astype(I32)
    rt = lax.broadcasted_iota(I32, (SUBLANES, nt), 0)
    tile_ref[...] = jnp.where(rt == 0, expert, jnp.where(rt == 1, used, 0))
    n_rows = cb_ref.shape[0]
    cb = cb_ref[...]
    cn = cn_ref[...]
    lane8 = lax.broadcasted_iota(I32, (SUBLANES, LANES), 1)
    total = cb[n_rows - SUBLANES:n_rows, :] + _ceil_to(cn[n_rows - SUBLANES:n_rows, :], RUN_ALIGN)
    live8 = lane8 < N_EXPERTS
    region = jnp.where(live8, _ceil_to(total, row_tile), 0.0)
    offs8 = _cumsum_lanes(region, N_EXPERTS) - region
    offs = offs8[0:1, :]
    tail_ref[...] = (jnp.where(live8, offs8 + total, 0.0)
                     + pltpu.roll(jnp.where(live8, (region - total) * (1.0 / RUN_ALIGN), 0.0), TABLE_N, axis=1)
                     ).astype(I32)
    run = _ceil_to(cn, RUN_ALIGN)
    local = _cumsum_lanes(run, N_EXPERTS) - run
    live = lax.broadcasted_iota(I32, cb.shape, 1) < N_EXPERTS
    tab = (jnp.where(live, cb + offs, 0.0)
           + pltpu.roll(jnp.where(live, cn, 0.0), TABLE_N, axis=1)
           + pltpu.roll(jnp.where(live, local, 0.0), TABLE_LOCAL, axis=1))
    tab_ref[...] = tab.astype(I32)


def _plan(cnt, cb, cn, *, row_tile, n_tiles_pad):
    full = lambda a: pl.BlockSpec(a.shape, lambda i: (0, 0))
    return pl.pallas_call(
        functools.partial(_plan_kernel, row_tile=row_tile),
        grid=(1,),
        in_specs=[full(cnt), full(cb), full(cn)],
        out_specs=(full(cb), pl.BlockSpec((SUBLANES, n_tiles_pad), lambda i: (0, 0)),
                   pl.BlockSpec((SUBLANES, LANES), lambda i: (0, 0))),
        out_shape=(jax.ShapeDtypeStruct(cb.shape, I32), jax.ShapeDtypeStruct((SUBLANES, n_tiles_pad), I32),
                   jax.ShapeDtypeStruct((SUBLANES, LANES), I32)),
        compiler_params=_params(("arbitrary",)),
    )(cnt, cb, cn)


def _local_rows(m):
    return 2 * m + N_EXPERTS * RUN_ALIGN


def _run_chunks(tab_ref, tile, e):
    return (tab_ref[tile * LANES + TABLE_N + e] + (RUN_ALIGN - 1)) >> RUN_SHIFT


def _start_runs(tab_ref, tile, chunk_copy):
    base = tile * LANES

    def per_expert(e, carry):
        first = tab_ref[base + e]
        local = tab_ref[base + TABLE_LOCAL + e]

        def per_chunk(c, carry):
            off = c * RUN_ALIGN
            chunk_copy(pl.multiple_of(local + off, RUN_ALIGN), pl.multiple_of(first + off, RUN_ALIGN)).start()
            return carry

        lax.fori_loop(0, _run_chunks(tab_ref, tile, e), per_chunk, 0)
        return carry

    lax.fori_loop(0, N_EXPERTS, per_expert, 0)


def _wait_runs(tab_ref, tile, chunk_copy):
    total = lax.fori_loop(0, N_EXPERTS, lambda e, acc: acc + _run_chunks(tab_ref, tile, e), 0)

    def wait_one(c, carry):
        chunk_copy(0, 0).wait()
        return carry

    lax.fori_loop(0, total, wait_one, 0)


def _scatter_kernel(tab_ref, tail_ref, nu_ref, ri_ref, h2_ref, xs_ref, xl_s, zero_s, sem, zsem, *, row_tile):
    m = h2_ref.shape[0]
    lc = xl_s.shape[1]
    i = pl.program_id(0)
    slot = i & 1

    @pl.when(i == 0)
    def _():
        zero_s[...] = jnp.zeros(zero_s.shape, F32)

        def zero_chunk(row):
            return pltpu.make_async_copy(zero_s, xs_ref.at[pl.ds(pl.multiple_of(row, RUN_ALIGN), RUN_ALIGN)], zsem)

        def span(first, n_chunks):
            def body(c, carry):
                zero_chunk(first + c * RUN_ALIGN).start()
                return carry
            lax.fori_loop(0, n_chunks, body, 0)
            return n_chunks

        def per_expert(e, total):
            return total + span(tail_ref[e], tail_ref[TABLE_N + e])

        total = lax.fori_loop(0, N_EXPERTS, per_expert, 0)
        used_rows = nu_ref[0] * row_tile
        total = total + span(used_rows, (xs_ref.shape[0] - used_rows) >> RUN_SHIFT)

        def wait_one(c, carry):
            zero_chunk(0).wait()
            return carry

        lax.fori_loop(0, total, wait_one, 0)

    k = lax.broadcasted_iota(I32, (lc, m), 0)
    pick = jnp.where(k == ri_ref[0:1, :], 1.0, jnp.where(k == ri_ref[1:2, :], 1.0, 0.0)).astype(BF16)
    xl_s[slot] = jnp.dot(pick, h2_ref[...].astype(BF16), preferred_element_type=F32)

    def chunk_copy(buf):
        def build(local_row, sorted_row):
            return pltpu.make_async_copy(xl_s.at[buf, pl.ds(local_row, RUN_ALIGN)],
                                         xs_ref.at[pl.ds(sorted_row, RUN_ALIGN)], sem.at[buf])
        return build

    _start_runs(tab_ref, i, chunk_copy(slot))

    @pl.when(i > 0)
    def _():
        _wait_runs(tab_ref, i - 1, chunk_copy(1 - slot))

    @pl.when(i == pl.num_programs(0) - 1)
    def _():
        _wait_runs(tab_ref, i, chunk_copy(slot))


def _scatter(tab_flat, tail_flat, n_used, ri, h2, *, m, row_tile, n_tiles):
    t, d = h2.shape
    grid_spec = pltpu.PrefetchScalarGridSpec(
        num_scalar_prefetch=3,
        grid=(t // m,),
        in_specs=[pl.BlockSpec((SUBLANES, m), lambda i, tab, tail, nu: (0, i)),
                  pl.BlockSpec((m, d), lambda i, tab, tail, nu: (i, 0))],
        out_specs=pl.BlockSpec(memory_space=pl.ANY),
        scratch_shapes=[pltpu.VMEM((2, _local_rows(m), d), F32), pltpu.VMEM((RUN_ALIGN, d), F32),
                        pltpu.SemaphoreType.DMA((2,)), pltpu.SemaphoreType.DMA(())],
    )
    return pl.pallas_call(
        functools.partial(_scatter_kernel, row_tile=row_tile),
        grid_spec=grid_spec,
        out_shape=jax.ShapeDtypeStruct((n_tiles * row_tile, d), F32),
        compiler_params=_params(("arbitrary",), VMEM_LIMIT),
    )(tab_flat, tail_flat, n_used, ri, h2)


def _experts_kernel(te_ref, nu_ref, x_ref, wg_ref, wu_ref, wd_ref, y_ref, wgb, wub, wdb):
    i = pl.program_id(0)

    @pl.when(i < nu_ref[0])
    def _():
        @pl.when((i == 0) | (te_ref[i] != te_ref[jnp.maximum(i - 1, 0)]))
        def _():
            wgb[...] = wg_ref[0].astype(BF16)
            wub[...] = wu_ref[0].astype(BF16)
            wdb[...] = wd_ref[0].astype(BF16)

        x = x_ref[...].astype(BF16)
        a = jnp.dot(x, wgb[...], preferred_element_type=F32)
        u = jnp.dot(x, wub[...], preferred_element_type=F32)
        act = (a * _sigmoid(a)) * u
        y_ref[...] = jnp.dot(act.astype(BF16), wdb[...], preferred_element_type=F32)

    @pl.when(i >= nu_ref[0])
    def _():
        y_ref[...] = jnp.zeros(y_ref.shape, F32)


def _experts(tile_expert, n_used, xs, w_gate, w_up, w_down, *, row_tile):
    p, d = xs.shape
    de = w_gate.shape[-1]
    row_map = lambda i, te, nu: (jnp.maximum(jnp.minimum(i, nu[0] - 1), 0), 0)
    grid_spec = pltpu.PrefetchScalarGridSpec(
        num_scalar_prefetch=2,
        grid=(p // row_tile,),
        in_specs=[pl.BlockSpec((row_tile, d), row_map),
                  pl.BlockSpec((1, d, de), lambda i, te, nu: (te[i], 0, 0)),
                  pl.BlockSpec((1, d, de), lambda i, te, nu: (te[i], 0, 0)),
                  pl.BlockSpec((1, de, d), lambda i, te, nu: (te[i], 0, 0))],
        out_specs=pl.BlockSpec((row_tile, d), lambda i, te, nu: (i, 0)),
        scratch_shapes=[pltpu.VMEM((d, de), BF16), pltpu.VMEM((d, de), BF16), pltpu.VMEM((de, d), BF16)],
    )
    return pl.pallas_call(
        _experts_kernel,
        grid_spec=grid_spec,
        out_shape=jax.ShapeDtypeStruct((p, d), F32),
        compiler_params=_params(("arbitrary",), VMEM_LIMIT),
    )(tile_expert, n_used, xs, w_gate, w_up, w_down)


def _combine_kernel(tab_ref, x1_ref, gf_ref, rw_ref, ys_ref, g2_ref, b2_ref, o_ref, yl_s, sem,
                    *, bb, ts, alpha):
    m = bb * ts
    d = x1_ref.shape[-1]
    lc = yl_s.shape[1]
    tile = pl.program_id(0) * pl.num_programs(1) + pl.program_id(1)
    n_tok_tiles = pl.num_programs(0) * pl.num_programs(1)
    slot = tile & 1

    def chunk_copy(buf):
        def build(local_row, sorted_row):
            return pltpu.make_async_copy(ys_ref.at[pl.ds(sorted_row, RUN_ALIGN)],
                                         yl_s.at[buf, pl.ds(local_row, RUN_ALIGN)], sem.at[buf])
        return build

    @pl.when(tile == 0)
    def _():
        yl_s[...] = jnp.zeros(yl_s.shape, F32)
        _start_runs(tab_ref, tile, chunk_copy(0))

    @pl.when(tile + 1 < n_tok_tiles)
    def _():
        _start_runs(tab_ref, tile + 1, chunk_copy(1 - slot))

    _wait_runs(tab_ref, tile, chunk_copy(slot))
    cols = jnp.concatenate([rw_ref[...], jnp.zeros((LANES - SUBLANES, m), F32)], axis=0).T
    kl = lax.broadcasted_iota(I32, (m, lc), 1)
    weights = (jnp.where(kl == cols[:, 2:3].astype(I32), cols[:, 0:1], 0.0)
               + jnp.where(kl == cols[:, 3:4].astype(I32), cols[:, 1:2], 0.0))
    moe = jnp.dot(weights.astype(BF16), yl_s[slot].astype(BF16), preferred_element_type=F32)
    y = alpha * x1_ref[...] + (1.0 + gf_ref[...]) * moe.reshape(bb, ts, d)
    o_ref[...] = _layernorm_rows(y, g2_ref[...], b2_ref[...])


def _combine(tab_flat, x1, gf, rw, ys, g2, b2, *, bb, ts, alpha):
    b, s, d = x1.shape
    m = bb * ts
    ns = s // ts
    grid_spec = pltpu.PrefetchScalarGridSpec(
        num_scalar_prefetch=1,
        grid=(b // bb, ns),
        in_specs=[pl.BlockSpec((bb, ts, d), lambda i, j, tab: (i, j, 0)),
                  pl.BlockSpec((bb, 1, d), lambda i, j, tab: (i, 0, 0)),
                  pl.BlockSpec((SUBLANES, m), lambda i, j, tab: (0, i * ns + j)),
                  pl.BlockSpec(memory_space=pl.ANY),
                  pl.BlockSpec((1, d), lambda i, j, tab: (0, 0)),
                  pl.BlockSpec((1, d), lambda i, j, tab: (0, 0))],
        out_specs=pl.BlockSpec((bb, ts, d), lambda i, j, tab: (i, j, 0)),
        scratch_shapes=[pltpu.VMEM((2, _local_rows(m), d), F32), pltpu.SemaphoreType.DMA((2,))],
    )
    return pl.pallas_call(
        functools.partial(_combine_kernel, bb=bb, ts=ts, alpha=alpha),
        grid_spec=grid_spec,
        out_shape=jax.ShapeDtypeStruct((b, s, d), F32),
        compiler_params=_params(("arbitrary", "arbitrary"), VMEM_LIMIT),
    )(tab_flat, x1, gf, rw, ys, g2, b2)


def _layer(x, mod, p, lam_init, alpha, conv_buf, c0, n0, m0, paged, *, sample):
    b, s, d = x.shape
    sh_a, sc_a, g_a, sh_f, sc_f, g_f = mod
    if sample:
        bb, ts, act = b, s, F32
    else:
        bb, ts, act = 1, min(s, 512), BF16
    ca, q, k, v, om, g, qd, kd, vd, kdn, vdn, conv_new = _proj(
        x, sc_a, sh_a, p["wa"], p["wg"], p["wb"], p["bg"], conv_buf, p["w_conv"], p["b_conv"], p["wqk"],
        bb=bb, ts=ts, act=act)

    if sample:
        chunk = LANES
        pad_rows = lambda a: jnp.pad(a, ((0, 0), (0, chunk - s), (0, 0)))
        lane = jnp.arange(GATE_LANES)
        gate_pad = jnp.where(lane < ML_HEADS, -jnp.inf, jnp.where(lane < 2 * ML_HEADS, jnp.inf, 0.0)).astype(F32)
        g_in = jnp.concatenate([g, jnp.broadcast_to(gate_pad, (b, chunk - s, GATE_LANES))], axis=1)
        hm, c1, n1, m1 = _mlstm(pad_rows(q), pad_rows(k), pad_rows(v), g_in, pad_rows(ca), pad_rows(om),
                                c0, n0, m0, p["gn_m"], p["skip_m"], chunk=chunk, act=act)
        hm = hm[:, :s]
        cache_k, cache_v, page_table = paged
        ad = _dattn_sample(qd, kdn, vdn, cache_k, cache_v, page_table, p["lams"], p["subln_g"],
                           pps=min(PAGES_PER_STEP, page_table.shape[1]), lam_init=lam_init)
    else:
        hm, c1, n1, m1 = _mlstm(q, k, v, g, ca, om, c0, n0, m0, p["gn_m"], p["skip_m"],
                                chunk=min(s, 256), act=act)
        ad = _dattn_prompt(qd, kd, vd, p["lams"], p["subln_g"].reshape(DA_V_DIM, 1), tq=min(s, ATTN_TQ),
                           tk=min(s, ATTN_TK), lam_init=lam_init, act=act)

    t = b * s
    x1, h2, ri, rw, cnt, cb, cn = _mix(hm, ad, x, g_a, sc_f, sh_f, p["wo1"], p["wo2"], p["ln1_g"], p["ln1_b"],
                                       p["wrt"], p["brt"], bb=bb, ts=ts, alpha=alpha)
    row_tile = SAMPLE_ROW_TILE if sample else ROW_TILE
    n_tok_tiles = t // (bb * ts)
    n_tiles = -(-(2 * t + N_EXPERTS * (RUN_ALIGN - 1) * n_tok_tiles) // row_tile) + N_EXPERTS
    n_tiles_pad = -(-n_tiles // LANES) * LANES
    tab, tiles, tail = _plan(cnt, cb, cn, row_tile=row_tile, n_tiles_pad=n_tiles_pad)
    tab_flat = tab[::SUBLANES].reshape(-1)
    n_used = tiles[1, 0:1]
    xs = _scatter(tab_flat, tail[0], n_used, ri, h2, m=bb * ts, row_tile=row_tile, n_tiles=n_tiles)
    ys = _experts(tiles[0, :n_tiles], n_used, xs, p["w_gate"], p["w_up"], p["w_down"], row_tile=row_tile)
    y = _combine(tab_flat, x1, g_f, rw, ys, p["ln2_g"], p["ln2_b"], bb=bb, ts=ts, alpha=alpha)
    return y, kdn, vdn, c1, n1, m1, conv_new


def _layer_params(l, w_in, w_conv, b_conv, w_mq, w_mk, b_i, b_f, gn_m, skip_m, lam_q1, lam_k1, lam_q2, lam_k2,
                  subln_g, w_out, ln1_g, ln1_b, w_rg, b_rg, w_re, b_re, w_gate, w_up, w_down, ln2_g, ln2_b):
    w3 = 3 * ML_WIDTH
    n_gate = 2 * ML_HEADS
    wi = w_in[l]
    d = wi.shape[0]
    wg = jnp.zeros((d, GATE_LANES), F32).at[:, :n_gate].set(wi[:, w3:w3 + n_gate])
    bg = jnp.zeros((1, GATE_LANES), F32).at[0, :ML_HEADS].set(b_i[l]).at[0, ML_HEADS:n_gate].set(b_f[l])
    wrt = jnp.zeros((ROUTER_ROWS, d), F32).at[:N_GROUPS].set(w_rg[l].T).at[8:].set(w_re[l].T)
    brt = jnp.zeros((ROUTER_ROWS, 1), F32).at[:N_GROUPS, 0].set(b_rg[l]).at[8:, 0].set(b_re[l])
    return {
        "wa": wi[:, :w3].astype(BF16),
        "wg": wg.astype(BF16),
        "wb": wi[:, w3 + n_gate:].astype(BF16),
        "bg": bg,
        "w_conv": w_conv[l],
        "b_conv": b_conv[l][None, :],
        "wqk": jnp.concatenate([w_mq[l], w_mk[l]], axis=-1).astype(BF16),
        "gn_m": gn_m[l], "skip_m": skip_m[l],
        "lams": (lam_q1[l][None, :], lam_k1[l][None, :], lam_q2[l][None, :], lam_k2[l][None, :]),
        "subln_g": subln_g[l][None, :],
        "wo1": w_out[l][:ML_WIDTH].astype(BF16),
        "wo2": w_out[l][ML_WIDTH:].astype(BF16),
        "ln1_g": ln1_g[l][None, :], "ln1_b": ln1_b[l][None, :],
        "wrt": wrt, "brt": brt,
        "w_gate": w_gate[l], "w_up": w_up[l], "w_down": w_down[l],
        "ln2_g": ln2_g[l][None, :], "ln2_b": ln2_b[l][None, :],
    }


def kernel(x_prompt, x_sample, cache_k, cache_v, state_C, state_n, state_m, state_conv, page_table, c_prompt, c_sample, w_ada, b_ada, w_in, w_conv, b_conv, w_mq, w_mk, b_i, b_f, gn_m, skip_m, lam_q1, lam_k1, lam_q2, lam_k2, subln_g, w_out, ln1_g, ln1_b, w_rg, b_rg, w_re, b_re, w_gate, w_up, w_down, ln2_g, ln2_b):
    depth = w_ada.shape[0]
    bp, sp, d = x_prompt.shape
    bs, ss, _ = x_sample.shape
    alpha = (2 * depth) ** 0.25
    yp, ys = x_prompt, x_sample
    outs_p = [[] for _ in range(6)]
    outs_s = [[] for _ in range(6)]
    c_all = jnp.concatenate([c_prompt, c_sample], axis=0)
    for l in range(depth):
        p = _layer_params(l, w_in, w_conv, b_conv, w_mq, w_mk, b_i, b_f, gn_m, skip_m, lam_q1, lam_k1, lam_q2,
                          lam_k2, subln_g, w_out, ln1_g, ln1_b, w_rg, b_rg, w_re, b_re, w_gate, w_up, w_down,
                          ln2_g, ln2_b)
        lam_init = 0.8 - 0.6 * math.exp(-0.3 * l)
        mod = _ada(c_all, w_ada[l], b_ada[l][None, :])
        mod_p = tuple(mod[:bp, None, i * d:(i + 1) * d] for i in range(6))
        mod_s = tuple(mod[bp:, None, i * d:(i + 1) * d] for i in range(6))
        h, hd = ML_HEADS, ML_HEAD_DIM
        res_p = _layer(yp, mod_p, p, lam_init, alpha,
                       jnp.zeros((bp, CONV_W - 1, ML_WIDTH), F32), jnp.zeros((bp, h, hd, hd), F32),
                       jnp.zeros((bp, h, hd), F32), jnp.zeros((bp, 1, h), F32), None, sample=False)
        n_pool, page = cache_k.shape[1], cache_k.shape[2]
        paged = (cache_k[l].reshape(n_pool, page * DA_HEADS, DA_V_DIM),
                 cache_v[l].reshape(n_pool, page * DA_HEADS, DA_V_DIM), page_table)
        res_s = _layer(ys, mod_s, p, lam_init, alpha, state_conv[l], state_C[l], state_n[l],
                       state_m[l][:, None, :], paged, sample=True)
        yp, ys = res_p[0], res_s[0]
        for outs, res, nb, ns in ((outs_p, res_p, bp, sp), (outs_s, res_s, bs, ss)):
            outs[0].append(res[1].reshape(nb, ns, DA_HEADS, 2 * DA_HEAD_DIM))
            outs[1].append(res[2].reshape(nb, ns, DA_HEADS, DA_V_DIM))
            outs[2].append(res[3])
            outs[3].append(res[4])
            outs[4].append(res[5].reshape(nb, h))
            outs[5].append(res[6])
    return (yp, ys, *(jnp.stack(o) for o in outs_p), *(jnp.stack(o) for o in outs_s))
```

```python
import functools
import math

import jax
import jax.numpy as jnp
from jax import lax
from jax.experimental import pallas as pl
from jax.experimental.pallas import tpu as pltpu

F32 = jnp.float32
BF16 = jnp.bfloat16
I32 = jnp.int32
HIGHEST = lax.Precision.HIGHEST

LN_EPS = 1e-5
ML_HEADS = 4
ML_HEAD_DIM = 128
ML_WIDTH = ML_HEADS * ML_HEAD_DIM
CONV_W = 4
DA_HEADS = 4
DA_HEAD_DIM = 64
DA_V_DIM = 2 * DA_HEAD_DIM
DA_WIDTH = DA_HEADS * DA_V_DIM
N_GROUPS = 4
EXPERTS_PER_GROUP = 8
N_EXPERTS = N_GROUPS * EXPERTS_PER_GROUP
GATE_LANES = 128
ROUTER_ROWS = 48
SUBLANES = 8
LANES = 128
ROW_TILE = 512
SAMPLE_ROW_TILE = 128
ATTN_TQ = 512
ATTN_TK = 512
MLSTM_CHUNK = 256
PAGE_GROUP = 16
PAGES_PER_STEP = 16
RUN_SHIFT = 3
RUN_ALIGN = 1 << RUN_SHIFT
TABLE_N = N_EXPERTS
TABLE_LOCAL = 2 * N_EXPERTS
VMEM_LIMIT = 56 * 2 ** 20

_NT = (((1,), (1,)), ((), ()))
LOG2E = 1.4426950408889634


def _params(sem, vmem=None):
    return pltpu.CompilerParams(dimension_semantics=sem, vmem_limit_bytes=vmem)


def _sigmoid(x):
    return jax.nn.sigmoid(x)


def _log_sigmoid(x):
    return jnp.minimum(x, 0.0) - jnp.log1p(jnp.exp(-jnp.abs(x)))


def _ceil_to(x, k):
    return jnp.floor((x + (k - 1)) * (1.0 / k)) * k


def _cumsum_rows(x):
    n = x.shape[0]
    row = lax.broadcasted_iota(I32, x.shape, 0)
    shift = 1
    while shift < n:
        x = x + jnp.where(row >= shift, pltpu.roll(x, shift, axis=0), 0.0)
        shift *= 2
    return x


def _cumsum_lanes(x, n):
    lane = lax.broadcasted_iota(I32, x.shape, 1)
    shift = 1
    while shift < n:
        x = x + jnp.where(lane >= shift, pltpu.roll(x, shift, axis=1), 0.0)
        shift *= 2
    return x


def _layernorm_rows(y, g, b):
    mu = jnp.mean(y, axis=-1, keepdims=True)
    d = y - mu
    var = jnp.mean(d * d, axis=-1, keepdims=True)
    return d * lax.rsqrt(var + LN_EPS) * g + b


def _ada_kernel(c_ref, w_ref, b_ref, o_ref):
    c = c_ref[...]
    s = c * _sigmoid(c)
    o_ref[...] = jnp.dot(s, w_ref[...], preferred_element_type=F32, precision=HIGHEST) + b_ref[...]


def _ada(c, w, b):
    bc, d = c.shape
    n = w.shape[1]
    tn = 512
    return pl.pallas_call(
        _ada_kernel,
        grid=(n // tn,),
        in_specs=[pl.BlockSpec((bc, d), lambda j: (0, 0)),
                  pl.BlockSpec((d, tn), lambda j: (0, j)),
                  pl.BlockSpec((1, tn), lambda j: (0, j))],
        out_specs=pl.BlockSpec((bc, tn), lambda j: (0, j)),
        out_shape=jax.ShapeDtypeStruct((bc, n), F32),
        compiler_params=_params(("arbitrary",)),
    )(c, w, b)


def _proj_kernel(x_ref, sc_ref, sh_ref, wa_ref, wg_ref, wb_ref, bg_ref, cbuf_ref, wconv_ref, bconv_ref, wqk_ref,
                 ca_ref, q_ref, k_ref, v_ref, om_ref, g_ref, qd_ref, kd_ref, vd_ref, kdn_ref, vdn_ref, cnew_ref,
                 ext_ref, *, bb, ts):
    si = pl.program_id(1)
    m = bb * ts
    d = x_ref.shape[-1]
    h = (x_ref[...] * (1.0 + sc_ref[...]) + sh_ref[...]).reshape(m, d).astype(BF16)
    pa = jnp.dot(h, wa_ref[...], preferred_element_type=F32)
    pb = jnp.dot(h, wb_ref[...], preferred_element_type=F32)
    g = jnp.dot(h, wg_ref[...], preferred_element_type=F32) + bg_ref[...]
    g_ref[...] = g.reshape(bb, ts, GATE_LANES)
    w = ML_WIDTH
    v_ref[...] = pa[:, w:2 * w].reshape(bb, ts, w).astype(v_ref.dtype)
    om_ref[...] = pa[:, 2 * w:3 * w].reshape(bb, ts, w)
    qd_ref[...] = pb[:, 0:w].reshape(bb, ts, w).astype(qd_ref.dtype)
    kd = pb[:, w:2 * w]
    vd = pb[:, 2 * w:3 * w]
    kd_ref[...] = kd.reshape(bb, ts, w).astype(kd_ref.dtype)
    vd_ref[...] = vd.reshape(bb, ts, w).astype(vd_ref.dtype)
    for hh in range(DA_HEADS):
        cols = slice(hh * DA_V_DIM, (hh + 1) * DA_V_DIM)
        kdn_ref[:, pl.ds(hh, ts, stride=DA_HEADS), :] = kd[:, cols].reshape(bb, ts, DA_V_DIM)
        vdn_ref[:, pl.ds(hh, ts, stride=DA_HEADS), :] = vd[:, cols].reshape(bb, ts, DA_V_DIM)

    @pl.when(si == 0)
    def _():
        ext_ref[:, 5:8, :] = cbuf_ref[...]

    @pl.when(si > 0)
    def _():
        ext_ref[:, 0:8, :] = ext_ref[:, ts:ts + 8, :]

    ext_ref[:, 8:8 + ts, :] = pa[:, 0:w].reshape(bb, ts, w)
    y = bconv_ref[...]
    for j in range(CONV_W):
        y = y + wconv_ref[j:j + 1, :] * ext_ref[:, 5 + j:5 + j + ts, :]
    ca = y * _sigmoid(y)
    ca_ref[...] = ca
    cnew_ref[...] = ext_ref[:, ts + 5:ts + 8, :]

    ca2 = ca.reshape(m, w)
    hd = ML_HEAD_DIM
    for hh in range(ML_HEADS):
        qk = jnp.dot(ca2[:, hh * hd:(hh + 1) * hd].astype(BF16), wqk_ref[hh], preferred_element_type=F32)
        q_ref[:, :, hh * hd:(hh + 1) * hd] = (qk[:, 0:hd] * (hd ** -0.5)).reshape(bb, ts, hd).astype(q_ref.dtype)
        k_ref[:, :, hh * hd:(hh + 1) * hd] = qk[:, hd:2 * hd].reshape(bb, ts, hd).astype(k_ref.dtype)


def _proj(x, sc, sh, wa, wg, wb, bg, cbuf, wconv, bconv, wqk, *, bb, ts, act):
    b, s, d = x.shape
    w = ML_WIDTH
    grid = (b // bb, s // ts)
    tok = lambda n: pl.BlockSpec((bb, ts, n), lambda i, j: (i, j, 0))
    per_b = lambda r, n: pl.BlockSpec((bb, r, n), lambda i, j: (i, 0, 0))
    const = lambda shape: pl.BlockSpec(shape, lambda i, j: (0,) * len(shape))
    out_shape = (
        jax.ShapeDtypeStruct((b, s, w), F32),
        jax.ShapeDtypeStruct((b, s, w), act),
        jax.ShapeDtypeStruct((b, s, w), act),
        jax.ShapeDtypeStruct((b, s, w), act),
        jax.ShapeDtypeStruct((b, s, w), F32),
        jax.ShapeDtypeStruct((b, s, GATE_LANES), F32),
        jax.ShapeDtypeStruct((b, s, w), act),
        jax.ShapeDtypeStruct((b, s, w), act),
        jax.ShapeDtypeStruct((b, s, w), act),
        jax.ShapeDtypeStruct((b, s * DA_HEADS, DA_V_DIM), F32),
        jax.ShapeDtypeStruct((b, s * DA_HEADS, DA_V_DIM), F32),
        jax.ShapeDtypeStruct((b, CONV_W - 1, w), F32),
    )
    cache_rows = pl.BlockSpec((bb, ts * DA_HEADS, DA_V_DIM), lambda i, j: (i, j, 0))
    out_specs = (tok(w), tok(w), tok(w), tok(w), tok(w), tok(GATE_LANES), tok(w), tok(w), tok(w),
                 cache_rows, cache_rows, per_b(CONV_W - 1, w))
    return pl.pallas_call(
        functools.partial(_proj_kernel, bb=bb, ts=ts),
        grid=grid,
        in_specs=[tok(d), per_b(1, d), per_b(1, d), const(wa.shape), const(wg.shape), const(wb.shape),
                  const(bg.shape), per_b(CONV_W - 1, w), const(wconv.shape), const(bconv.shape),
                  const(wqk.shape)],
        out_specs=out_specs,
        out_shape=out_shape,
        scratch_shapes=[pltpu.VMEM((bb, ts + 8, w), F32)],
        compiler_params=_params(("arbitrary", "arbitrary"), VMEM_LIMIT),
    )(x, sc, sh, wa, wg, wb, bg, cbuf, wconv, bconv, wqk)


def _mlstm_kernel(q_ref, k_ref, v_ref, g_ref, ca_ref, om_ref, c0_ref, n0_ref, m0_ref, gn_ref, skip_ref,
                  hm_ref, c1_ref, n1_ref, m1_ref, c_s, n_s, m_s, *, chunk):
    si = pl.program_id(1)
    ln = chunk
    hd = ML_HEAD_DIM

    @pl.when(si == 0)
    def _():
        c_s[...] = c0_ref[0]
        n_s[...] = n0_ref[0]
        m_s[...] = m0_ref[0]

    g = g_ref[0]
    row = lax.broadcasted_iota(I32, (ln, GATE_LANES), 0)
    bc = _log_sigmoid(g)
    shift = 1
    while shift < ln:
        bc = bc + jnp.where(row >= shift, pltpu.roll(bc, shift, axis=0), 0.0)
        shift *= 2
    g_t = g.T
    bc_t = bc.T
    causal = lax.broadcasted_iota(I32, (ln, ln), 0) >= lax.broadcasted_iota(I32, (ln, ln), 1)

    for hh in range(ML_HEADS):
        cols = slice(hh * hd, (hh + 1) * hd)
        qb = q_ref[0, :, cols].astype(BF16)
        kf = k_ref[0, :, cols].astype(F32)
        kb = kf.astype(BF16)
        vb = v_ref[0, :, cols].astype(BF16)
        b_col = bc[:, ML_HEADS + hh:ML_HEADS + hh + 1]
        i_col = g[:, hh:hh + 1]
        b_row = bc_t[ML_HEADS + hh:ML_HEADS + hh + 1, :]
        i_row = g_t[hh:hh + 1, :]
        m_prev = m_s[:, hh:hh + 1]
        log_d = jnp.where(causal, b_col - b_row + i_row, -jnp.inf)
        inter = b_col + m_prev
        m_t = jnp.maximum(inter, jnp.max(log_d, axis=-1, keepdims=True))
        w_inter = jnp.exp(inter - m_t)
        s = lax.dot_general(qb, kb, _NT, preferred_element_type=F32) * jnp.exp(log_d - m_t)
        c_old = c_s[hh]
        n_old = n_s[hh:hh + 1, :]
        num = (w_inter * jnp.dot(qb, c_old.astype(BF16), preferred_element_type=F32)
               + jnp.dot(s.astype(BF16), vb, preferred_element_type=F32))
        den = (w_inter * jnp.sum(qb.astype(F32) * n_old, axis=-1, keepdims=True)
               + jnp.sum(s, axis=-1, keepdims=True))
        hc = num / jnp.maximum(jnp.abs(den), jnp.exp(-m_t))
        m_new = m_t[ln - 1:ln, :]
        b_last = b_col[ln - 1:ln, :]
        w_state = jnp.exp(b_last + m_prev - m_new)
        kw = jnp.exp(b_last - b_col + i_col - m_new) * kf
        c_s[hh] = w_state * c_old + jnp.dot(kw.T.astype(BF16), vb, preferred_element_type=F32)
        n_s[hh:hh + 1, :] = w_state * n_old + jnp.sum(kw, axis=0, keepdims=True)
        m_s[:, hh:hh + 1] = m_new
        mu = jnp.mean(hc, axis=-1, keepdims=True)
        dlt = hc - mu
        var = jnp.mean(dlt * dlt, axis=-1, keepdims=True)
        hn = dlt * lax.rsqrt(var + LN_EPS) * gn_ref[hh:hh + 1, :]
        out = (hn + skip_ref[hh:hh + 1, :] * ca_ref[0, :, cols]) * _sigmoid(om_ref[0, :, cols])
        hm_ref[0, :, cols] = out.astype(hm_ref.dtype)

    @pl.when(si == pl.num_programs(1) - 1)
    def _():
        c1_ref[0] = c_s[...]
        n1_ref[0] = n_s[...]
        m1_ref[0] = m_s[...]


def _mlstm(q, k, v, g, ca, om, c0, n0, m0, gn, skip, *, chunk, act):
    b, s, w = q.shape
    h, hd = ML_HEADS, ML_HEAD_DIM
    tok = lambda n: pl.BlockSpec((1, chunk, n), lambda i, j: (i, j, 0))
    c_spec = pl.BlockSpec((1, h, hd, hd), lambda i, j: (i, 0, 0, 0))
    n_spec = pl.BlockSpec((1, h, hd), lambda i, j: (i, 0, 0))
    m_spec = pl.BlockSpec((1, 1, h), lambda i, j: (i, 0, 0))
    hw_spec = pl.BlockSpec((h, hd), lambda i, j: (0, 0))
    return pl.pallas_call(
        functools.partial(_mlstm_kernel, chunk=chunk),
        grid=(b, s // chunk),
        in_specs=[tok(w), tok(w), tok(w), tok(GATE_LANES), tok(w), tok(w), c_spec, n_spec, m_spec,
                  hw_spec, hw_spec],
        out_specs=(tok(w), c_spec, n_spec, m_spec),
        out_shape=(jax.ShapeDtypeStruct((b, s, w), act),
                   jax.ShapeDtypeStruct((b, h, hd, hd), F32),
                   jax.ShapeDtypeStruct((b, h, hd), F32),
                   jax.ShapeDtypeStruct((b, 1, h), F32)),
        scratch_shapes=[pltpu.VMEM((h, hd, hd), F32), pltpu.VMEM((h, hd), F32), pltpu.VMEM((1, h), F32)],
        compiler_params=_params(("arbitrary", "arbitrary"), VMEM_LIMIT),
    )(q, k, v, g, ca, om, c0, n0, m0, gn, skip)


def _lam(lq1_ref, lk1_ref, lq2_ref, lk2_ref, lam_init):
    a = jnp.sum(lq1_ref[...] * lk1_ref[...], axis=-1, keepdims=True)
    b = jnp.sum(lq2_ref[...] * lk2_ref[...], axis=-1, keepdims=True)
    return jnp.exp(a) - jnp.exp(b) + lam_init


def _head_rms(o, sg, lam_init):
    return o * lax.rsqrt(jnp.mean(o * o, axis=-1, keepdims=True) + LN_EPS) * sg * (1.0 - lam_init)


def _softmax_update(s, vt, m_ref, l_ref, a_ref):
    m_old = m_ref[...]
    m_new = jnp.maximum(m_old, jnp.max(s, axis=-1, keepdims=True))
    alpha = jnp.exp2(m_old - m_new)
    p = jnp.exp2(s - m_new)
    l_ref[...] = alpha * l_ref[...] + jnp.sum(p, axis=-1, keepdims=True)
    a_ref[...] = alpha * a_ref[...] + jnp.dot(p.astype(BF16), vt, preferred_element_type=F32)
    m_ref[...] = m_new


def _dattn_kernel(lq1_ref, lk1_ref, lq2_ref, lk2_ref, sgc_ref, q_ref, k_ref, v_ref, o_ref,
                  vt_s, s00, s01, s10, s11, x00, x01, x10, x11, m1, l1, a1, m2, l2, a2, *, tq, tk, lam_init):
    qi = pl.program_id(2)
    n_chunks = k_ref.shape[1] // tk
    assert tq == tk
    s_s = ((s00, s01), (s10, s11))
    x_s = ((x00, x01), (x10, x11))

    @pl.when(qi == 0)
    def _():
        for c in range(n_chunks):
            vt_s[c] = v_ref[0, c * tk:(c + 1) * tk, :].astype(F32).T.astype(BF16)

    lam = _lam(lq1_ref, lk1_ref, lq2_ref, lk2_ref, lam_init)
    q = q_ref[0].astype(F32) * (DA_HEAD_DIM ** -0.5 * LOG2E)
    lane = lax.broadcasted_iota(I32, q.shape, 1)
    q1 = jnp.where(lane < DA_HEAD_DIM, q, 0.0).astype(BF16)
    q2 = jnp.where(lane >= DA_HEAD_DIM, q, 0.0).astype(BF16)
    for m_ref, l_ref, a_ref in ((m1, l1, a1), (m2, l2, a2)):
        m_ref[...] = jnp.full(m_ref.shape, -jnp.inf, F32)
        l_ref[...] = jnp.zeros(l_ref.shape, F32)
        a_ref[...] = jnp.zeros(a_ref.shape, F32)
    key_minus_query = (lax.broadcasted_iota(I32, (tk, tq), 0) - lax.broadcasted_iota(I32, (tk, tq), 1))
    maps = ((0, q1, m1, l1, a1), (1, q2, m2, l2, a2))

    def score_tile(j, slot, limit):
        kt = k_ref[0, pl.ds(pl.multiple_of(j * tk, tk), tk), :].astype(BF16)
        for mp, qz, _, _, _ in maps:
            st = lax.dot_general(kt, qz, _NT, preferred_element_type=F32)
            if limit is not None:
                st = jnp.where(key_minus_query <= limit, st, -jnp.inf)
            s_s[slot][mp][...] = st
            x_s[slot][mp][...] = jnp.max(st, axis=0, keepdims=True)

    def consume_tile(j, slot):
        vt = vt_s[j]
        for mp, _, m_ref, l_ref, a_ref in maps:
            m_old = m_ref[...]
            m_new = jnp.maximum(m_old, x_s[slot][mp][...])
            alpha = jnp.exp2(m_old - m_new)
            p = jnp.exp2(s_s[slot][mp][...] - m_new)
            l_ref[...] = alpha * l_ref[...] + jnp.sum(p, axis=0, keepdims=True)
            a_ref[...] = alpha * a_ref[...] + jnp.dot(vt, p.astype(BF16), preferred_element_type=F32)
            m_ref[...] = m_new

    def by_parity(j, next_limit, have_next=True):
        for slot in range(2):
            @pl.when((j & 1) == slot)
            def _():
                if have_next:
                    score_tile(j + 1, 1 - slot, next_limit)
                consume_tile(j, slot)

    def body(j, carry):
        by_parity(j, None)
        return carry

    score_tile(0, 0, jnp.where(qi == 0, 0, tk))
    lax.fori_loop(0, qi - 1, body, 0)

    @pl.when(qi > 0)
    def _():
        by_parity(qi - 1, 0)

    by_parity(qi, None, have_next=False)
    ot = a1[...] / l1[...] - lam * (a2[...] / l2[...])
    ot = ot * lax.rsqrt(jnp.mean(ot * ot, axis=0, keepdims=True) + LN_EPS) * sgc_ref[...] * (1.0 - lam_init)
    o_ref[0] = ot.T.astype(o_ref.dtype)


def _dattn_prompt(qd, kd, vd, lams, sgc, *, tq, tk, lam_init, act):
    b, s, w = qd.shape
    dv = DA_V_DIM
    lam_spec = pl.BlockSpec((1, DA_HEAD_DIM), lambda i, h, j: (0, 0))
    stat = pltpu.VMEM((1, tq), F32)
    acc = pltpu.VMEM((dv, tq), F32)
    return pl.pallas_call(
        functools.partial(_dattn_kernel, tq=tq, tk=tk, lam_init=lam_init),
        grid=(b, DA_HEADS, s // tq),
        in_specs=[lam_spec, lam_spec, lam_spec, lam_spec,
                  pl.BlockSpec((dv, 1), lambda i, h, j: (0, 0)),
                  pl.BlockSpec((1, tq, dv), lambda i, h, j: (i, j, h)),
                  pl.BlockSpec((1, s, dv), lambda i, h, j: (i, 0, h)),
                  pl.BlockSpec((1, s, dv), lambda i, h, j: (i, 0, h))],
        out_specs=pl.BlockSpec((1, tq, dv), lambda i, h, j: (i, j, h)),
        out_shape=jax.ShapeDtypeStruct((b, s, w), act),
        scratch_shapes=[pltpu.VMEM((s // tk, dv, tk), BF16),
                        *([pltpu.VMEM((tk, tq), F32)] * 4),
                        stat, stat, stat, stat,
                        stat, stat, acc, stat, stat, acc],
        compiler_params=_params(("arbitrary", "arbitrary", "arbitrary"), VMEM_LIMIT),
    )(*lams, sgc, qd, kd, vd)


def _sattn_kernel(pt_ref, lq1_ref, lk1_ref, lq2_ref, lk2_ref, sg_ref, q_ref, kn_ref, vn_ref, *rest,
                  pps, sd, lam_init):
    del pt_ref
    k_refs = rest[0:pps]
    v_refs = rest[pps:2 * pps]
    o_ref, qx_s, m_s, l_s, a_s = rest[2 * pps:]
    j = pl.program_id(1)
    nq = DA_HEADS * sd
    nrow = 2 * nq
    dv = DA_V_DIM

    @pl.when(j == 0)
    def _():
        q = q_ref[0].astype(F32) * (DA_HEAD_DIM ** -0.5 * LOG2E)
        qh = jnp.concatenate([q[:, hh * dv:(hh + 1) * dv] for hh in range(DA_HEADS)], axis=0)
        lane = lax.broadcasted_iota(I32, qh.shape, 1)
        qx_s[...] = jnp.concatenate([jnp.where(lane < DA_HEAD_DIM, qh, 0.0),
                                     jnp.where(lane >= DA_HEAD_DIM, qh, 0.0)], axis=0).astype(BF16)
        m_s[...] = jnp.full(m_s.shape, -jnp.inf, F32)
        l_s[...] = jnp.zeros(l_s.shape, F32)
        a_s[...] = jnp.zeros(a_s.shape, F32)

    qx = qx_s[...]
    n_keys = k_refs[0].shape[1]
    r = lax.broadcasted_iota(I32, (nrow, n_keys), 0)
    c = lax.broadcasted_iota(I32, (nrow, n_keys), 1)
    bias = jnp.where(((r // sd) % DA_HEADS) == (c % DA_HEADS), 0.0, -jnp.inf)
    m_run = m_s[...]
    lsum = l_s[...]
    acc = a_s[...]
    for first in range(0, pps, PAGE_GROUP):
        group = range(first, min(first + PAGE_GROUP, pps))
        scores = [lax.dot_general(qx, k_refs[p][0].astype(BF16), _NT, preferred_element_type=F32) + bias
                  for p in group]
        m_new = m_run
        for sp in scores:
            m_new = jnp.maximum(m_new, jnp.max(sp, axis=-1, keepdims=True))
        alpha = jnp.exp2(m_run - m_new)
        lsum = alpha * lsum
        acc = alpha * acc
        for p, sp in zip(group, scores):
            pp = jnp.exp2(sp - m_new)
            lsum = lsum + jnp.sum(pp, axis=-1, keepdims=True)
            acc = acc + jnp.dot(pp.astype(BF16), v_refs[p][0].astype(BF16), preferred_element_type=F32)
        m_run = m_new
    l_s[...] = lsum
    a_s[...] = acc
    m_s[...] = m_run

    @pl.when(j == pl.num_programs(1) - 1)
    def _():
        pad = jnp.zeros((LANES - nq, dv), F32)
        kn = jnp.concatenate([kn_ref[0], pad], axis=0).astype(BF16)
        vn = jnp.concatenate([vn_ref[0], pad], axis=0).astype(BF16)
        sn = lax.dot_general(qx, kn, _NT, preferred_element_type=F32)
        rn = lax.broadcasted_iota(I32, sn.shape, 0)
        cn = lax.broadcasted_iota(I32, sn.shape, 1)
        valid = (cn < nq) & ((cn % DA_HEADS) == ((rn // sd) % DA_HEADS)) & ((cn // DA_HEADS) <= (rn % sd))
        _softmax_update(jnp.where(valid, sn, -jnp.inf), vn, m_s, l_s, a_s)
        lam = _lam(lq1_ref, lk1_ref, lq2_ref, lk2_ref, lam_init)
        o = a_s[0:nq, :] / l_s[0:nq, :] - lam * (a_s[nq:nrow, :] / l_s[nq:nrow, :])
        o = _head_rms(o, sg_ref[...], lam_init)
        for hh in range(DA_HEADS):
            o_ref[0, :, hh * dv:(hh + 1) * dv] = o[hh * sd:(hh + 1) * sd, :].astype(o_ref.dtype)


def _dattn_sample(qd, kn, vn, cache_k, cache_v, page_table, lams, sg, *, pps, lam_init):
    b, sd, w = qd.shape
    n_pages = page_table.shape[1]
    n_keys, dv = cache_k.shape[1], cache_k.shape[2]
    h = DA_HEADS
    lam_spec = pl.BlockSpec((1, DA_HEAD_DIM), lambda i, j, pt: (0, 0))
    tok = pl.BlockSpec((1, sd, w), lambda i, j, pt: (i, 0, 0))
    new_rows = pl.BlockSpec((1, sd * h, dv), lambda i, j, pt: (i, 0, 0))

    nrow = 2 * h * sd

    def page_spec(p):
        return pl.BlockSpec((1, n_keys, dv), lambda i, j, pt: (pt[i, j * pps + p], 0, 0))

    grid_spec = pltpu.PrefetchScalarGridSpec(
        num_scalar_prefetch=1,
        grid=(b, n_pages // pps),
        in_specs=[lam_spec, lam_spec, lam_spec, lam_spec,
                  pl.BlockSpec((1, dv), lambda i, j, pt: (0, 0)),
                  tok, new_rows, new_rows]
                 + [page_spec(p) for p in range(pps)] + [page_spec(p) for p in range(pps)],
        out_specs=tok,
        scratch_shapes=[pltpu.VMEM((nrow, dv), BF16), pltpu.VMEM((nrow, 1), F32), pltpu.VMEM((nrow, 1), F32),
                        pltpu.VMEM((nrow, dv), F32)],
    )
    return pl.pallas_call(
        functools.partial(_sattn_kernel, pps=pps, sd=sd, lam_init=lam_init),
        grid_spec=grid_spec,
        out_shape=jax.ShapeDtypeStruct((b, sd, w), F32),
        compiler_params=_params(("arbitrary", "arbitrary"), VMEM_LIMIT),
    )(page_table, *lams, sg, qd, kn, vn, *([cache_k] * pps), *([cache_v] * pps))


def _mix_kernel(hm_ref, ad_ref, x_ref, ga_ref, scf_ref, shf_ref, wo1_ref, wo2_ref, g1_ref, b1_ref,
                wrh_ref, wrl_ref, brt_ref, x1_ref, h2_ref, ri_ref, rw_ref, cnt_ref, cb_ref, cn_ref,
                carry_s, carry_row_s,
                *, bb, ts, alpha):
    m = bb * ts
    d = x_ref.shape[-1]

    @pl.when((pl.program_id(0) == 0) & (pl.program_id(1) == 0))
    def _():
        carry_s[...] = jnp.zeros(carry_s.shape, F32)
        carry_row_s[...] = jnp.zeros(carry_row_s.shape, F32)

    hm = hm_ref[...].reshape(m, ML_WIDTH).astype(BF16)
    ad = ad_ref[...].reshape(m, DA_WIDTH).astype(BF16)
    mixed = (jnp.dot(hm, wo1_ref[...], preferred_element_type=F32)
             + jnp.dot(ad, wo2_ref[...], preferred_element_type=F32))
    y = alpha * x_ref[...] + (1.0 + ga_ref[...]) * mixed.reshape(bb, ts, d)
    x1 = _layernorm_rows(y, g1_ref[...], b1_ref[...])
    x1_ref[...] = x1
    h2 = (x1 * (1.0 + scf_ref[...]) + shf_ref[...]).reshape(m, d)
    h2_ref[...] = h2

    h2_hi = h2.astype(BF16)
    h2_lo = (h2 - h2_hi.astype(F32)).astype(BF16)
    lt = (lax.dot_general(wrh_ref[...], h2_hi, _NT, preferred_element_type=F32)
          + lax.dot_general(wrl_ref[...], h2_hi, _NT, preferred_element_type=F32)
          + lax.dot_general(wrh_ref[...], h2_lo, _NT, preferred_element_type=F32)) + brt_ref[...]
    gl = lt[0:N_GROUPS]
    gmax = jnp.max(gl, axis=0, keepdims=True)
    r4 = lax.broadcasted_iota(I32, gl.shape, 0)
    gidx = jnp.min(jnp.where(gl == gmax, r4, N_GROUPS), axis=0, keepdims=True)
    gp = 1.0 / jnp.sum(jnp.exp(gl - gmax), axis=0, keepdims=True)
    epg = EXPERTS_PER_GROUP
    esel = lt[8 + (N_GROUPS - 1) * epg:8 + N_GROUPS * epg]
    for grp in range(N_GROUPS - 2, -1, -1):
        esel = jnp.where(gidx == grp, lt[8 + grp * epg:8 + (grp + 1) * epg], esel)
    r8 = lax.broadcasted_iota(I32, esel.shape, 0)
    t1 = jnp.max(esel, axis=0, keepdims=True)
    i1 = jnp.min(jnp.where(esel == t1, r8, epg), axis=0, keepdims=True)
    rest = jnp.where(r8 == i1, -jnp.inf, esel)
    t2 = jnp.max(rest, axis=0, keepdims=True)
    i2 = jnp.min(jnp.where(rest == t2, r8, epg), axis=0, keepdims=True)
    z = jnp.exp(t2 - t1)
    w1 = gp / (1.0 + z)
    w2 = gp * z / (1.0 + z)
    e0 = gidx * epg + i1
    e1 = gidx * epg + i2

    r32 = lax.broadcasted_iota(I32, (N_EXPERTS, m), 0)
    hit0 = r32 == e0
    hit1 = r32 == e1
    onehot = jnp.where(hit0, 1.0, jnp.where(hit1, 1.0, 0.0))
    onehot_b = onehot.astype(BF16)
    before = (lax.broadcasted_iota(I32, (m, m), 0) < lax.broadcasted_iota(I32, (m, m), 1))
    prefix = jnp.dot(onehot_b, jnp.where(before, 1.0, 0.0).astype(BF16), preferred_element_type=F32)
    cnt_col = jnp.sum(onehot, axis=1, keepdims=True)
    run = jnp.broadcast_to(_ceil_to(cnt_col, RUN_ALIGN), (N_EXPERTS, LANES))
    start = _cumsum_rows(run) - run
    slot = prefix + start[:, 0:1]
    slot0 = jnp.sum(jnp.where(hit0, slot, 0.0), axis=0, keepdims=True)
    slot1 = jnp.sum(jnp.where(hit1, slot, 0.0), axis=0, keepdims=True)
    rr = lax.broadcasted_iota(I32, (SUBLANES, m), 0)
    ri_ref[...] = jnp.where(rr == 0, slot0, jnp.where(rr == 1, slot1, 0.0)).astype(I32)
    rw_ref[...] = jnp.where(rr == 0, w1, jnp.where(rr == 1, w2, jnp.where(rr == 2, slot0,
                            jnp.where(rr == 3, slot1, 0.0))))
    padded_hot = jnp.concatenate([onehot_b, jnp.zeros((LANES - N_EXPERTS, m), BF16)], axis=0)
    cnt_row = lax.dot_general(jnp.ones((SUBLANES, m), BF16), padded_hot, _NT, preferred_element_type=F32)
    cb_ref[...] = carry_row_s[...]
    cn_ref[...] = cnt_row
    carry_row_s[...] = carry_row_s[...] + _ceil_to(cnt_row, RUN_ALIGN)
    carry_s[...] = carry_s[...] + run
    cnt_ref[...] = carry_s[...]


def _mix(hm, ad, x, ga, scf, shf, wo1, wo2, g1, b1, wrh, wrl, brt, *, bb, ts, alpha):
    b, s, d = x.shape
    m = bb * ts
    t = b * s
    ns = s // ts
    n_tok_tiles = t // m
    tok = lambda n: pl.BlockSpec((bb, ts, n), lambda i, j: (i, j, 0))
    per_b = pl.BlockSpec((bb, 1, d), lambda i, j: (i, 0, 0))
    const = lambda shape: pl.BlockSpec(shape, lambda i, j: (0,) * len(shape))
    lin = pl.BlockSpec((SUBLANES, m), lambda i, j: (0, i * ns + j))
    per_tile = pl.BlockSpec((SUBLANES, LANES), lambda i, j: (i * ns + j, 0))
    cnt_shape = (N_EXPERTS, LANES)
    return pl.pallas_call(
        functools.partial(_mix_kernel, bb=bb, ts=ts, alpha=alpha),
        grid=(b // bb, ns),
        in_specs=[tok(ML_WIDTH), tok(DA_WIDTH), tok(d), per_b, per_b, per_b, const(wo1.shape), const(wo2.shape),
                  const(g1.shape), const(b1.shape), const(wrh.shape), const(wrl.shape), const(brt.shape)],
        out_specs=(tok(d), pl.BlockSpec((m, d), lambda i, j: (i * ns + j, 0)), lin, lin, const(cnt_shape),
                   per_tile, per_tile),
        out_shape=(jax.ShapeDtypeStruct((b, s, d), F32),
                   jax.ShapeDtypeStruct((t, d), F32),
                   jax.ShapeDtypeStruct((SUBLANES, t), I32),
                   jax.ShapeDtypeStruct((SUBLANES, t), F32),
                   jax.ShapeDtypeStruct(cnt_shape, F32),
                   jax.ShapeDtypeStruct((n_tok_tiles * SUBLANES, LANES), F32),
                   jax.ShapeDtypeStruct((n_tok_tiles * SUBLANES, LANES), F32)),
        scratch_shapes=[pltpu.VMEM(cnt_shape, F32), pltpu.VMEM((SUBLANES, LANES), F32)],
        compiler_params=_params(("arbitrary", "arbitrary"), VMEM_LIMIT),
    )(hm, ad, x, ga, scf, shf, wo1, wo2, g1, b1, wrh, wrl, brt)


def _plan_kernel(cnt_ref, cb_ref, cn_ref, tab_ref, tile_ref, tail_ref, *, row_tile):
    padded = _ceil_to(cnt_ref[...], row_tile)
    ends = _cumsum_rows(padded)
    nt = tile_ref.shape[1]
    first_row = (lax.broadcasted_iota(I32, (N_EXPERTS, nt), 1) * row_tile).astype(F32)
    done = jnp.sum(jnp.where(ends[:, 0:1] <= first_row, 1, 0), axis=0, keepdims=True)
    expert = jnp.minimum(done, N_EXPERTS - 1)
    used = (ends[N_EXPERTS - 1:N_EXPERTS, 0:1] * (1.0 / row_tile)).astype(I32)
    rt = lax.broadcasted_iota(I32, (SUBLANES, nt), 0)
    tile_ref[...] = jnp.where(rt == 0, expert, jnp.where(rt == 1, used, 0))
    n_rows = cb_ref.shape[0]
    cb = cb_ref[...]
    cn = cn_ref[...]
    lane8 = lax.broadcasted_iota(I32, (SUBLANES, LANES), 1)
    total = cb[n_rows - SUBLANES:n_rows, :] + _ceil_to(cn[n_rows - SUBLANES:n_rows, :], RUN_ALIGN)
    live8 = lane8 < N_EXPERTS
    region = jnp.where(live8, _ceil_to(total, row_tile), 0.0)
    offs8 = _cumsum_lanes(region, N_EXPERTS) - region
    offs = offs8[0:1, :]
    tail_ref[...] = (jnp.where(live8, offs8 + total, 0.0)
                     + pltpu.roll(jnp.where(live8, (region - total) * (1.0 / RUN_ALIGN), 0.0), TABLE_N, axis=1)
                     ).astype(I32)
    run = _ceil_to(cn, RUN_ALIGN)
    local = _cumsum_lanes(run, N_EXPERTS) - run
    live = lax.broadcasted_iota(I32, cb.shape, 1) < N_EXPERTS
    tab = (jnp.where(live, cb + offs, 0.0)
           + pltpu.roll(jnp.where(live, cn, 0.0), TABLE_N, axis=1)
           + pltpu.roll(jnp.where(live, local, 0.0), TABLE_LOCAL, axis=1))
    tab_ref[...] = tab.astype(I32)


def _plan(cnt, cb, cn, *, row_tile, n_tiles_pad):
    full = lambda a: pl.BlockSpec(a.shape, lambda i: (0, 0))
    return pl.pallas_call(
        functools.partial(_plan_kernel, row_tile=row_tile),
        grid=(1,),
        in_specs=[full(cnt), full(cb), full(cn)],
        out_specs=(full(cb), pl.BlockSpec((SUBLANES, n_tiles_pad), lambda i: (0, 0)),
                   pl.BlockSpec((SUBLANES, LANES), lambda i: (0, 0))),
        out_shape=(jax.ShapeDtypeStruct(cb.shape, I32), jax.ShapeDtypeStruct((SUBLANES, n_tiles_pad), I32),
                   jax.ShapeDtypeStruct((SUBLANES, LANES), I32)),
        compiler_params=_params(("arbitrary",)),
    )(cnt, cb, cn)


def _local_rows(m):
    return 2 * m + N_EXPERTS * RUN_ALIGN


def _run_chunks(tab_ref, tile, e):
    return (tab_ref[tile * LANES + TABLE_N + e] + (RUN_ALIGN - 1)) >> RUN_SHIFT


def _start_runs(tab_ref, tile, chunk_copy):
    base = tile * LANES

    def per_expert(e, carry):
        first = tab_ref[base + e]
        local = tab_ref[base + TABLE_LOCAL + e]

        def per_chunk(c, carry):
            off = c * RUN_ALIGN
            chunk_copy(pl.multiple_of(local + off, RUN_ALIGN), pl.multiple_of(first + off, RUN_ALIGN)).start()
            return carry

        lax.fori_loop(0, _run_chunks(tab_ref, tile, e), per_chunk, 0)
        return carry

    lax.fori_loop(0, N_EXPERTS, per_expert, 0)


def _wait_runs(tab_ref, tile, chunk_copy):
    total = lax.fori_loop(0, N_EXPERTS, lambda e, acc: acc + _run_chunks(tab_ref, tile, e), 0)

    def wait_one(c, carry):
        chunk_copy(0, 0).wait()
        return carry

    lax.fori_loop(0, total, wait_one, 0)


def _scatter_kernel(tab_ref, tail_ref, nu_ref, ri_ref, h2_ref, xs_ref, xl_s, zero_s, sem, zsem, *, row_tile):
    m = h2_ref.shape[0]
    lc = xl_s.shape[0]
    i = pl.program_id(0)

    @pl.when(i == 0)
    def _():
        zero_s[...] = jnp.zeros(zero_s.shape, F32)

        def zero_chunk(row):
            return pltpu.make_async_copy(zero_s, xs_ref.at[pl.ds(pl.multiple_of(row, RUN_ALIGN), RUN_ALIGN)], zsem)

        def span(first, n_chunks):
            def body(c, carry):
                zero_chunk(first + c * RUN_ALIGN).start()
                return carry
            lax.fori_loop(0, n_chunks, body, 0)
            return n_chunks

        def per_expert(e, total):
            return total + span(tail_ref[e], tail_ref[TABLE_N + e])

        total = lax.fori_loop(0, N_EXPERTS, per_expert, 0)
        used_rows = nu_ref[0] * row_tile
        total = total + span(used_rows, (xs_ref.shape[0] - used_rows) >> RUN_SHIFT)

        def wait_one(c, carry):
            zero_chunk(0).wait()
            return carry

        lax.fori_loop(0, total, wait_one, 0)

    k = lax.broadcasted_iota(I32, (lc, m), 0)
    pick = jnp.where(k == ri_ref[0:1, :], 1.0, jnp.where(k == ri_ref[1:2, :], 1.0, 0.0)).astype(BF16)
    xl_s[...] = jnp.dot(pick, h2_ref[...].astype(BF16), preferred_element_type=F32)

    def chunk_copy(local_row, sorted_row):
        return pltpu.make_async_copy(xl_s.at[pl.ds(local_row, RUN_ALIGN)], xs_ref.at[pl.ds(sorted_row, RUN_ALIGN)],
                                     sem)

    _start_runs(tab_ref, i, chunk_copy)
    _wait_runs(tab_ref, i, chunk_copy)


def _scatter(tab_flat, tail_flat, n_used, ri, h2, *, m, row_tile, n_tiles):
    t, d = h2.shape
    grid_spec = pltpu.PrefetchScalarGridSpec(
        num_scalar_prefetch=3,
        grid=(t // m,),
        in_specs=[pl.BlockSpec((SUBLANES, m), lambda i, tab, tail, nu: (0, i)),
                  pl.BlockSpec((m, d), lambda i, tab, tail, nu: (i, 0))],
        out_specs=pl.BlockSpec(memory_space=pl.ANY),
        scratch_shapes=[pltpu.VMEM((_local_rows(m), d), F32), pltpu.VMEM((RUN_ALIGN, d), F32),
                        pltpu.SemaphoreType.DMA(()), pltpu.SemaphoreType.DMA(())],
    )
    return pl.pallas_call(
        functools.partial(_scatter_kernel, row_tile=row_tile),
        grid_spec=grid_spec,
        out_shape=jax.ShapeDtypeStruct((n_tiles * row_tile, d), F32),
        compiler_params=_params(("arbitrary",), VMEM_LIMIT),
    )(tab_flat, tail_flat, n_used, ri, h2)


def _experts_kernel(te_ref, nu_ref, x_ref, wg_ref, wu_ref, wd_ref, y_ref, wgb, wub, wdb):
    i = pl.program_id(0)

    @pl.when(i < nu_ref[0])
    def _():
        @pl.when((i == 0) | (te_ref[i] != te_ref[jnp.maximum(i - 1, 0)]))
        def _():
            wgb[...] = wg_ref[0].astype(BF16)
            wub[...] = wu_ref[0].astype(BF16)
            wdb[...] = wd_ref[0].astype(BF16)

        x = x_ref[...].astype(BF16)
        a = jnp.dot(x, wgb[...], preferred_element_type=F32)
        u = jnp.dot(x, wub[...], preferred_element_type=F32)
        act = (a * _sigmoid(a)) * u
        y_ref[...] = jnp.dot(act.astype(BF16), wdb[...], preferred_element_type=F32)

    @pl.when(i >= nu_ref[0])
    def _():
        y_ref[...] = jnp.zeros(y_ref.shape, F32)


def _experts(tile_expert, n_used, xs, w_gate, w_up, w_down, *, row_tile):
    p, d = xs.shape
    de = w_gate.shape[-1]
    row_map = lambda i, te, nu: (jnp.maximum(jnp.minimum(i, nu[0] - 1), 0), 0)
    grid_spec = pltpu.PrefetchScalarGridSpec(
        num_scalar_prefetch=2,
        grid=(p // row_tile,),
        in_specs=[pl.BlockSpec((row_tile, d), row_map),
                  pl.BlockSpec((1, d, de), lambda i, te, nu: (te[i], 0, 0)),
                  pl.BlockSpec((1, d, de), lambda i, te, nu: (te[i], 0, 0)),
                  pl.BlockSpec((1, de, d), lambda i, te, nu: (te[i], 0, 0))],
        out_specs=pl.BlockSpec((row_tile, d), lambda i, te, nu: (i, 0)),
        scratch_shapes=[pltpu.VMEM((d, de), BF16), pltpu.VMEM((d, de), BF16), pltpu.VMEM((de, d), BF16)],
    )
    return pl.pallas_call(
        _experts_kernel,
        grid_spec=grid_spec,
        out_shape=jax.ShapeDtypeStruct((p, d), F32),
        compiler_params=_params(("arbitrary",), VMEM_LIMIT),
    )(tile_expert, n_used, xs, w_gate, w_up, w_down)


def _combine_kernel(tab_ref, x1_ref, gf_ref, rw_ref, ys_ref, g2_ref, b2_ref, o_ref, yl_s, sem,
                    *, bb, ts, alpha):
    m = bb * ts
    d = x1_ref.shape[-1]
    lc = yl_s.shape[1]
    tile = pl.program_id(0) * pl.num_programs(1) + pl.program_id(1)
    n_tok_tiles = pl.num_programs(0) * pl.num_programs(1)
    slot = tile & 1

    def chunk_copy(buf):
        def build(local_row, sorted_row):
            return pltpu.make_async_copy(ys_ref.at[pl.ds(sorted_row, RUN_ALIGN)],
                                         yl_s.at[buf, pl.ds(local_row, RUN_ALIGN)], sem.at[buf])
        return build

    @pl.when(tile == 0)
    def _():
        yl_s[...] = jnp.zeros(yl_s.shape, F32)
        _start_runs(tab_ref, tile, chunk_copy(0))

    @pl.when(tile + 1 < n_tok_tiles)
    def _():
        _start_runs(tab_ref, tile + 1, chunk_copy(1 - slot))

    _wait_runs(tab_ref, tile, chunk_copy(slot))
    cols = jnp.concatenate([rw_ref[...], jnp.zeros((LANES - SUBLANES, m), F32)], axis=0).T
    kl = lax.broadcasted_iota(I32, (m, lc), 1)
    weights = (jnp.where(kl == cols[:, 2:3].astype(I32), cols[:, 0:1], 0.0)
               + jnp.where(kl == cols[:, 3:4].astype(I32), cols[:, 1:2], 0.0))
    moe = jnp.dot(weights.astype(BF16), yl_s[slot].astype(BF16), preferred_element_type=F32)
    y = alpha * x1_ref[...] + (1.0 + gf_ref[...]) * moe.reshape(bb, ts, d)
    o_ref[...] = _layernorm_rows(y, g2_ref[...], b2_ref[...])


def _combine(tab_flat, x1, gf, rw, ys, g2, b2, *, bb, ts, alpha):
    b, s, d = x1.shape
    m = bb * ts
    ns = s // ts
    grid_spec = pltpu.PrefetchScalarGridSpec(
        num_scalar_prefetch=1,
        grid=(b // bb, ns),
        in_specs=[pl.BlockSpec((bb, ts, d), lambda i, j, tab: (i, j, 0)),
                  pl.BlockSpec((bb, 1, d), lambda i, j, tab: (i, 0, 0)),
                  pl.BlockSpec((SUBLANES, m), lambda i, j, tab: (0, i * ns + j)),
                  pl.BlockSpec(memory_space=pl.ANY),
                  pl.BlockSpec((1, d), lambda i, j, tab: (0, 0)),
                  pl.BlockSpec((1, d), lambda i, j, tab: (0, 0))],
        out_specs=pl.BlockSpec((bb, ts, d), lambda i, j, tab: (i, j, 0)),
        scratch_shapes=[pltpu.VMEM((2, _local_rows(m), d), F32), pltpu.SemaphoreType.DMA((2,))],
    )
    return pl.pallas_call(
        functools.partial(_combine_kernel, bb=bb, ts=ts, alpha=alpha),
        grid_spec=grid_spec,
        out_shape=jax.ShapeDtypeStruct((b, s, d), F32),
        compiler_params=_params(("arbitrary", "arbitrary"), VMEM_LIMIT),
    )(tab_flat, x1, gf, rw, ys, g2, b2)


def _layer(x, mod, p, lam_init, alpha, conv_buf, c0, n0, m0, paged, *, sample):
    b, s, d = x.shape
    sh_a, sc_a, g_a, sh_f, sc_f, g_f = mod
    if sample:
        bb, ts, act = b, s, F32
    else:
        bb, ts, act = 1, min(s, 512), BF16
    ca, q, k, v, om, g, qd, kd, vd, kdn, vdn, conv_new = _proj(
        x, sc_a, sh_a, p["wa"], p["wg"], p["wb"], p["bg"], conv_buf, p["w_conv"], p["b_conv"], p["wqk"],
        bb=bb, ts=ts, act=act)

    if sample:
        chunk = LANES
        pad_rows = lambda a: jnp.pad(a, ((0, 0), (0, chunk - s), (0, 0)))
        lane = jnp.arange(GATE_LANES)
        gate_pad = jnp.where(lane < ML_HEADS, -jnp.inf, jnp.where(lane < 2 * ML_HEADS, jnp.inf, 0.0)).astype(F32)
        g_in = jnp.concatenate([g, jnp.broadcast_to(gate_pad, (b, chunk - s, GATE_LANES))], axis=1)
        hm, c1, n1, m1 = _mlstm(pad_rows(q), pad_rows(k), pad_rows(v), g_in, pad_rows(ca), pad_rows(om),
                                c0, n0, m0, p["gn_m"], p["skip_m"], chunk=chunk, act=act)
        hm = hm[:, :s]
        cache_k, cache_v, page_table = paged
        ad = _dattn_sample(qd, kdn, vdn, cache_k, cache_v, page_table, p["lams"], p["subln_g"],
                           pps=min(PAGES_PER_STEP, page_table.shape[1]), lam_init=lam_init)
    else:
        hm, c1, n1, m1 = _mlstm(q, k, v, g, ca, om, c0, n0, m0, p["gn_m"], p["skip_m"],
                                chunk=min(s, MLSTM_CHUNK), act=act)
        ad = _dattn_prompt(qd, kd, vd, p["lams"], p["subln_g"].reshape(DA_V_DIM, 1), tq=min(s, ATTN_TQ),
                           tk=min(s, ATTN_TK), lam_init=lam_init, act=act)

    t = b * s
    x1, h2, ri, rw, cnt, cb, cn = _mix(hm, ad, x, g_a, sc_f, sh_f, p["wo1"], p["wo2"], p["ln1_g"], p["ln1_b"],
                                       p["wrh"], p["wrl"], p["brt"], bb=bb, ts=ts, alpha=alpha)
    row_tile = SAMPLE_ROW_TILE if sample else ROW_TILE
    n_tok_tiles = t // (bb * ts)
    n_tiles = -(-(2 * t + N_EXPERTS * (RUN_ALIGN - 1) * n_tok_tiles) // row_tile) + N_EXPERTS
    n_tiles_pad = -(-n_tiles // LANES) * LANES
    tab, tiles, tail = _plan(cnt, cb, cn, row_tile=row_tile, n_tiles_pad=n_tiles_pad)
    tab_flat = tab[::SUBLANES].reshape(-1)
    n_used = tiles[1, 0:1]
    xs = _scatter(tab_flat, tail[0], n_used, ri, h2, m=bb * ts, row_tile=row_tile, n_tiles=n_tiles)
    ys = _experts(tiles[0, :n_tiles], n_used, xs, p["w_gate"], p["w_up"], p["w_down"], row_tile=row_tile)
    y = _combine(tab_flat, x1, g_f, rw, ys, p["ln2_g"], p["ln2_b"], bb=bb, ts=ts, alpha=alpha)
    return y, kdn, vdn, c1, n1, m1, conv_new


def _layer_params(l, w_in, w_conv, b_conv, w_mq, w_mk, b_i, b_f, gn_m, skip_m, lam_q1, lam_k1, lam_q2, lam_k2,
                  subln_g, w_out, ln1_g, ln1_b, w_rg, b_rg, w_re, b_re, w_gate, w_up, w_down, ln2_g, ln2_b):
    w3 = 3 * ML_WIDTH
    n_gate = 2 * ML_HEADS
    wi = w_in[l]
    d = wi.shape[0]
    wg = jnp.zeros((d, GATE_LANES), F32).at[:, :n_gate].set(wi[:, w3:w3 + n_gate])
    bg = jnp.zeros((1, GATE_LANES), F32).at[0, :ML_HEADS].set(b_i[l]).at[0, ML_HEADS:n_gate].set(b_f[l])
    wrt = jnp.zeros((ROUTER_ROWS, d), F32).at[:N_GROUPS].set(w_rg[l].T).at[8:8 + N_EXPERTS].set(w_re[l].T)
    brt = jnp.zeros((ROUTER_ROWS, 1), F32).at[:N_GROUPS, 0].set(b_rg[l]).at[8:8 + N_EXPERTS, 0].set(b_re[l])
    return {
        "wa": wi[:, :w3].astype(BF16),
        "wg": wg.astype(BF16),
        "wb": wi[:, w3 + n_gate:].astype(BF16),
        "bg": bg,
        "w_conv": w_conv[l],
        "b_conv": b_conv[l][None, :],
        "wqk": jnp.concatenate([w_mq[l], w_mk[l]], axis=-1).astype(BF16),
        "gn_m": gn_m[l], "skip_m": skip_m[l],
        "lams": (lam_q1[l][None, :], lam_k1[l][None, :], lam_q2[l][None, :], lam_k2[l][None, :]),
        "subln_g": subln_g[l][None, :],
        "wo1": w_out[l][:ML_WIDTH].astype(BF16),
        "wo2": w_out[l][ML_WIDTH:].astype(BF16),
        "ln1_g": ln1_g[l][None, :], "ln1_b": ln1_b[l][None, :],
        "wrh": wrt.astype(BF16), "wrl": (wrt - wrt.astype(BF16).astype(F32)).astype(BF16), "brt": brt,
        "w_gate": w_gate[l], "w_up": w_up[l], "w_down": w_down[l],
        "ln2_g": ln2_g[l][None, :], "ln2_b": ln2_b[l][None, :],
    }


def kernel(x_prompt, x_sample, cache_k, cache_v, state_C, state_n, state_m, state_conv, page_table, c_prompt, c_sample, w_ada, b_ada, w_in, w_conv, b_conv, w_mq, w_mk, b_i, b_f, gn_m, skip_m, lam_q1, lam_k1, lam_q2, lam_k2, subln_g, w_out, ln1_g, ln1_b, w_rg, b_rg, w_re, b_re, w_gate, w_up, w_down, ln2_g, ln2_b):
    depth = w_ada.shape[0]
    bp, sp, d = x_prompt.shape
    bs, ss, _ = x_sample.shape
    alpha = (2 * depth) ** 0.25
    yp, ys = x_prompt, x_sample
    outs_p = [[] for _ in range(6)]
    outs_s = [[] for _ in range(6)]
    c_all = jnp.concatenate([c_prompt, c_sample], axis=0)
    for l in range(depth):
        p = _layer_params(l, w_in, w_conv, b_conv, w_mq, w_mk, b_i, b_f, gn_m, skip_m, lam_q1, lam_k1, lam_q2,
                          lam_k2, subln_g, w_out, ln1_g, ln1_b, w_rg, b_rg, w_re, b_re, w_gate, w_up, w_down,
                          ln2_g, ln2_b)
        lam_init = 0.8 - 0.6 * math.exp(-0.3 * l)
        mod = _ada(c_all, w_ada[l], b_ada[l][None, :])
        mod_p = tuple(mod[:bp, None, i * d:(i + 1) * d] for i in range(6))
        mod_s = tuple(mod[bp:, None, i * d:(i + 1) * d] for i in range(6))
        h, hd = ML_HEADS, ML_HEAD_DIM
        res_p = _layer(yp, mod_p, p, lam_init, alpha,
                       jnp.zeros((bp, CONV_W - 1, ML_WIDTH), F32), jnp.zeros((bp, h, hd, hd), F32),
                       jnp.zeros((bp, h, hd), F32), jnp.zeros((bp, 1, h), F32), None, sample=False)
        n_pool, page = cache_k.shape[1], cache_k.shape[2]
        paged = (cache_k[l].reshape(n_pool, page * DA_HEADS, DA_V_DIM),
                 cache_v[l].reshape(n_pool, page * DA_HEADS, DA_V_DIM), page_table)
        res_s = _layer(ys, mod_s, p, lam_init, alpha, state_conv[l], state_C[l], state_n[l],
                       state_m[l][:, None, :], paged, sample=True)
        yp, ys = res_p[0], res_s[0]
        for outs, res, nb, ns in ((outs_p, res_p, bp, sp), (outs_s, res_s, bs, ss)):
            outs[0].append(res[1].reshape(nb, ns, DA_HEADS, 2 * DA_HEAD_DIM))
            outs[1].append(res[2].reshape(nb, ns, DA_HEADS, DA_V_DIM))
            outs[2].append(res[3])
            outs[3].append(res[4])
            outs[4].append(res[5].reshape(nb, h))
            outs[5].append(res[6])
    return (yp, ys, *(jnp.stack(o) for o in outs_p), *(jnp.stack(o) for o in outs_s))
```

```python
import functools
import math

import jax
import jax.numpy as jnp
from jax import lax
from jax.experimental import pallas as pl
from jax.experimental.pallas import tpu as pltpu

F32 = jnp.float32
BF16 = jnp.bfloat16
I32 = jnp.int32
HIGHEST = lax.Precision.HIGHEST

LN_EPS = 1e-5
ML_HEADS = 4
ML_HEAD_DIM = 128
ML_WIDTH = ML_HEADS * ML_HEAD_DIM
CONV_W = 4
DA_HEADS = 4
DA_HEAD_DIM = 64
DA_V_DIM = 2 * DA_HEAD_DIM
DA_WIDTH = DA_HEADS * DA_V_DIM
N_GROUPS = 4
EXPERTS_PER_GROUP = 8
N_EXPERTS = N_GROUPS * EXPERTS_PER_GROUP
GATE_LANES = 128
ROUTER_ROWS = 48
SUBLANES = 8
LANES = 128
ROW_TILE = 512
SAMPLE_ROW_TILE = 128
ATTN_TQ = 512
ATTN_TK = 512
EXPERT_COL_BLOCK = 256
MLSTM_SEQS_PER_STEP = 1
MLSTM_CHUNK = 256
PAGE_STREAMS = 1
PAGES_PER_STEP = 16
RUN_SHIFT = 4
RUN_ALIGN = 1 << RUN_SHIFT
SORTED = jnp.bfloat16
TABLE_N = N_EXPERTS
TABLE_LOCAL = 2 * N_EXPERTS
VMEM_LIMIT = 56 * 2 ** 20

_NT = (((1,), (1,)), ((), ()))
LOG2E = 1.4426950408889634


def _params(sem, vmem=None):
    return pltpu.CompilerParams(dimension_semantics=sem, vmem_limit_bytes=vmem)


def _sigmoid(x):
    return jax.nn.sigmoid(x)


def _log_sigmoid(x):
    return jnp.minimum(x, 0.0) - jnp.log1p(jnp.exp(-jnp.abs(x)))


def _ceil_to(x, k):
    return jnp.floor((x + (k - 1)) * (1.0 / k)) * k


def _cumsum_rows(x):
    n = x.shape[0]
    row = lax.broadcasted_iota(I32, x.shape, 0)
    shift = 1
    while shift < n:
        x = x + jnp.where(row >= shift, pltpu.roll(x, shift, axis=0), 0.0)
        shift *= 2
    return x


def _cumsum_lanes(x, n):
    lane = lax.broadcasted_iota(I32, x.shape, 1)
    shift = 1
    while shift < n:
        x = x + jnp.where(lane >= shift, pltpu.roll(x, shift, axis=1), 0.0)
        shift *= 2
    return x


def _layernorm_rows(y, g, b):
    mu = jnp.mean(y, axis=-1, keepdims=True)
    d = y - mu
    var = jnp.mean(d * d, axis=-1, keepdims=True)
    return d * lax.rsqrt(var + LN_EPS) * g + b


def _ada_kernel(c_ref, w_ref, b_ref, o_ref):
    c = c_ref[...]
    s = c * _sigmoid(c)
    o_ref[...] = jnp.dot(s, w_ref[...], preferred_element_type=F32, precision=HIGHEST) + b_ref[...]


def _ada(c, w, b):
    bc, d = c.shape
    n = w.shape[1]
    tn = 512
    return pl.pallas_call(
        _ada_kernel,
        grid=(n // tn,),
        in_specs=[pl.BlockSpec((bc, d), lambda j: (0, 0)),
                  pl.BlockSpec((d, tn), lambda j: (0, j)),
                  pl.BlockSpec((1, tn), lambda j: (0, j))],
        out_specs=pl.BlockSpec((bc, tn), lambda j: (0, j)),
        out_shape=jax.ShapeDtypeStruct((bc, n), F32),
        compiler_params=_params(("arbitrary",)),
    )(c, w, b)


def _proj_kernel(x_ref, sc_ref, sh_ref, wa_ref, wg_ref, wb_ref, bg_ref, cbuf_ref, wconv_ref, bconv_ref, wqk_ref,
                 ca_ref, q_ref, k_ref, v_ref, om_ref, g_ref, qd_ref, kd_ref, vd_ref, kdn_ref, vdn_ref, cnew_ref,
                 ext_ref, *, bb, ts):
    si = pl.program_id(1)
    m = bb * ts
    d = x_ref.shape[-1]
    h = (x_ref[...] * (1.0 + sc_ref[...]) + sh_ref[...]).reshape(m, d).astype(BF16)
    pa = jnp.dot(h, wa_ref[...], preferred_element_type=F32)
    pb = jnp.dot(h, wb_ref[...], preferred_element_type=F32)
    g = jnp.dot(h, wg_ref[...], preferred_element_type=F32) + bg_ref[...]
    g_ref[...] = g.reshape(bb, ts, GATE_LANES)
    w = ML_WIDTH
    v_ref[...] = pa[:, w:2 * w].reshape(bb, ts, w).astype(v_ref.dtype)
    om_ref[...] = pa[:, 2 * w:3 * w].reshape(bb, ts, w)
    qd_ref[...] = pb[:, 0:w].reshape(bb, ts, w).astype(qd_ref.dtype)
    kd = pb[:, w:2 * w]
    vd = pb[:, 2 * w:3 * w]
    kd_ref[...] = kd.reshape(bb, ts, w).astype(kd_ref.dtype)
    vd_ref[...] = vd.reshape(bb, ts, w).astype(vd_ref.dtype)
    for hh in range(DA_HEADS):
        cols = slice(hh * DA_V_DIM, (hh + 1) * DA_V_DIM)
        kdn_ref[:, pl.ds(hh, ts, stride=DA_HEADS), :] = kd[:, cols].reshape(bb, ts, DA_V_DIM)
        vdn_ref[:, pl.ds(hh, ts, stride=DA_HEADS), :] = vd[:, cols].reshape(bb, ts, DA_V_DIM)

    @pl.when(si == 0)
    def _():
        ext_ref[:, 5:8, :] = cbuf_ref[...]

    @pl.when(si > 0)
    def _():
        ext_ref[:, 0:8, :] = ext_ref[:, ts:ts + 8, :]

    ext_ref[:, 8:8 + ts, :] = pa[:, 0:w].reshape(bb, ts, w)
    y = bconv_ref[...]
    for j in range(CONV_W):
        y = y + wconv_ref[j:j + 1, :] * ext_ref[:, 5 + j:5 + j + ts, :]
    ca = y * _sigmoid(y)
    ca_ref[...] = ca
    cnew_ref[...] = ext_ref[:, ts + 5:ts + 8, :]

    ca2 = ca.reshape(m, w)
    hd = ML_HEAD_DIM
    for hh in range(ML_HEADS):
        qk = jnp.dot(ca2[:, hh * hd:(hh + 1) * hd].astype(BF16), wqk_ref[hh], preferred_element_type=F32)
        q_ref[:, :, hh * hd:(hh + 1) * hd] = (qk[:, 0:hd] * (hd ** -0.5)).reshape(bb, ts, hd).astype(q_ref.dtype)
        k_ref[:, :, hh * hd:(hh + 1) * hd] = qk[:, hd:2 * hd].reshape(bb, ts, hd).astype(k_ref.dtype)


def _proj(x, sc, sh, wa, wg, wb, bg, cbuf, wconv, bconv, wqk, *, bb, ts, act):
    b, s, d = x.shape
    w = ML_WIDTH
    grid = (b // bb, s // ts)
    tok = lambda n: pl.BlockSpec((bb, ts, n), lambda i, j: (i, j, 0))
    per_b = lambda r, n: pl.BlockSpec((bb, r, n), lambda i, j: (i, 0, 0))
    const = lambda shape: pl.BlockSpec(shape, lambda i, j: (0,) * len(shape))
    out_shape = (
        jax.ShapeDtypeStruct((b, s, w), F32),
        jax.ShapeDtypeStruct((b, s, w), act),
        jax.ShapeDtypeStruct((b, s, w), act),
        jax.ShapeDtypeStruct((b, s, w), act),
        jax.ShapeDtypeStruct((b, s, w), F32),
        jax.ShapeDtypeStruct((b, s, GATE_LANES), F32),
        jax.ShapeDtypeStruct((b, s, w), act),
        jax.ShapeDtypeStruct((b, s, w), act),
        jax.ShapeDtypeStruct((b, s, w), act),
        jax.ShapeDtypeStruct((b, s * DA_HEADS, DA_V_DIM), F32),
        jax.ShapeDtypeStruct((b, s * DA_HEADS, DA_V_DIM), F32),
        jax.ShapeDtypeStruct((b, CONV_W - 1, w), F32),
    )
    cache_rows = pl.BlockSpec((bb, ts * DA_HEADS, DA_V_DIM), lambda i, j: (i, j, 0))
    out_specs = (tok(w), tok(w), tok(w), tok(w), tok(w), tok(GATE_LANES), tok(w), tok(w), tok(w),
                 cache_rows, cache_rows, per_b(CONV_W - 1, w))
    return pl.pallas_call(
        functools.partial(_proj_kernel, bb=bb, ts=ts),
        grid=grid,
        in_specs=[tok(d), per_b(1, d), per_b(1, d), const(wa.shape), const(wg.shape), const(wb.shape),
                  const(bg.shape), per_b(CONV_W - 1, w), const(wconv.shape), const(bconv.shape),
                  const(wqk.shape)],
        out_specs=out_specs,
        out_shape=out_shape,
        scratch_shapes=[pltpu.VMEM((bb, ts + 8, w), F32)],
        compiler_params=_params(("arbitrary", "arbitrary"), VMEM_LIMIT),
    )(x, sc, sh, wa, wg, wb, bg, cbuf, wconv, bconv, wqk)


def _mlstm_kernel(q_ref, k_ref, v_ref, g_ref, ca_ref, om_ref, c0_ref, n0_ref, m0_ref, gn_ref, skip_ref,
                  hm_ref, c1_ref, n1_ref, m1_ref, c_s, n_s, m_s, *, chunk):
    si = pl.program_id(1)
    ln = chunk
    hd = ML_HEAD_DIM

    nb = q_ref.shape[0]

    @pl.when(si == 0)
    def _():
        c_s[...] = c0_ref[...]
        n_s[...] = n0_ref[...]
        m_s[...] = m0_ref[...]

    for bi in range(nb):
        g = g_ref[bi]
        row = lax.broadcasted_iota(I32, (ln, GATE_LANES), 0)
        bc = _log_sigmoid(g)
        shift = 1
        while shift < ln:
            bc = bc + jnp.where(row >= shift, pltpu.roll(bc, shift, axis=0), 0.0)
            shift *= 2
        g_t = g.T
        bc_t = bc.T
        causal = lax.broadcasted_iota(I32, (ln, ln), 0) >= lax.broadcasted_iota(I32, (ln, ln), 1)

        for hh in range(ML_HEADS):
            cols = slice(hh * hd, (hh + 1) * hd)
            qb = q_ref[bi, :, cols].astype(BF16)
            kf = k_ref[bi, :, cols].astype(F32)
            kb = kf.astype(BF16)
            vb = v_ref[bi, :, cols].astype(BF16)
            b_col = bc[:, ML_HEADS + hh:ML_HEADS + hh + 1]
            i_col = g[:, hh:hh + 1]
            b_row = bc_t[ML_HEADS + hh:ML_HEADS + hh + 1, :]
            i_row = g_t[hh:hh + 1, :]
            m_prev = m_s[bi, :, hh:hh + 1]
            log_d = jnp.where(causal, b_col - b_row + i_row, -jnp.inf)
            inter = b_col + m_prev
            m_t = jnp.maximum(inter, jnp.max(log_d, axis=-1, keepdims=True))
            w_inter = jnp.exp(inter - m_t)
            s = lax.dot_general(qb, kb, _NT, preferred_element_type=F32) * jnp.exp(log_d - m_t)
            c_old = c_s[bi, hh]
            n_old = n_s[bi, hh:hh + 1, :]
            num = (w_inter * jnp.dot(qb, c_old.astype(BF16), preferred_element_type=F32)
                   + jnp.dot(s.astype(BF16), vb, preferred_element_type=F32))
            den = (w_inter * jnp.sum(qb.astype(F32) * n_old, axis=-1, keepdims=True)
                   + jnp.sum(s, axis=-1, keepdims=True))
            hc = num / jnp.maximum(jnp.abs(den), jnp.exp(-m_t))
            m_new = m_t[ln - 1:ln, :]
            b_last = b_col[ln - 1:ln, :]
            w_state = jnp.exp(b_last + m_prev - m_new)
            kw = jnp.exp(b_last - b_col + i_col - m_new) * kf
            c_s[bi, hh] = w_state * c_old + jnp.dot(kw.T.astype(BF16), vb, preferred_element_type=F32)
            n_s[bi, hh:hh + 1, :] = w_state * n_old + jnp.sum(kw, axis=0, keepdims=True)
            m_s[bi, :, hh:hh + 1] = m_new
            mu = jnp.mean(hc, axis=-1, keepdims=True)
            dlt = hc - mu
            var = jnp.mean(dlt * dlt, axis=-1, keepdims=True)
            hn = dlt * lax.rsqrt(var + LN_EPS) * gn_ref[hh:hh + 1, :]
            out = (hn + skip_ref[hh:hh + 1, :] * ca_ref[bi, :, cols]) * _sigmoid(om_ref[bi, :, cols])
            hm_ref[bi, :, cols] = out.astype(hm_ref.dtype)

    @pl.when(si == pl.num_programs(1) - 1)
    def _():
        c1_ref[...] = c_s[...]
        n1_ref[...] = n_s[...]
        m1_ref[...] = m_s[...]


def _mlstm(q, k, v, g, ca, om, c0, n0, m0, gn, skip, *, chunk, act):
    b, s, w = q.shape
    h, hd = ML_HEADS, ML_HEAD_DIM
    nb = MLSTM_SEQS_PER_STEP if b % MLSTM_SEQS_PER_STEP == 0 else 1
    tok = lambda n: pl.BlockSpec((nb, chunk, n), lambda i, j: (i, j, 0))
    c_spec = pl.BlockSpec((nb, h, hd, hd), lambda i, j: (i, 0, 0, 0))
    n_spec = pl.BlockSpec((nb, h, hd), lambda i, j: (i, 0, 0))
    m_spec = pl.BlockSpec((nb, 1, h), lambda i, j: (i, 0, 0))
    hw_spec = pl.BlockSpec((h, hd), lambda i, j: (0, 0))
    return pl.pallas_call(
        functools.partial(_mlstm_kernel, chunk=chunk),
        grid=(b // nb, s // chunk),
        in_specs=[tok(w), tok(w), tok(w), tok(GATE_LANES), tok(w), tok(w), c_spec, n_spec, m_spec,
                  hw_spec, hw_spec],
        out_specs=(tok(w), c_spec, n_spec, m_spec),
        out_shape=(jax.ShapeDtypeStruct((b, s, w), act),
                   jax.ShapeDtypeStruct((b, h, hd, hd), F32),
                   jax.ShapeDtypeStruct((b, h, hd), F32),
                   jax.ShapeDtypeStruct((b, 1, h), F32)),
        scratch_shapes=[pltpu.VMEM((nb, h, hd, hd), F32), pltpu.VMEM((nb, h, hd), F32),
                        pltpu.VMEM((nb, 1, h), F32)],
        compiler_params=_params(("arbitrary", "arbitrary"), VMEM_LIMIT),
    )(q, k, v, g, ca, om, c0, n0, m0, gn, skip)


def _lam(lq1_ref, lk1_ref, lq2_ref, lk2_ref, lam_init):
    a = jnp.sum(lq1_ref[...] * lk1_ref[...], axis=-1, keepdims=True)
    b = jnp.sum(lq2_ref[...] * lk2_ref[...], axis=-1, keepdims=True)
    return jnp.exp(a) - jnp.exp(b) + lam_init


def _head_rms(o, sg, lam_init):
    return o * lax.rsqrt(jnp.mean(o * o, axis=-1, keepdims=True) + LN_EPS) * sg * (1.0 - lam_init)


def _softmax_update(s, vt, m_ref, l_ref, a_ref):
    m_old = m_ref[...]
    m_new = jnp.maximum(m_old, jnp.max(s, axis=-1, keepdims=True))
    alpha = jnp.exp2(m_old - m_new)
    p = jnp.exp2(s - m_new)
    l_ref[...] = alpha * l_ref[...] + jnp.sum(p, axis=-1, keepdims=True)
    a_ref[...] = alpha * a_ref[...] + jnp.dot(p.astype(BF16), vt, preferred_element_type=F32)
    m_ref[...] = m_new


def _dattn_kernel(lq1_ref, lk1_ref, lq2_ref, lk2_ref, sgc_ref, q_ref, k_ref, v_ref, o_ref,
                  vt_s, s00, s01, s10, s11, x00, x01, x10, x11, m1, l1, a1, m2, l2, a2, *, tq, tk, lam_init):
    qi = pl.program_id(2)
    n_chunks = k_ref.shape[1] // tk
    assert tq == tk
    s_s = ((s00, s01), (s10, s11))
    x_s = ((x00, x01), (x10, x11))

    @pl.when(qi == 0)
    def _():
        for c in range(n_chunks):
            vt_s[c] = v_ref[0, c * tk:(c + 1) * tk, :].astype(F32).T.astype(BF16)

    lam = _lam(lq1_ref, lk1_ref, lq2_ref, lk2_ref, lam_init)
    q = q_ref[0].astype(F32) * (DA_HEAD_DIM ** -0.5 * LOG2E)
    lane = lax.broadcasted_iota(I32, q.shape, 1)
    q1 = jnp.where(lane < DA_HEAD_DIM, q, 0.0).astype(BF16)
    q2 = jnp.where(lane >= DA_HEAD_DIM, q, 0.0).astype(BF16)
    for m_ref, l_ref, a_ref in ((m1, l1, a1), (m2, l2, a2)):
        m_ref[...] = jnp.full(m_ref.shape, -jnp.inf, F32)
        l_ref[...] = jnp.zeros(l_ref.shape, F32)
        a_ref[...] = jnp.zeros(a_ref.shape, F32)
    key_minus_query = (lax.broadcasted_iota(I32, (tk, tq), 0) - lax.broadcasted_iota(I32, (tk, tq), 1))
    maps = ((0, q1, m1, l1, a1), (1, q2, m2, l2, a2))

    def score_tile(j, slot, limit):
        kt = k_ref[0, pl.ds(pl.multiple_of(j * tk, tk), tk), :].astype(BF16)
        for mp, qz, _, _, _ in maps:
            st = lax.dot_general(kt, qz, _NT, preferred_element_type=F32)
            if limit is not None:
                st = jnp.where(key_minus_query <= limit, st, -jnp.inf)
            s_s[slot][mp][...] = st
            x_s[slot][mp][...] = jnp.max(st, axis=0, keepdims=True)

    def consume_tile(j, slot):
        vt = vt_s[j]
        for mp, _, m_ref, l_ref, a_ref in maps:
            m_old = m_ref[...]
            m_new = jnp.maximum(m_old, x_s[slot][mp][...])
            alpha = jnp.exp2(m_old - m_new)
            p = jnp.exp2(s_s[slot][mp][...] - m_new)
            l_ref[...] = alpha * l_ref[...] + jnp.sum(p, axis=0, keepdims=True)
            a_ref[...] = alpha * a_ref[...] + jnp.dot(vt, p.astype(BF16), preferred_element_type=F32)
            m_ref[...] = m_new

    def by_parity(j, next_limit, have_next=True):
        for slot in range(2):
            @pl.when((j & 1) == slot)
            def _():
                if have_next:
                    score_tile(j + 1, 1 - slot, next_limit)
                consume_tile(j, slot)

    def body(j, carry):
        by_parity(j, None)
        return carry

    score_tile(0, 0, jnp.where(qi == 0, 0, tk))
    lax.fori_loop(0, qi - 1, body, 0)

    @pl.when(qi > 0)
    def _():
        by_parity(qi - 1, 0)

    by_parity(qi, None, have_next=False)
    ot = a1[...] / l1[...] - lam * (a2[...] / l2[...])
    ot = ot * lax.rsqrt(jnp.mean(ot * ot, axis=0, keepdims=True) + LN_EPS) * sgc_ref[...] * (1.0 - lam_init)
    o_ref[0] = ot.T.astype(o_ref.dtype)


def _dattn_prompt(qd, kd, vd, lams, sgc, *, tq, tk, lam_init, act):
    b, s, w = qd.shape
    dv = DA_V_DIM
    lam_spec = pl.BlockSpec((1, DA_HEAD_DIM), lambda i, h, j: (0, 0))
    stat = pltpu.VMEM((1, tq), F32)
    acc = pltpu.VMEM((dv, tq), F32)
    return pl.pallas_call(
        functools.partial(_dattn_kernel, tq=tq, tk=tk, lam_init=lam_init),
        grid=(b, DA_HEADS, s // tq),
        in_specs=[lam_spec, lam_spec, lam_spec, lam_spec,
                  pl.BlockSpec((dv, 1), lambda i, h, j: (0, 0)),
                  pl.BlockSpec((1, tq, dv), lambda i, h, j: (i, j, h)),
                  pl.BlockSpec((1, s, dv), lambda i, h, j: (i, 0, h)),
                  pl.BlockSpec((1, s, dv), lambda i, h, j: (i, 0, h))],
        out_specs=pl.BlockSpec((1, tq, dv), lambda i, h, j: (i, j, h)),
        out_shape=jax.ShapeDtypeStruct((b, s, w), act),
        scratch_shapes=[pltpu.VMEM((s // tk, dv, tk), BF16),
                        *([pltpu.VMEM((tk, tq), F32)] * 4),
                        stat, stat, stat, stat,
                        stat, stat, acc, stat, stat, acc],
        compiler_params=_params(("arbitrary", "arbitrary", "arbitrary"), VMEM_LIMIT),
    )(*lams, sgc, qd, kd, vd)


def _sattn_kernel(pt_ref, lq1_ref, lk1_ref, lq2_ref, lk2_ref, sg_ref, q_ref, kn_ref, vn_ref, *rest,
                  pps, sd, lam_init):
    del pt_ref
    k_refs = rest[0:pps]
    v_refs = rest[pps:2 * pps]
    o_ref, qx_s, m_s, l_s, a_s = rest[2 * pps:]
    j = pl.program_id(1)
    nq = DA_HEADS * sd
    nrow = 2 * nq
    dv = DA_V_DIM

    @pl.when(j == 0)
    def _():
        q = q_ref[0].astype(F32) * (DA_HEAD_DIM ** -0.5 * LOG2E)
        qh = jnp.concatenate([q[:, hh * dv:(hh + 1) * dv] for hh in range(DA_HEADS)], axis=0)
        lane = lax.broadcasted_iota(I32, qh.shape, 1)
        qx_s[...] = jnp.concatenate([jnp.where(lane < DA_HEAD_DIM, qh, 0.0),
                                     jnp.where(lane >= DA_HEAD_DIM, qh, 0.0)], axis=0).astype(BF16)
        m_s[...] = jnp.full(m_s.shape, -jnp.inf, F32)
        l_s[...] = jnp.zeros(l_s.shape, F32)
        a_s[...] = jnp.zeros(a_s.shape, F32)

    qx = qx_s[...]
    n_keys = k_refs[0].shape[1]
    r = lax.broadcasted_iota(I32, (nrow, n_keys), 0)
    c = lax.broadcasted_iota(I32, (nrow, n_keys), 1)
    bias = jnp.where(((r // sd) % DA_HEADS) == (c % DA_HEADS), 0.0, -jnp.inf)
    for g in range(PAGE_STREAMS):
        group = range(g, pps, PAGE_STREAMS)
        scores = [lax.dot_general(qx, k_refs[p][0].astype(BF16), _NT, preferred_element_type=F32) + bias
                  for p in group]
        m_old = m_s[g]
        m_new = m_old
        for sp in scores:
            m_new = jnp.maximum(m_new, jnp.max(sp, axis=-1, keepdims=True))
        alpha = jnp.exp2(m_old - m_new)
        lsum = alpha * l_s[g]
        acc = alpha * a_s[g]
        for p, sp in zip(group, scores):
            pp = jnp.exp2(sp - m_new)
            lsum = lsum + jnp.sum(pp, axis=-1, keepdims=True)
            acc = acc + jnp.dot(pp.astype(BF16), v_refs[p][0].astype(BF16), preferred_element_type=F32)
        l_s[g] = lsum
        a_s[g] = acc
        m_s[g] = m_new

    @pl.when(j == pl.num_programs(1) - 1)
    def _():
        m_all = m_s[0]
        for g in range(1, PAGE_STREAMS):
            m_all = jnp.maximum(m_all, m_s[g])
        l_all = jnp.zeros_like(l_s[0])
        a_all = jnp.zeros_like(a_s[0])
        for g in range(PAGE_STREAMS):
            w_g = jnp.exp2(m_s[g] - m_all)
            l_all = l_all + w_g * l_s[g]
            a_all = a_all + w_g * a_s[g]
        m_s[0] = m_all
        l_s[0] = l_all
        a_s[0] = a_all
        m_0, l_0, a_0 = m_s.at[0], l_s.at[0], a_s.at[0]
        pad = jnp.zeros((LANES - nq, dv), F32)
        kn = jnp.concatenate([kn_ref[0], pad], axis=0).astype(BF16)
        vn = jnp.concatenate([vn_ref[0], pad], axis=0).astype(BF16)
        sn = lax.dot_general(qx, kn, _NT, preferred_element_type=F32)
        rn = lax.broadcasted_iota(I32, sn.shape, 0)
        cn = lax.broadcasted_iota(I32, sn.shape, 1)
        valid = (cn < nq) & ((cn % DA_HEADS) == ((rn // sd) % DA_HEADS)) & ((cn // DA_HEADS) <= (rn % sd))
        _softmax_update(jnp.where(valid, sn, -jnp.inf), vn, m_0, l_0, a_0)
        lam = _lam(lq1_ref, lk1_ref, lq2_ref, lk2_ref, lam_init)
        o = a_0[0:nq, :] / l_0[0:nq, :] - lam * (a_0[nq:nrow, :] / l_0[nq:nrow, :])
        o = _head_rms(o, sg_ref[...], lam_init)
        for hh in range(DA_HEADS):
            o_ref[0, :, hh * dv:(hh + 1) * dv] = o[hh * sd:(hh + 1) * sd, :].astype(o_ref.dtype)


def _dattn_sample(qd, kn, vn, cache_k, cache_v, page_table, lams, sg, *, pps, lam_init):
    b, sd, w = qd.shape
    n_pages = page_table.shape[1]
    n_keys, dv = cache_k.shape[1], cache_k.shape[2]
    h = DA_HEADS
    lam_spec = pl.BlockSpec((1, DA_HEAD_DIM), lambda i, j, pt: (0, 0))
    tok = pl.BlockSpec((1, sd, w), lambda i, j, pt: (i, 0, 0))
    new_rows = pl.BlockSpec((1, sd * h, dv), lambda i, j, pt: (i, 0, 0))

    nrow = 2 * h * sd

    def page_spec(p):
        return pl.BlockSpec((1, n_keys, dv), lambda i, j, pt: (pt[i, j * pps + p], 0, 0))

    grid_spec = pltpu.PrefetchScalarGridSpec(
        num_scalar_prefetch=1,
        grid=(b, n_pages // pps),
        in_specs=[lam_spec, lam_spec, lam_spec, lam_spec,
                  pl.BlockSpec((1, dv), lambda i, j, pt: (0, 0)),
                  tok, new_rows, new_rows]
                 + [page_spec(p) for p in range(pps)] + [page_spec(p) for p in range(pps)],
        out_specs=tok,
        scratch_shapes=[pltpu.VMEM((nrow, dv), BF16), pltpu.VMEM((PAGE_STREAMS, nrow, 1), F32),
                        pltpu.VMEM((PAGE_STREAMS, nrow, 1), F32), pltpu.VMEM((PAGE_STREAMS, nrow, dv), F32)],
    )
    return pl.pallas_call(
        functools.partial(_sattn_kernel, pps=pps, sd=sd, lam_init=lam_init),
        grid_spec=grid_spec,
        out_shape=jax.ShapeDtypeStruct((b, sd, w), F32),
        compiler_params=_params(("arbitrary", "arbitrary"), VMEM_LIMIT),
    )(page_table, *lams, sg, qd, kn, vn, *([cache_k] * pps), *([cache_v] * pps))


def _mix_kernel(hm_ref, ad_ref, x_ref, ga_ref, scf_ref, shf_ref, wo1_ref, wo2_ref, g1_ref, b1_ref,
                wrh_ref, wrl_ref, brt_ref, x1_ref, h2_ref, ri_ref, rw_ref, cnt_ref, cb_ref, cn_ref,
                carry_s, carry_row_s,
                *, bb, ts, alpha):
    m = bb * ts
    d = x_ref.shape[-1]

    @pl.when((pl.program_id(0) == 0) & (pl.program_id(1) == 0))
    def _():
        carry_s[...] = jnp.zeros(carry_s.shape, F32)
        carry_row_s[...] = jnp.zeros(carry_row_s.shape, F32)

    hm = hm_ref[...].reshape(m, ML_WIDTH).astype(BF16)
    ad = ad_ref[...].reshape(m, DA_WIDTH).astype(BF16)
    mixed = (jnp.dot(hm, wo1_ref[...], preferred_element_type=F32)
             + jnp.dot(ad, wo2_ref[...], preferred_element_type=F32))
    y = alpha * x_ref[...] + (1.0 + ga_ref[...]) * mixed.reshape(bb, ts, d)
    x1 = _layernorm_rows(y, g1_ref[...], b1_ref[...])
    x1_ref[...] = x1
    h2 = (x1 * (1.0 + scf_ref[...]) + shf_ref[...]).reshape(m, d)
    h2_ref[...] = h2

    h2_hi = h2.astype(BF16)
    h2_lo = (h2 - h2_hi.astype(F32)).astype(BF16)
    lt = (lax.dot_general(wrh_ref[...], h2_hi, _NT, preferred_element_type=F32)
          + lax.dot_general(wrl_ref[...], h2_hi, _NT, preferred_element_type=F32)
          + lax.dot_general(wrh_ref[...], h2_lo, _NT, preferred_element_type=F32)) + brt_ref[...]
    gl = lt[0:N_GROUPS]
    gmax = jnp.max(gl, axis=0, keepdims=True)
    r4 = lax.broadcasted_iota(I32, gl.shape, 0)
    gidx = jnp.min(jnp.where(gl == gmax, r4, N_GROUPS), axis=0, keepdims=True)
    gp = 1.0 / jnp.sum(jnp.exp(gl - gmax), axis=0, keepdims=True)
    epg = EXPERTS_PER_GROUP
    esel = lt[8 + (N_GROUPS - 1) * epg:8 + N_GROUPS * epg]
    for grp in range(N_GROUPS - 2, -1, -1):
        esel = jnp.where(gidx == grp, lt[8 + grp * epg:8 + (grp + 1) * epg], esel)
    r8 = lax.broadcasted_iota(I32, esel.shape, 0)
    t1 = jnp.max(esel, axis=0, keepdims=True)
    i1 = jnp.min(jnp.where(esel == t1, r8, epg), axis=0, keepdims=True)
    rest = jnp.where(r8 == i1, -jnp.inf, esel)
    t2 = jnp.max(rest, axis=0, keepdims=True)
    i2 = jnp.min(jnp.where(rest == t2, r8, epg), axis=0, keepdims=True)
    z = jnp.exp(t2 - t1)
    w1 = gp / (1.0 + z)
    w2 = gp * z / (1.0 + z)
    e0 = gidx * epg + i1
    e1 = gidx * epg + i2

    r32 = lax.broadcasted_iota(I32, (N_EXPERTS, m), 0)
    hit0 = r32 == e0
    hit1 = r32 == e1
    onehot = jnp.where(hit0, 1.0, jnp.where(hit1, 1.0, 0.0))
    onehot_b = onehot.astype(BF16)
    before = (lax.broadcasted_iota(I32, (m, m), 0) < lax.broadcasted_iota(I32, (m, m), 1))
    prefix = jnp.dot(onehot_b, jnp.where(before, 1.0, 0.0).astype(BF16), preferred_element_type=F32)
    cnt_col = jnp.sum(onehot, axis=1, keepdims=True)
    run = jnp.broadcast_to(_ceil_to(cnt_col, RUN_ALIGN), (N_EXPERTS, LANES))
    start = _cumsum_rows(run) - run
    slot = prefix + start[:, 0:1]
    slot0 = jnp.sum(jnp.where(hit0, slot, 0.0), axis=0, keepdims=True)
    slot1 = jnp.sum(jnp.where(hit1, slot, 0.0), axis=0, keepdims=True)
    rr = lax.broadcasted_iota(I32, (SUBLANES, m), 0)
    ri_ref[...] = jnp.where(rr == 0, slot0, jnp.where(rr == 1, slot1, 0.0)).astype(I32)
    rw_ref[...] = jnp.where(rr == 0, w1, jnp.where(rr == 1, w2, jnp.where(rr == 2, slot0,
                            jnp.where(rr == 3, slot1, 0.0))))
    padded_hot = jnp.concatenate([onehot_b, jnp.zeros((LANES - N_EXPERTS, m), BF16)], axis=0)
    cnt_row = lax.dot_general(jnp.ones((SUBLANES, m), BF16), padded_hot, _NT, preferred_element_type=F32)
    cb_ref[...] = carry_row_s[...]
    cn_ref[...] = cnt_row
    carry_row_s[...] = carry_row_s[...] + _ceil_to(cnt_row, RUN_ALIGN)
    carry_s[...] = carry_s[...] + run
    cnt_ref[...] = carry_s[...]


def _mix(hm, ad, x, ga, scf, shf, wo1, wo2, g1, b1, wrh, wrl, brt, *, bb, ts, alpha):
    b, s, d = x.shape
    m = bb * ts
    t = b * s
    ns = s // ts
    n_tok_tiles = t // m
    tok = lambda n: pl.BlockSpec((bb, ts, n), lambda i, j: (i, j, 0))
    per_b = pl.BlockSpec((bb, 1, d), lambda i, j: (i, 0, 0))
    const = lambda shape: pl.BlockSpec(shape, lambda i, j: (0,) * len(shape))
    lin = pl.BlockSpec((SUBLANES, m), lambda i, j: (0, i * ns + j))
    per_tile = pl.BlockSpec((SUBLANES, LANES), lambda i, j: (i * ns + j, 0))
    cnt_shape = (N_EXPERTS, LANES)
    return pl.pallas_call(
        functools.partial(_mix_kernel, bb=bb, ts=ts, alpha=alpha),
        grid=(b // bb, ns),
        in_specs=[tok(ML_WIDTH), tok(DA_WIDTH), tok(d), per_b, per_b, per_b, const(wo1.shape), const(wo2.shape),
                  const(g1.shape), const(b1.shape), const(wrh.shape), const(wrl.shape), const(brt.shape)],
        out_specs=(tok(d), pl.BlockSpec((m, d), lambda i, j: (i * ns + j, 0)), lin, lin, const(cnt_shape),
                   per_tile, per_tile),
        out_shape=(jax.ShapeDtypeStruct((b, s, d), F32),
                   jax.ShapeDtypeStruct((t, d), F32),
                   jax.ShapeDtypeStruct((SUBLANES, t), I32),
                   jax.ShapeDtypeStruct((SUBLANES, t), F32),
                   jax.ShapeDtypeStruct(cnt_shape, F32),
                   jax.ShapeDtypeStruct((n_tok_tiles * SUBLANES, LANES), F32),
                   jax.ShapeDtypeStruct((n_tok_tiles * SUBLANES, LANES), F32)),
        scratch_shapes=[pltpu.VMEM(cnt_shape, F32), pltpu.VMEM((SUBLANES, LANES), F32)],
        compiler_params=_params(("arbitrary", "arbitrary"), VMEM_LIMIT),
    )(hm, ad, x, ga, scf, shf, wo1, wo2, g1, b1, wrh, wrl, brt)


def _plan_kernel(cnt_ref, cb_ref, cn_ref, tab_ref, tile_ref, tail_ref, *, row_tile):
    padded = _ceil_to(cnt_ref[...], row_tile)
    ends = _cumsum_rows(padded)
    nt = tile_ref.shape[1]
    first_row = (lax.broadcasted_iota(I32, (N_EXPERTS, nt), 1) * row_tile).astype(F32)
    done = jnp.sum(jnp.where(ends[:, 0:1] <= first_row, 1, 0), axis=0, keepdims=True)
    expert = jnp.minimum(done, N_EXPERTS - 1)
    used = (ends[N_EXPERTS - 1:N_EXPERTS, 0:1] * (1.0 / row_tile)).astype(I32)
    rt = lax.broadcasted_iota(I32, (SUBLANES, nt), 0)
    tile_ref[...] = jnp.where(rt == 0, expert, jnp.where(rt == 1, used, 0))
    n_rows = cb_ref.shape[0]
    cb = cb_ref[...]
    cn = cn_ref[...]
    lane8 = lax.broadcasted_iota(I32, (SUBLANES, LANES), 1)
    total = cb[n_rows - SUBLANES:n_rows, :] + _ceil_to(cn[n_rows - SUBLANES:n_rows, :], RUN_ALIGN)
    live8 = lane8 < N_EXPERTS
    region = jnp.where(live8, _ceil_to(total, row_tile), 0.0)
    offs8 = _cumsum_lanes(region, N_EXPERTS) - region
    offs = offs8[0:1, :]
    tail_ref[...] = (jnp.where(live8, offs8 + total, 0.0)
                     + pltpu.roll(jnp.where(live8, (region - total) * (1.0 / RUN_ALIGN), 0.0), TABLE_N, axis=1)
                     ).astype(I32)
    run = _ceil_to(cn, RUN_ALIGN)
    local = _cumsum_lanes(run, N_EXPERTS) - run
    live = lax.broadcasted_iota(I32, cb.shape, 1) < N_EXPERTS
    tab = (jnp.where(live, cb + offs, 0.0)
           + pltpu.roll(jnp.where(live, cn, 0.0), TABLE_N, axis=1)
           + pltpu.roll(jnp.where(live, local, 0.0), TABLE_LOCAL, axis=1))
    tab_ref[...] = tab.astype(I32)


def _plan(cnt, cb, cn, *, row_tile, n_tiles_pad):
    full = lambda a: pl.BlockSpec(a.shape, lambda i: (0, 0))
    return pl.pallas_call(
        functools.partial(_plan_kernel, row_tile=row_tile),
        grid=(1,),
        in_specs=[full(cnt), full(cb), full(cn)],
        out_specs=(full(cb), pl.BlockSpec((SUBLANES, n_tiles_pad), lambda i: (0, 0)),
                   pl.BlockSpec((SUBLANES, LANES), lambda i: (0, 0))),
        out_shape=(jax.ShapeDtypeStruct(cb.shape, I32), jax.ShapeDtypeStruct((SUBLANES, n_tiles_pad), I32),
                   jax.ShapeDtypeStruct((SUBLANES, LANES), I32)),
        compiler_params=_params(("arbitrary",)),
    )(cnt, cb, cn)


def _local_rows(m):
    return 2 * m + N_EXPERTS * RUN_ALIGN


def _run_chunks(tab_ref, tile, e):
    return (tab_ref[tile * LANES + TABLE_N + e] + (RUN_ALIGN - 1)) >> RUN_SHIFT


def _start_runs(tab_ref, tile, chunk_copy):
    base = tile * LANES

    def per_expert(e, carry):
        first = tab_ref[base + e]
        local = tab_ref[base + TABLE_LOCAL + e]

        def per_chunk(c, carry):
            off = c * RUN_ALIGN
            chunk_copy(pl.multiple_of(local + off, RUN_ALIGN), pl.multiple_of(first + off, RUN_ALIGN)).start()
            return carry

        lax.fori_loop(0, _run_chunks(tab_ref, tile, e), per_chunk, 0)
        return carry

    lax.fori_loop(0, N_EXPERTS, per_expert, 0)


def _wait_runs(tab_ref, tile, chunk_copy):
    total = lax.fori_loop(0, N_EXPERTS, lambda e, acc: acc + _run_chunks(tab_ref, tile, e), 0)

    def wait_one(c, carry):
        chunk_copy(0, 0).wait()
        return carry

    lax.fori_loop(0, total, wait_one, 0)


def _scatter_kernel(tab_ref, tail_ref, nu_ref, ri_ref, h2_ref, xs_ref, xl_s, zero_s, sem, zsem, *, row_tile):
    m = h2_ref.shape[0]
    lc = xl_s.shape[0]
    i = pl.program_id(0)

    @pl.when(i == 0)
    def _():
        zero_s[...] = jnp.zeros(zero_s.shape, SORTED)

        def zero_chunk(row):
            return pltpu.make_async_copy(zero_s, xs_ref.at[pl.ds(pl.multiple_of(row, RUN_ALIGN), RUN_ALIGN)], zsem)

        def span(first, n_chunks):
            def body(c, carry):
                zero_chunk(first + c * RUN_ALIGN).start()
                return carry
            lax.fori_loop(0, n_chunks, body, 0)
            return n_chunks

        def per_expert(e, total):
            return total + span(tail_ref[e], tail_ref[TABLE_N + e])

        total = lax.fori_loop(0, N_EXPERTS, per_expert, 0)
        used_rows = nu_ref[0] * row_tile
        total = total + span(used_rows, (xs_ref.shape[0] - used_rows) >> RUN_SHIFT)

        def wait_one(c, carry):
            zero_chunk(0).wait()
            return carry

        lax.fori_loop(0, total, wait_one, 0)

    k = lax.broadcasted_iota(I32, (lc, m), 0)
    pick = jnp.where(k == ri_ref[0:1, :], 1.0, jnp.where(k == ri_ref[1:2, :], 1.0, 0.0)).astype(BF16)
    xl_s[...] = jnp.dot(pick, h2_ref[...].astype(BF16), preferred_element_type=F32).astype(SORTED)

    def chunk_copy(local_row, sorted_row):
        return pltpu.make_async_copy(xl_s.at[pl.ds(local_row, RUN_ALIGN)], xs_ref.at[pl.ds(sorted_row, RUN_ALIGN)],
                                     sem)

    _start_runs(tab_ref, i, chunk_copy)
    _wait_runs(tab_ref, i, chunk_copy)


def _scatter(tab_flat, tail_flat, n_used, ri, h2, *, m, row_tile, n_tiles):
    t, d = h2.shape
    grid_spec = pltpu.PrefetchScalarGridSpec(
        num_scalar_prefetch=3,
        grid=(t // m,),
        in_specs=[pl.BlockSpec((SUBLANES, m), lambda i, tab, tail, nu: (0, i)),
                  pl.BlockSpec((m, d), lambda i, tab, tail, nu: (i, 0))],
        out_specs=pl.BlockSpec(memory_space=pl.ANY),
        scratch_shapes=[pltpu.VMEM((_local_rows(m), d), SORTED), pltpu.VMEM((RUN_ALIGN, d), SORTED),
                        pltpu.SemaphoreType.DMA(()), pltpu.SemaphoreType.DMA(())],
    )
    return pl.pallas_call(
        functools.partial(_scatter_kernel, row_tile=row_tile),
        grid_spec=grid_spec,
        out_shape=jax.ShapeDtypeStruct((n_tiles * row_tile, d), SORTED),
        compiler_params=_params(("arbitrary",), VMEM_LIMIT),
    )(tab_flat, tail_flat, n_used, ri, h2)


def _experts_kernel(te_ref, nu_ref, x_ref, wg_ref, wu_ref, wd_ref, y_ref, wgb, wub, wdb, act_s):
    i = pl.program_id(0)
    de = wgb.shape[1]

    @pl.when(i < nu_ref[0])
    def _():
        @pl.when((i == 0) | (te_ref[i] != te_ref[jnp.maximum(i - 1, 0)]))
        def _():
            wgb[...] = wg_ref[0].astype(BF16)
            wub[...] = wu_ref[0].astype(BF16)
            wdb[...] = wd_ref[0].astype(BF16)

        x = x_ref[...].astype(BF16)
        for c in range(0, de, EXPERT_COL_BLOCK):
            cols = slice(c, c + EXPERT_COL_BLOCK)
            a = jnp.dot(x, wgb[:, cols], preferred_element_type=F32)
            u = jnp.dot(x, wub[:, cols], preferred_element_type=F32)
            act_s[:, cols] = ((a * _sigmoid(a)) * u).astype(BF16)
        y_ref[...] = jnp.dot(act_s[...], wdb[...], preferred_element_type=F32).astype(y_ref.dtype)

    @pl.when(i >= nu_ref[0])
    def _():
        y_ref[...] = jnp.zeros(y_ref.shape, y_ref.dtype)


def _experts(tile_expert, n_used, xs, w_gate, w_up, w_down, *, row_tile):
    p, d = xs.shape
    de = w_gate.shape[-1]
    row_map = lambda i, te, nu: (jnp.maximum(jnp.minimum(i, nu[0] - 1), 0), 0)
    grid_spec = pltpu.PrefetchScalarGridSpec(
        num_scalar_prefetch=2,
        grid=(p // row_tile,),
        in_specs=[pl.BlockSpec((row_tile, d), row_map),
                  pl.BlockSpec((1, d, de), lambda i, te, nu: (te[i], 0, 0)),
                  pl.BlockSpec((1, d, de), lambda i, te, nu: (te[i], 0, 0)),
                  pl.BlockSpec((1, de, d), lambda i, te, nu: (te[i], 0, 0))],
        out_specs=pl.BlockSpec((row_tile, d), lambda i, te, nu: (i, 0)),
        scratch_shapes=[pltpu.VMEM((d, de), BF16), pltpu.VMEM((d, de), BF16), pltpu.VMEM((de, d), BF16),
                        pltpu.VMEM((row_tile, de), BF16)],
    )
    return pl.pallas_call(
        _experts_kernel,
        grid_spec=grid_spec,
        out_shape=jax.ShapeDtypeStruct((p, d), SORTED),
        compiler_params=_params(("arbitrary",), VMEM_LIMIT),
    )(tile_expert, n_used, xs, w_gate, w_up, w_down)


def _combine_kernel(tab_ref, x1_ref, gf_ref, rw_ref, ys_ref, g2_ref, b2_ref, o_ref, yl_s, sem,
                    *, bb, ts, alpha):
    m = bb * ts
    d = x1_ref.shape[-1]
    lc = yl_s.shape[1]
    tile = pl.program_id(0) * pl.num_programs(1) + pl.program_id(1)
    n_tok_tiles = pl.num_programs(0) * pl.num_programs(1)
    slot = tile & 1

    def chunk_copy(buf):
        def build(local_row, sorted_row):
            return pltpu.make_async_copy(ys_ref.at[pl.ds(sorted_row, RUN_ALIGN)],
                                         yl_s.at[buf, pl.ds(local_row, RUN_ALIGN)], sem.at[buf])
        return build

    @pl.when(tile == 0)
    def _():
        yl_s[...] = jnp.zeros(yl_s.shape, SORTED)
        _start_runs(tab_ref, tile, chunk_copy(0))

    @pl.when(tile + 1 < n_tok_tiles)
    def _():
        _start_runs(tab_ref, tile + 1, chunk_copy(1 - slot))

    _wait_runs(tab_ref, tile, chunk_copy(slot))
    cols = jnp.concatenate([rw_ref[...], jnp.zeros((LANES - SUBLANES, m), F32)], axis=0).T
    kl = lax.broadcasted_iota(I32, (m, lc), 1)
    weights = jnp.where(kl == cols[:, 2:3].astype(I32), cols[:, 0:1],
                        jnp.where(kl == cols[:, 3:4].astype(I32), cols[:, 1:2], 0.0))
    moe = jnp.dot(weights.astype(BF16), yl_s[slot].astype(BF16), preferred_element_type=F32)
    y = alpha * x1_ref[...] + (1.0 + gf_ref[...]) * moe.reshape(bb, ts, d)
    o_ref[...] = _layernorm_rows(y, g2_ref[...], b2_ref[...])


def _combine(tab_flat, x1, gf, rw, ys, g2, b2, *, bb, ts, alpha):
    b, s, d = x1.shape
    m = bb * ts
    ns = s // ts
    grid_spec = pltpu.PrefetchScalarGridSpec(
        num_scalar_prefetch=1,
        grid=(b // bb, ns),
        in_specs=[pl.BlockSpec((bb, ts, d), lambda i, j, tab: (i, j, 0)),
                  pl.BlockSpec((bb, 1, d), lambda i, j, tab: (i, 0, 0)),
                  pl.BlockSpec((SUBLANES, m), lambda i, j, tab: (0, i * ns + j)),
                  pl.BlockSpec(memory_space=pl.ANY),
                  pl.BlockSpec((1, d), lambda i, j, tab: (0, 0)),
                  pl.BlockSpec((1, d), lambda i, j, tab: (0, 0))],
        out_specs=pl.BlockSpec((bb, ts, d), lambda i, j, tab: (i, j, 0)),
        scratch_shapes=[pltpu.VMEM((2, _local_rows(m), d), SORTED), pltpu.SemaphoreType.DMA((2,))],
    )
    return pl.pallas_call(
        functools.partial(_combine_kernel, bb=bb, ts=ts, alpha=alpha),
        grid_spec=grid_spec,
        out_shape=jax.ShapeDtypeStruct((b, s, d), F32),
        compiler_params=_params(("arbitrary", "arbitrary"), VMEM_LIMIT),
    )(tab_flat, x1, gf, rw, ys, g2, b2)


def _layer(x, mod, p, lam_init, alpha, conv_buf, c0, n0, m0, paged, *, sample):
    b, s, d = x.shape
    sh_a, sc_a, g_a, sh_f, sc_f, g_f = mod
    if sample:
        bb, ts, act = b, s, F32
    else:
        bb, ts, act = 1, min(s, 512), BF16
    ca, q, k, v, om, g, qd, kd, vd, kdn, vdn, conv_new = _proj(
        x, sc_a, sh_a, p["wa"], p["wg"], p["wb"], p["bg"], conv_buf, p["w_conv"], p["b_conv"], p["wqk"],
        bb=bb, ts=ts, act=act)

    if sample:
        chunk = LANES
        pad_rows = lambda a: jnp.pad(a, ((0, 0), (0, chunk - s), (0, 0)))
        lane = jnp.arange(GATE_LANES)
        gate_pad = jnp.where(lane < ML_HEADS, -jnp.inf, jnp.where(lane < 2 * ML_HEADS, jnp.inf, 0.0)).astype(F32)
        g_in = jnp.concatenate([g, jnp.broadcast_to(gate_pad, (b, chunk - s, GATE_LANES))], axis=1)
        hm, c1, n1, m1 = _mlstm(pad_rows(q), pad_rows(k), pad_rows(v), g_in, pad_rows(ca), pad_rows(om),
                                c0, n0, m0, p["gn_m"], p["skip_m"], chunk=chunk, act=act)
        hm = hm[:, :s]
        cache_k, cache_v, page_table = paged
        ad = _dattn_sample(qd, kdn, vdn, cache_k, cache_v, page_table, p["lams"], p["subln_g"],
                           pps=min(PAGES_PER_STEP, page_table.shape[1]), lam_init=lam_init)
    else:
        hm, c1, n1, m1 = _mlstm(q, k, v, g, ca, om, c0, n0, m0, p["gn_m"], p["skip_m"],
                                chunk=min(s, MLSTM_CHUNK), act=act)
        ad = _dattn_prompt(qd, kd, vd, p["lams"], p["subln_g"].reshape(DA_V_DIM, 1), tq=min(s, ATTN_TQ),
                           tk=min(s, ATTN_TK), lam_init=lam_init, act=act)

    t = b * s
    x1, h2, ri, rw, cnt, cb, cn = _mix(hm, ad, x, g_a, sc_f, sh_f, p["wo1"], p["wo2"], p["ln1_g"], p["ln1_b"],
                                       p["wrh"], p["wrl"], p["brt"], bb=bb, ts=ts, alpha=alpha)
    row_tile = SAMPLE_ROW_TILE if sample else ROW_TILE
    n_tok_tiles = t // (bb * ts)
    n_tiles = -(-(2 * t + N_EXPERTS * (RUN_ALIGN - 1) * n_tok_tiles) // row_tile) + N_EXPERTS
    n_tiles_pad = -(-n_tiles // LANES) * LANES
    tab, tiles, tail = _plan(cnt, cb, cn, row_tile=row_tile, n_tiles_pad=n_tiles_pad)
    tab_flat = tab[::SUBLANES].reshape(-1)
    n_used = tiles[1, 0:1]
    xs = _scatter(tab_flat, tail[0], n_used, ri, h2, m=bb * ts, row_tile=row_tile, n_tiles=n_tiles)
    ys = _experts(tiles[0, :n_tiles], n_used, xs, p["w_gate"], p["w_up"], p["w_down"], row_tile=row_tile)
    y = _combine(tab_flat, x1, g_f, rw, ys, p["ln2_g"], p["ln2_b"], bb=bb, ts=ts, alpha=alpha)
    return y, kdn, vdn, c1, n1, m1, conv_new


def _layer_params(l, w_in, w_conv, b_conv, w_mq, w_mk, b_i, b_f, gn_m, skip_m, lam_q1, lam_k1, lam_q2, lam_k2,
                  subln_g, w_out, ln1_g, ln1_b, w_rg, b_rg, w_re, b_re, w_gate, w_up, w_down, ln2_g, ln2_b):
    w3 = 3 * ML_WIDTH
    n_gate = 2 * ML_HEADS
    wi = w_in[l]
    d = wi.shape[0]
    wg = jnp.zeros((d, GATE_LANES), F32).at[:, :n_gate].set(wi[:, w3:w3 + n_gate])
    bg = jnp.zeros((1, GATE_LANES), F32).at[0, :ML_HEADS].set(b_i[l]).at[0, ML_HEADS:n_gate].set(b_f[l])
    wrt = jnp.zeros((ROUTER_ROWS, d), F32).at[:N_GROUPS].set(w_rg[l].T).at[8:8 + N_EXPERTS].set(w_re[l].T)
    brt = jnp.zeros((ROUTER_ROWS, 1), F32).at[:N_GROUPS, 0].set(b_rg[l]).at[8:8 + N_EXPERTS, 0].set(b_re[l])
    return {
        "wa": wi[:, :w3].astype(BF16),
        "wg": wg.astype(BF16),
        "wb": wi[:, w3 + n_gate:].astype(BF16),
        "bg": bg,
        "w_conv": w_conv[l],
        "b_conv": b_conv[l][None, :],
        "wqk": jnp.concatenate([w_mq[l], w_mk[l]], axis=-1).astype(BF16),
        "gn_m": gn_m[l], "skip_m": skip_m[l],
        "lams": (lam_q1[l][None, :], lam_k1[l][None, :], lam_q2[l][None, :], lam_k2[l][None, :]),
        "subln_g": subln_g[l][None, :],
        "wo1": w_out[l][:ML_WIDTH].astype(BF16),
        "wo2": w_out[l][ML_WIDTH:].astype(BF16),
        "ln1_g": ln1_g[l][None, :], "ln1_b": ln1_b[l][None, :],
        "wrh": wrt.astype(BF16), "wrl": (wrt - wrt.astype(BF16).astype(F32)).astype(BF16), "brt": brt,
        "w_gate": w_gate[l], "w_up": w_up[l], "w_down": w_down[l],
        "ln2_g": ln2_g[l][None, :], "ln2_b": ln2_b[l][None, :],
    }


def kernel(x_prompt, x_sample, cache_k, cache_v, state_C, state_n, state_m, state_conv, page_table, c_prompt, c_sample, w_ada, b_ada, w_in, w_conv, b_conv, w_mq, w_mk, b_i, b_f, gn_m, skip_m, lam_q1, lam_k1, lam_q2, lam_k2, subln_g, w_out, ln1_g, ln1_b, w_rg, b_rg, w_re, b_re, w_gate, w_up, w_down, ln2_g, ln2_b):
    depth = w_ada.shape[0]
    bp, sp, d = x_prompt.shape
    bs, ss, _ = x_sample.shape
    alpha = (2 * depth) ** 0.25
    yp, ys = x_prompt, x_sample
    outs_p = [[] for _ in range(6)]
    outs_s = [[] for _ in range(6)]
    c_all = jnp.concatenate([c_prompt, c_sample], axis=0)
    for l in range(depth):
        p = _layer_params(l, w_in, w_conv, b_conv, w_mq, w_mk, b_i, b_f, gn_m, skip_m, lam_q1, lam_k1, lam_q2,
                          lam_k2, subln_g, w_out, ln1_g, ln1_b, w_rg, b_rg, w_re, b_re, w_gate, w_up, w_down,
                          ln2_g, ln2_b)
        lam_init = 0.8 - 0.6 * math.exp(-0.3 * l)
        mod = _ada(c_all, w_ada[l], b_ada[l][None, :])
        mod_p = tuple(mod[:bp, None, i * d:(i + 1) * d] for i in range(6))
        mod_s = tuple(mod[bp:, None, i * d:(i + 1) * d] for i in range(6))
        h, hd = ML_HEADS, ML_HEAD_DIM
        res_p = _layer(yp, mod_p, p, lam_init, alpha,
                       jnp.zeros((bp, CONV_W - 1, ML_WIDTH), F32), jnp.zeros((bp, h, hd, hd), F32),
                       jnp.zeros((bp, h, hd), F32), jnp.zeros((bp, 1, h), F32), None, sample=False)
        n_pool, page = cache_k.shape[1], cache_k.shape[2]
        paged = (cache_k[l].reshape(n_pool, page * DA_HEADS, DA_V_DIM),
                 cache_v[l].reshape(n_pool, page * DA_HEADS, DA_V_DIM), page_table)
        res_s = _layer(ys, mod_s, p, lam_init, alpha, state_conv[l], state_C[l], state_n[l],
                       state_m[l][:, None, :], paged, sample=True)
        yp, ys = res_p[0], res_s[0]
        for outs, res, nb, ns in ((outs_p, res_p, bp, sp), (outs_s, res_s, bs, ss)):
            outs[0].append(res[1].reshape(nb, ns, DA_HEADS, 2 * DA_HEAD_DIM))
            outs[1].append(res[2].reshape(nb, ns, DA_HEADS, DA_V_DIM))
            outs[2].append(res[3])
            outs[3].append(res[4])
            outs[4].append(res[5].reshape(nb, h))
            outs[5].append(res[6])
    return (yp, ys, *(jnp.stack(o) for o in outs_p), *(jnp.stack(o) for o in outs_s))
```

```python
import functools
import math

import jax
import jax.numpy as jnp
from jax import lax
from jax.experimental import pallas as pl
from jax.experimental.pallas import tpu as pltpu

F32 = jnp.float32
BF16 = jnp.bfloat16
I32 = jnp.int32
HIGHEST = lax.Precision.HIGHEST

LN_EPS = 1e-5
ML_HEADS = 4
ML_HEAD_DIM = 128
ML_WIDTH = ML_HEADS * ML_HEAD_DIM
CONV_W = 4
DA_HEADS = 4
DA_HEAD_DIM = 64
DA_V_DIM = 2 * DA_HEAD_DIM
DA_WIDTH = DA_HEADS * DA_V_DIM
N_GROUPS = 4
EXPERTS_PER_GROUP = 8
N_EXPERTS = N_GROUPS * EXPERTS_PER_GROUP
GATE_LANES = 128
ROUTER_ROWS = 48
SUBLANES = 8
LANES = 128
ROW_TILE = 512
SAMPLE_ROW_TILE = 128
ATTN_TQ = 512
ATTN_TK = 512
EXPERT_COL_BLOCK = 256
MLSTM_SEQS_PER_STEP = 1
MLSTM_CHUNK = 256
PAGE_STREAMS = 1
PAGES_PER_STEP = 32
RUN_SHIFT = 4
RUN_ALIGN = 1 << RUN_SHIFT
SORTED = jnp.bfloat16
TABLE_N = N_EXPERTS
TABLE_LOCAL = 2 * N_EXPERTS
VMEM_LIMIT = 56 * 2 ** 20

_NT = (((1,), (1,)), ((), ()))
LOG2E = 1.4426950408889634


def _params(sem, vmem=None):
    return pltpu.CompilerParams(dimension_semantics=sem, vmem_limit_bytes=vmem)


def _sigmoid(x):
    return jax.nn.sigmoid(x)


def _log_sigmoid(x):
    return jnp.minimum(x, 0.0) - jnp.log1p(jnp.exp(-jnp.abs(x)))


def _ceil_to(x, k):
    return jnp.floor((x + (k - 1)) * (1.0 / k)) * k


def _cumsum_rows(x):
    n = x.shape[0]
    row = lax.broadcasted_iota(I32, x.shape, 0)
    shift = 1
    while shift < n:
        x = x + jnp.where(row >= shift, pltpu.roll(x, shift, axis=0), 0.0)
        shift *= 2
    return x


def _cumsum_lanes(x, n):
    lane = lax.broadcasted_iota(I32, x.shape, 1)
    shift = 1
    while shift < n:
        x = x + jnp.where(lane >= shift, pltpu.roll(x, shift, axis=1), 0.0)
        shift *= 2
    return x


def _layernorm_rows(y, g, b):
    mu = jnp.mean(y, axis=-1, keepdims=True)
    d = y - mu
    var = jnp.mean(d * d, axis=-1, keepdims=True)
    return d * lax.rsqrt(var + LN_EPS) * g + b


def _ada_kernel(c_ref, w_ref, b_ref, o_ref):
    c = c_ref[...]
    s = c * _sigmoid(c)
    o_ref[...] = jnp.dot(s, w_ref[...], preferred_element_type=F32, precision=HIGHEST) + b_ref[...]


def _ada(c, w, b):
    bc, d = c.shape
    n = w.shape[1]
    tn = 512
    return pl.pallas_call(
        _ada_kernel,
        grid=(n // tn,),
        in_specs=[pl.BlockSpec((bc, d), lambda j: (0, 0)),
                  pl.BlockSpec((d, tn), lambda j: (0, j)),
                  pl.BlockSpec((1, tn), lambda j: (0, j))],
        out_specs=pl.BlockSpec((bc, tn), lambda j: (0, j)),
        out_shape=jax.ShapeDtypeStruct((bc, n), F32),
        compiler_params=_params(("arbitrary",)),
    )(c, w, b)


def _proj_kernel(x_ref, sc_ref, sh_ref, wa_ref, wg_ref, wb_ref, bg_ref, cbuf_ref, wconv_ref, bconv_ref, wqk_ref,
                 ca_ref, q_ref, k_ref, v_ref, om_ref, g_ref, qd_ref, kd_ref, vd_ref, kdn_ref, vdn_ref, cnew_ref,
                 ext_ref, *, bb, ts):
    si = pl.program_id(1)
    m = bb * ts
    d = x_ref.shape[-1]
    h = (x_ref[...] * (1.0 + sc_ref[...]) + sh_ref[...]).reshape(m, d).astype(BF16)
    pa = jnp.dot(h, wa_ref[...], preferred_element_type=F32)
    pb = jnp.dot(h, wb_ref[...], preferred_element_type=F32)
    g = jnp.dot(h, wg_ref[...], preferred_element_type=F32) + bg_ref[...]
    g_ref[...] = g.reshape(bb, ts, GATE_LANES)
    w = ML_WIDTH
    v_ref[...] = pa[:, w:2 * w].reshape(bb, ts, w).astype(v_ref.dtype)
    om_ref[...] = pa[:, 2 * w:3 * w].reshape(bb, ts, w)
    qd_ref[...] = pb[:, 0:w].reshape(bb, ts, w).astype(qd_ref.dtype)
    kd = pb[:, w:2 * w]
    vd = pb[:, 2 * w:3 * w]
    kd_ref[...] = kd.reshape(bb, ts, w).astype(kd_ref.dtype)
    vd_ref[...] = vd.reshape(bb, ts, w).astype(vd_ref.dtype)
    for hh in range(DA_HEADS):
        cols = slice(hh * DA_V_DIM, (hh + 1) * DA_V_DIM)
        kdn_ref[:, pl.ds(hh, ts, stride=DA_HEADS), :] = kd[:, cols].reshape(bb, ts, DA_V_DIM)
        vdn_ref[:, pl.ds(hh, ts, stride=DA_HEADS), :] = vd[:, cols].reshape(bb, ts, DA_V_DIM)

    @pl.when(si == 0)
    def _():
        ext_ref[:, 5:8, :] = cbuf_ref[...]

    @pl.when(si > 0)
    def _():
        ext_ref[:, 0:8, :] = ext_ref[:, ts:ts + 8, :]

    ext_ref[:, 8:8 + ts, :] = pa[:, 0:w].reshape(bb, ts, w)
    y = bconv_ref[...]
    for j in range(CONV_W):
        y = y + wconv_ref[j:j + 1, :] * ext_ref[:, 5 + j:5 + j + ts, :]
    ca = y * _sigmoid(y)
    ca_ref[...] = ca
    cnew_ref[...] = ext_ref[:, ts + 5:ts + 8, :]

    ca2 = ca.reshape(m, w)
    hd = ML_HEAD_DIM
    for hh in range(ML_HEADS):
        qk = jnp.dot(ca2[:, hh * hd:(hh + 1) * hd].astype(BF16), wqk_ref[hh], preferred_element_type=F32)
        q_ref[:, :, hh * hd:(hh + 1) * hd] = (qk[:, 0:hd] * (hd ** -0.5)).reshape(bb, ts, hd).astype(q_ref.dtype)
        k_ref[:, :, hh * hd:(hh + 1) * hd] = qk[:, hd:2 * hd].reshape(bb, ts, hd).astype(k_ref.dtype)


def _proj(x, sc, sh, wa, wg, wb, bg, cbuf, wconv, bconv, wqk, *, bb, ts, act):
    b, s, d = x.shape
    w = ML_WIDTH
    grid = (b // bb, s // ts)
    tok = lambda n: pl.BlockSpec((bb, ts, n), lambda i, j: (i, j, 0))
    per_b = lambda r, n: pl.BlockSpec((bb, r, n), lambda i, j: (i, 0, 0))
    const = lambda shape: pl.BlockSpec(shape, lambda i, j: (0,) * len(shape))
    out_shape = (
        jax.ShapeDtypeStruct((b, s, w), F32),
        jax.ShapeDtypeStruct((b, s, w), act),
        jax.ShapeDtypeStruct((b, s, w), act),
        jax.ShapeDtypeStruct((b, s, w), act),
        jax.ShapeDtypeStruct((b, s, w), F32),
        jax.ShapeDtypeStruct((b, s, GATE_LANES), F32),
        jax.ShapeDtypeStruct((b, s, w), act),
        jax.ShapeDtypeStruct((b, s, w), act),
        jax.ShapeDtypeStruct((b, s, w), act),
        jax.ShapeDtypeStruct((b, s * DA_HEADS, DA_V_DIM), F32),
        jax.ShapeDtypeStruct((b, s * DA_HEADS, DA_V_DIM), F32),
        jax.ShapeDtypeStruct((b, CONV_W - 1, w), F32),
    )
    cache_rows = pl.BlockSpec((bb, ts * DA_HEADS, DA_V_DIM), lambda i, j: (i, j, 0))
    out_specs = (tok(w), tok(w), tok(w), tok(w), tok(w), tok(GATE_LANES), tok(w), tok(w), tok(w),
                 cache_rows, cache_rows, per_b(CONV_W - 1, w))
    return pl.pallas_call(
        functools.partial(_proj_kernel, bb=bb, ts=ts),
        grid=grid,
        in_specs=[tok(d), per_b(1, d), per_b(1, d), const(wa.shape), const(wg.shape), const(wb.shape),
                  const(bg.shape), per_b(CONV_W - 1, w), const(wconv.shape), const(bconv.shape),
                  const(wqk.shape)],
        out_specs=out_specs,
        out_shape=out_shape,
        scratch_shapes=[pltpu.VMEM((bb, ts + 8, w), F32)],
        compiler_params=_params(("arbitrary", "arbitrary"), VMEM_LIMIT),
    )(x, sc, sh, wa, wg, wb, bg, cbuf, wconv, bconv, wqk)


def _mlstm_kernel(q_ref, k_ref, v_ref, g_ref, ca_ref, om_ref, c0_ref, n0_ref, m0_ref, gn_ref, skip_ref,
                  hm_ref, c1_ref, n1_ref, m1_ref, c_s, n_s, m_s, *, chunk):
    si = pl.program_id(1)
    ln = chunk
    hd = ML_HEAD_DIM

    nb = q_ref.shape[0]

    @pl.when(si == 0)
    def _():
        c_s[...] = c0_ref[...]
        n_s[...] = n0_ref[...]
        m_s[...] = m0_ref[...]

    for bi in range(nb):
        g = g_ref[bi]
        row = lax.broadcasted_iota(I32, (ln, GATE_LANES), 0)
        bc = _log_sigmoid(g)
        shift = 1
        while shift < ln:
            bc = bc + jnp.where(row >= shift, pltpu.roll(bc, shift, axis=0), 0.0)
            shift *= 2
        g_t = g.T
        bc_t = bc.T
        causal = lax.broadcasted_iota(I32, (ln, ln), 0) >= lax.broadcasted_iota(I32, (ln, ln), 1)

        for hh in range(ML_HEADS):
            cols = slice(hh * hd, (hh + 1) * hd)
            qb = q_ref[bi, :, cols].astype(BF16)
            kf = k_ref[bi, :, cols].astype(F32)
            kb = kf.astype(BF16)
            vb = v_ref[bi, :, cols].astype(BF16)
            b_col = bc[:, ML_HEADS + hh:ML_HEADS + hh + 1]
            i_col = g[:, hh:hh + 1]
            b_row = bc_t[ML_HEADS + hh:ML_HEADS + hh + 1, :]
            i_row = g_t[hh:hh + 1, :]
            m_prev = m_s[bi, :, hh:hh + 1]
            log_d = jnp.where(causal, b_col - b_row + i_row, -jnp.inf)
            inter = b_col + m_prev
            m_t = jnp.maximum(inter, jnp.max(log_d, axis=-1, keepdims=True))
            w_inter = jnp.exp(inter - m_t)
            s = lax.dot_general(qb, kb, _NT, preferred_element_type=F32) * jnp.exp(log_d - m_t)
            c_old = c_s[bi, hh]
            n_old = n_s[bi, hh:hh + 1, :]
            num = (w_inter * jnp.dot(qb, c_old.astype(BF16), preferred_element_type=F32)
                   + jnp.dot(s.astype(BF16), vb, preferred_element_type=F32))
            den = (w_inter * jnp.sum(qb.astype(F32) * n_old, axis=-1, keepdims=True)
                   + jnp.sum(s, axis=-1, keepdims=True))
            hc = num / jnp.maximum(jnp.abs(den), jnp.exp(-m_t))
            m_new = m_t[ln - 1:ln, :]
            b_last = b_col[ln - 1:ln, :]
            w_state = jnp.exp(b_last + m_prev - m_new)
            kw = jnp.exp(b_last - b_col + i_col - m_new) * kf
            c_s[bi, hh] = w_state * c_old + jnp.dot(kw.T.astype(BF16), vb, preferred_element_type=F32)
            n_s[bi, hh:hh + 1, :] = w_state * n_old + jnp.sum(kw, axis=0, keepdims=True)
            m_s[bi, :, hh:hh + 1] = m_new
            mu = jnp.mean(hc, axis=-1, keepdims=True)
            dlt = hc - mu
            var = jnp.mean(dlt * dlt, axis=-1, keepdims=True)
            hn = dlt * lax.rsqrt(var + LN_EPS) * gn_ref[hh:hh + 1, :]
            out = (hn + skip_ref[hh:hh + 1, :] * ca_ref[bi, :, cols]) * _sigmoid(om_ref[bi, :, cols])
            hm_ref[bi, :, cols] = out.astype(hm_ref.dtype)

    @pl.when(si == pl.num_programs(1) - 1)
    def _():
        c1_ref[...] = c_s[...]
        n1_ref[...] = n_s[...]
        m1_ref[...] = m_s[...]


def _mlstm(q, k, v, g, ca, om, c0, n0, m0, gn, skip, *, chunk, act):
    b, s, w = q.shape
    h, hd = ML_HEADS, ML_HEAD_DIM
    nb = MLSTM_SEQS_PER_STEP if b % MLSTM_SEQS_PER_STEP == 0 else 1
    tok = lambda n: pl.BlockSpec((nb, chunk, n), lambda i, j: (i, j, 0))
    c_spec = pl.BlockSpec((nb, h, hd, hd), lambda i, j: (i, 0, 0, 0))
    n_spec = pl.BlockSpec((nb, h, hd), lambda i, j: (i, 0, 0))
    m_spec = pl.BlockSpec((nb, 1, h), lambda i, j: (i, 0, 0))
    hw_spec = pl.BlockSpec((h, hd), lambda i, j: (0, 0))
    return pl.pallas_call(
        functools.partial(_mlstm_kernel, chunk=chunk),
        grid=(b // nb, s // chunk),
        in_specs=[tok(w), tok(w), tok(w), tok(GATE_LANES), tok(w), tok(w), c_spec, n_spec, m_spec,
                  hw_spec, hw_spec],
        out_specs=(tok(w), c_spec, n_spec, m_spec),
        out_shape=(jax.ShapeDtypeStruct((b, s, w), act),
                   jax.ShapeDtypeStruct((b, h, hd, hd), F32),
                   jax.ShapeDtypeStruct((b, h, hd), F32),
                   jax.ShapeDtypeStruct((b, 1, h), F32)),
        scratch_shapes=[pltpu.VMEM((nb, h, hd, hd), F32), pltpu.VMEM((nb, h, hd), F32),
                        pltpu.VMEM((nb, 1, h), F32)],
        compiler_params=_params(("arbitrary", "arbitrary"), VMEM_LIMIT),
    )(q, k, v, g, ca, om, c0, n0, m0, gn, skip)


def _lam(lq1_ref, lk1_ref, lq2_ref, lk2_ref, lam_init):
    a = jnp.sum(lq1_ref[...] * lk1_ref[...], axis=-1, keepdims=True)
    b = jnp.sum(lq2_ref[...] * lk2_ref[...], axis=-1, keepdims=True)
    return jnp.exp(a) - jnp.exp(b) + lam_init


def _head_rms(o, sg, lam_init):
    return o * lax.rsqrt(jnp.mean(o * o, axis=-1, keepdims=True) + LN_EPS) * sg * (1.0 - lam_init)


def _softmax_update(s, vt, m_ref, l_ref, a_ref):
    m_old = m_ref[...]
    m_new = jnp.maximum(m_old, jnp.max(s, axis=-1, keepdims=True))
    alpha = jnp.exp2(m_old - m_new)
    p = jnp.exp2(s - m_new)
    l_ref[...] = alpha * l_ref[...] + jnp.sum(p, axis=-1, keepdims=True)
    a_ref[...] = alpha * a_ref[...] + jnp.dot(p.astype(BF16), vt, preferred_element_type=F32)
    m_ref[...] = m_new


def _dattn_kernel(lq1_ref, lk1_ref, lq2_ref, lk2_ref, sgc_ref, q_ref, k_ref, v_ref, o_ref,
                  vt_s, s00, s01, s10, s11, x00, x01, x10, x11, m1, l1, a1, m2, l2, a2, *, tq, tk, lam_init):
    qi = pl.program_id(2)
    n_chunks = k_ref.shape[1] // tk
    assert tq == tk
    s_s = ((s00, s01), (s10, s11))
    x_s = ((x00, x01), (x10, x11))

    @pl.when(qi == 0)
    def _():
        for c in range(n_chunks):
            vt_s[c] = v_ref[0, c * tk:(c + 1) * tk, :].astype(F32).T.astype(BF16)

    lam = _lam(lq1_ref, lk1_ref, lq2_ref, lk2_ref, lam_init)
    q = q_ref[0].astype(F32) * (DA_HEAD_DIM ** -0.5 * LOG2E)
    lane = lax.broadcasted_iota(I32, q.shape, 1)
    q1 = jnp.where(lane < DA_HEAD_DIM, q, 0.0).astype(BF16)
    q2 = jnp.where(lane >= DA_HEAD_DIM, q, 0.0).astype(BF16)
    for m_ref, l_ref, a_ref in ((m1, l1, a1), (m2, l2, a2)):
        m_ref[...] = jnp.full(m_ref.shape, -jnp.inf, F32)
        l_ref[...] = jnp.zeros(l_ref.shape, F32)
        a_ref[...] = jnp.zeros(a_ref.shape, F32)
    key_minus_query = (lax.broadcasted_iota(I32, (tk, tq), 0) - lax.broadcasted_iota(I32, (tk, tq), 1))
    maps = ((0, q1, m1, l1, a1), (1, q2, m2, l2, a2))

    def score_tile(j, slot, limit):
        kt = k_ref[0, pl.ds(pl.multiple_of(j * tk, tk), tk), :].astype(BF16)
        for mp, qz, _, _, _ in maps:
            st = lax.dot_general(kt, qz, _NT, preferred_element_type=F32)
            if limit is not None:
                st = jnp.where(key_minus_query <= limit, st, -jnp.inf)
            s_s[slot][mp][...] = st
            x_s[slot][mp][...] = jnp.max(st, axis=0, keepdims=True)

    def consume_tile(j, slot):
        vt = vt_s[j]
        for mp, _, m_ref, l_ref, a_ref in maps:
            m_old = m_ref[...]
            m_new = jnp.maximum(m_old, x_s[slot][mp][...])
            alpha = jnp.exp2(m_old - m_new)
            p = jnp.exp2(s_s[slot][mp][...] - m_new)
            l_ref[...] = alpha * l_ref[...] + jnp.sum(p, axis=0, keepdims=True)
            a_ref[...] = alpha * a_ref[...] + jnp.dot(vt, p.astype(BF16), preferred_element_type=F32)
            m_ref[...] = m_new

    def by_parity(j, next_limit, have_next=True):
        for slot in range(2):
            @pl.when((j & 1) == slot)
            def _():
                if have_next:
                    score_tile(j + 1, 1 - slot, next_limit)
                consume_tile(j, slot)

    def body(j, carry):
        by_parity(j, None)
        return carry

    score_tile(0, 0, jnp.where(qi == 0, 0, tk))
    lax.fori_loop(0, qi - 1, body, 0)

    @pl.when(qi > 0)
    def _():
        by_parity(qi - 1, 0)

    by_parity(qi, None, have_next=False)
    ot = a1[...] / l1[...] - lam * (a2[...] / l2[...])
    ot = ot * lax.rsqrt(jnp.mean(ot * ot, axis=0, keepdims=True) + LN_EPS) * sgc_ref[...] * (1.0 - lam_init)
    o_ref[0] = ot.T.astype(o_ref.dtype)


def _dattn_prompt(qd, kd, vd, lams, sgc, *, tq, tk, lam_init, act):
    b, s, w = qd.shape
    dv = DA_V_DIM
    lam_spec = pl.BlockSpec((1, DA_HEAD_DIM), lambda i, h, j: (0, 0))
    stat = pltpu.VMEM((1, tq), F32)
    acc = pltpu.VMEM((dv, tq), F32)
    return pl.pallas_call(
        functools.partial(_dattn_kernel, tq=tq, tk=tk, lam_init=lam_init),
        grid=(b, DA_HEADS, s // tq),
        in_specs=[lam_spec, lam_spec, lam_spec, lam_spec,
                  pl.BlockSpec((dv, 1), lambda i, h, j: (0, 0)),
                  pl.BlockSpec((1, tq, dv), lambda i, h, j: (i, j, h)),
                  pl.BlockSpec((1, s, dv), lambda i, h, j: (i, 0, h)),
                  pl.BlockSpec((1, s, dv), lambda i, h, j: (i, 0, h))],
        out_specs=pl.BlockSpec((1, tq, dv), lambda i, h, j: (i, j, h)),
        out_shape=jax.ShapeDtypeStruct((b, s, w), act),
        scratch_shapes=[pltpu.VMEM((s // tk, dv, tk), BF16),
                        *([pltpu.VMEM((tk, tq), F32)] * 4),
                        stat, stat, stat, stat,
                        stat, stat, acc, stat, stat, acc],
        compiler_params=_params(("arbitrary", "arbitrary", "arbitrary"), VMEM_LIMIT),
    )(*lams, sgc, qd, kd, vd)


def _sattn_kernel(pt_ref, lq1_ref, lk1_ref, lq2_ref, lk2_ref, sg_ref, q_ref, kn_ref, vn_ref, *rest,
                  pps, sd, lam_init):
    del pt_ref
    k_refs = rest[0:pps]
    v_refs = rest[pps:2 * pps]
    o_ref, qx_s, m_s, l_s, a_s = rest[2 * pps:]
    j = pl.program_id(1)
    nq = DA_HEADS * sd
    nrow = 2 * nq
    dv = DA_V_DIM

    @pl.when(j == 0)
    def _():
        q = q_ref[0].astype(F32) * (DA_HEAD_DIM ** -0.5 * LOG2E)
        qh = jnp.concatenate([q[:, hh * dv:(hh + 1) * dv] for hh in range(DA_HEADS)], axis=0)
        lane = lax.broadcasted_iota(I32, qh.shape, 1)
        qx_s[...] = jnp.concatenate([jnp.where(lane < DA_HEAD_DIM, qh, 0.0),
                                     jnp.where(lane >= DA_HEAD_DIM, qh, 0.0)], axis=0).astype(BF16)
        m_s[...] = jnp.full(m_s.shape, -jnp.inf, F32)
        l_s[...] = jnp.zeros(l_s.shape, F32)
        a_s[...] = jnp.zeros(a_s.shape, F32)

    qx = qx_s[...]
    n_keys = k_refs[0].shape[1]
    r = lax.broadcasted_iota(I32, (nrow, n_keys), 0)
    c = lax.broadcasted_iota(I32, (nrow, n_keys), 1)
    bias = jnp.where(((r // sd) % DA_HEADS) == (c % DA_HEADS), 0.0, -jnp.inf)
    for g in range(PAGE_STREAMS):
        group = range(g, pps, PAGE_STREAMS)
        scores = [lax.dot_general(qx, k_refs[p][0].astype(BF16), _NT, preferred_element_type=F32) + bias
                  for p in group]
        m_old = m_s[g]
        m_new = m_old
        for sp in scores:
            m_new = jnp.maximum(m_new, jnp.max(sp, axis=-1, keepdims=True))
        alpha = jnp.exp2(m_old - m_new)
        lsum = alpha * l_s[g]
        acc = alpha * a_s[g]
        for p, sp in zip(group, scores):
            pp = jnp.exp2(sp - m_new)
            lsum = lsum + jnp.sum(pp, axis=-1, keepdims=True)
            acc = acc + jnp.dot(pp.astype(BF16), v_refs[p][0].astype(BF16), preferred_element_type=F32)
        l_s[g] = lsum
        a_s[g] = acc
        m_s[g] = m_new

    @pl.when(j == pl.num_programs(1) - 1)
    def _():
        m_all = m_s[0]
        for g in range(1, PAGE_STREAMS):
            m_all = jnp.maximum(m_all, m_s[g])
        l_all = jnp.zeros_like(l_s[0])
        a_all = jnp.zeros_like(a_s[0])
        for g in range(PAGE_STREAMS):
            w_g = jnp.exp2(m_s[g] - m_all)
            l_all = l_all + w_g * l_s[g]
            a_all = a_all + w_g * a_s[g]
        m_s[0] = m_all
        l_s[0] = l_all
        a_s[0] = a_all
        m_0, l_0, a_0 = m_s.at[0], l_s.at[0], a_s.at[0]
        pad = jnp.zeros((LANES - nq, dv), F32)
        kn = jnp.concatenate([kn_ref[0], pad], axis=0).astype(BF16)
        vn = jnp.concatenate([vn_ref[0], pad], axis=0).astype(BF16)
        sn = lax.dot_general(qx, kn, _NT, preferred_element_type=F32)
        rn = lax.broadcasted_iota(I32, sn.shape, 0)
        cn = lax.broadcasted_iota(I32, sn.shape, 1)
        valid = (cn < nq) & ((cn % DA_HEADS) == ((rn // sd) % DA_HEADS)) & ((cn // DA_HEADS) <= (rn % sd))
        _softmax_update(jnp.where(valid, sn, -jnp.inf), vn, m_0, l_0, a_0)
        lam = _lam(lq1_ref, lk1_ref, lq2_ref, lk2_ref, lam_init)
        o = a_0[0:nq, :] / l_0[0:nq, :] - lam * (a_0[nq:nrow, :] / l_0[nq:nrow, :])
        o = _head_rms(o, sg_ref[...], lam_init)
        for hh in range(DA_HEADS):
            o_ref[0, :, hh * dv:(hh + 1) * dv] = o[hh * sd:(hh + 1) * sd, :].astype(o_ref.dtype)


def _dattn_sample(qd, kn, vn, cache_k, cache_v, page_table, lams, sg, *, pps, lam_init):
    b, sd, w = qd.shape
    n_pages = page_table.shape[1]
    n_keys, dv = cache_k.shape[1], cache_k.shape[2]
    h = DA_HEADS
    lam_spec = pl.BlockSpec((1, DA_HEAD_DIM), lambda i, j, pt: (0, 0))
    tok = pl.BlockSpec((1, sd, w), lambda i, j, pt: (i, 0, 0))
    new_rows = pl.BlockSpec((1, sd * h, dv), lambda i, j, pt: (i, 0, 0))

    nrow = 2 * h * sd

    def page_spec(p):
        return pl.BlockSpec((1, n_keys, dv), lambda i, j, pt: (pt[i, j * pps + p], 0, 0))

    grid_spec = pltpu.PrefetchScalarGridSpec(
        num_scalar_prefetch=1,
        grid=(b, n_pages // pps),
        in_specs=[lam_spec, lam_spec, lam_spec, lam_spec,
                  pl.BlockSpec((1, dv), lambda i, j, pt: (0, 0)),
                  tok, new_rows, new_rows]
                 + [page_spec(p) for p in range(pps)] + [page_spec(p) for p in range(pps)],
        out_specs=tok,
        scratch_shapes=[pltpu.VMEM((nrow, dv), BF16), pltpu.VMEM((PAGE_STREAMS, nrow, 1), F32),
                        pltpu.VMEM((PAGE_STREAMS, nrow, 1), F32), pltpu.VMEM((PAGE_STREAMS, nrow, dv), F32)],
    )
    return pl.pallas_call(
        functools.partial(_sattn_kernel, pps=pps, sd=sd, lam_init=lam_init),
        grid_spec=grid_spec,
        out_shape=jax.ShapeDtypeStruct((b, sd, w), F32),
        compiler_params=_params(("arbitrary", "arbitrary"), VMEM_LIMIT),
    )(page_table, *lams, sg, qd, kn, vn, *([cache_k] * pps), *([cache_v] * pps))


def _mix_kernel(hm_ref, ad_ref, x_ref, ga_ref, scf_ref, shf_ref, wo1_ref, wo2_ref, g1_ref, b1_ref,
                wrh_ref, wrl_ref, brt_ref, x1_ref, h2_ref, ri_ref, rw_ref, cnt_ref, cb_ref, cn_ref,
                carry_s, carry_row_s,
                *, bb, ts, alpha):
    m = bb * ts
    d = x_ref.shape[-1]

    @pl.when((pl.program_id(0) == 0) & (pl.program_id(1) == 0))
    def _():
        carry_s[...] = jnp.zeros(carry_s.shape, F32)
        carry_row_s[...] = jnp.zeros(carry_row_s.shape, F32)

    hm = hm_ref[...].reshape(m, ML_WIDTH).astype(BF16)
    ad = ad_ref[...].reshape(m, DA_WIDTH).astype(BF16)
    mixed = (jnp.dot(hm, wo1_ref[...], preferred_element_type=F32)
             + jnp.dot(ad, wo2_ref[...], preferred_element_type=F32))
    y = alpha * x_ref[...] + (1.0 + ga_ref[...]) * mixed.reshape(bb, ts, d)
    x1 = _layernorm_rows(y, g1_ref[...], b1_ref[...])
    x1_ref[...] = x1
    h2 = (x1 * (1.0 + scf_ref[...]) + shf_ref[...]).reshape(m, d)
    h2_ref[...] = h2

    h2_hi = h2.astype(BF16)
    h2_lo = (h2 - h2_hi.astype(F32)).astype(BF16)
    lt = (lax.dot_general(wrh_ref[...], h2_hi, _NT, preferred_element_type=F32)
          + lax.dot_general(wrl_ref[...], h2_hi, _NT, preferred_element_type=F32)
          + lax.dot_general(wrh_ref[...], h2_lo, _NT, preferred_element_type=F32)) + brt_ref[...]
    gl = lt[0:N_GROUPS]
    gmax = jnp.max(gl, axis=0, keepdims=True)
    r4 = lax.broadcasted_iota(I32, gl.shape, 0)
    gidx = jnp.min(jnp.where(gl == gmax, r4, N_GROUPS), axis=0, keepdims=True)
    gp = 1.0 / jnp.sum(jnp.exp(gl - gmax), axis=0, keepdims=True)
    epg = EXPERTS_PER_GROUP
    esel = lt[8 + (N_GROUPS - 1) * epg:8 + N_GROUPS * epg]
    for grp in range(N_GROUPS - 2, -1, -1):
        esel = jnp.where(gidx == grp, lt[8 + grp * epg:8 + (grp + 1) * epg], esel)
    r8 = lax.broadcasted_iota(I32, esel.shape, 0)
    t1 = jnp.max(esel, axis=0, keepdims=True)
    i1 = jnp.min(jnp.where(esel == t1, r8, epg), axis=0, keepdims=True)
    rest = jnp.where(r8 == i1, -jnp.inf, esel)
    t2 = jnp.max(rest, axis=0, keepdims=True)
    i2 = jnp.min(jnp.where(rest == t2, r8, epg), axis=0, keepdims=True)
    z = jnp.exp(t2 - t1)
    w1 = gp / (1.0 + z)
    w2 = gp * z / (1.0 + z)
    e0 = gidx * epg + i1
    e1 = gidx * epg + i2

    r32 = lax.broadcasted_iota(I32, (N_EXPERTS, m), 0)
    hit0 = r32 == e0
    hit1 = r32 == e1
    onehot = jnp.where(hit0, 1.0, jnp.where(hit1, 1.0, 0.0))
    onehot_b = onehot.astype(BF16)
    before = (lax.broadcasted_iota(I32, (m, m), 0) < lax.broadcasted_iota(I32, (m, m), 1))
    prefix = jnp.dot(onehot_b, jnp.where(before, 1.0, 0.0).astype(BF16), preferred_element_type=F32)
    cnt_col = jnp.sum(onehot, axis=1, keepdims=True)
    run = jnp.broadcast_to(_ceil_to(cnt_col, RUN_ALIGN), (N_EXPERTS, LANES))
    start = _cumsum_rows(run) - run
    slot = prefix + start[:, 0:1]
    slot0 = jnp.sum(jnp.where(hit0, slot, 0.0), axis=0, keepdims=True)
    slot1 = jnp.sum(jnp.where(hit1, slot, 0.0), axis=0, keepdims=True)
    rr = lax.broadcasted_iota(I32, (SUBLANES, m), 0)
    ri_ref[...] = jnp.where(rr == 0, slot0, jnp.where(rr == 1, slot1, 0.0)).astype(I32)
    rw_ref[...] = jnp.where(rr == 0, w1, jnp.where(rr == 1, w2, jnp.where(rr == 2, slot0,
                            jnp.where(rr == 3, slot1, 0.0))))
    padded_hot = jnp.concatenate([onehot_b, jnp.zeros((LANES - N_EXPERTS, m), BF16)], axis=0)
    cnt_row = lax.dot_general(jnp.ones((SUBLANES, m), BF16), padded_hot, _NT, preferred_element_type=F32)
    cb_ref[...] = carry_row_s[...]
    cn_ref[...] = cnt_row
    carry_row_s[...] = carry_row_s[...] + _ceil_to(cnt_row, RUN_ALIGN)
    carry_s[...] = carry_s[...] + run
    cnt_ref[...] = carry_s[...]


def _mix(hm, ad, x, ga, scf, shf, wo1, wo2, g1, b1, wrh, wrl, brt, *, bb, ts, alpha):
    b, s, d = x.shape
    m = bb * ts
    t = b * s
    ns = s // ts
    n_tok_tiles = t // m
    tok = lambda n: pl.BlockSpec((bb, ts, n), lambda i, j: (i, j, 0))
    per_b = pl.BlockSpec((bb, 1, d), lambda i, j: (i, 0, 0))
    const = lambda shape: pl.BlockSpec(shape, lambda i, j: (0,) * len(shape))
    lin = pl.BlockSpec((SUBLANES, m), lambda i, j: (0, i * ns + j))
    per_tile = pl.BlockSpec((SUBLANES, LANES), lambda i, j: (i * ns + j, 0))
    cnt_shape = (N_EXPERTS, LANES)
    return pl.pallas_call(
        functools.partial(_mix_kernel, bb=bb, ts=ts, alpha=alpha),
        grid=(b // bb, ns),
        in_specs=[tok(ML_WIDTH), tok(DA_WIDTH), tok(d), per_b, per_b, per_b, const(wo1.shape), const(wo2.shape),
                  const(g1.shape), const(b1.shape), const(wrh.shape), const(wrl.shape), const(brt.shape)],
        out_specs=(tok(d), pl.BlockSpec((m, d), lambda i, j: (i * ns + j, 0)), lin, lin, const(cnt_shape),
                   per_tile, per_tile),
        out_shape=(jax.ShapeDtypeStruct((b, s, d), F32),
                   jax.ShapeDtypeStruct((t, d), F32),
                   jax.ShapeDtypeStruct((SUBLANES, t), I32),
                   jax.ShapeDtypeStruct((SUBLANES, t), F32),
                   jax.ShapeDtypeStruct(cnt_shape, F32),
                   jax.ShapeDtypeStruct((n_tok_tiles * SUBLANES, LANES), F32),
                   jax.ShapeDtypeStruct((n_tok_tiles * SUBLANES, LANES), F32)),
        scratch_shapes=[pltpu.VMEM(cnt_shape, F32), pltpu.VMEM((SUBLANES, LANES), F32)],
        compiler_params=_params(("arbitrary", "arbitrary"), VMEM_LIMIT),
    )(hm, ad, x, ga, scf, shf, wo1, wo2, g1, b1, wrh, wrl, brt)


def _plan_kernel(cnt_ref, cb_ref, cn_ref, tab_ref, tile_ref, tail_ref, *, row_tile):
    padded = _ceil_to(cnt_ref[...], row_tile)
    ends = _cumsum_rows(padded)
    nt = tile_ref.shape[1]
    first_row = (lax.broadcasted_iota(I32, (N_EXPERTS, nt), 1) * row_tile).astype(F32)
    done = jnp.sum(jnp.where(ends[:, 0:1] <= first_row, 1, 0), axis=0, keepdims=True)
    expert = jnp.minimum(done, N_EXPERTS - 1)
    used = (ends[N_EXPERTS - 1:N_EXPERTS, 0:1] * (1.0 / row_tile)).astype(I32)
    rt = lax.broadcasted_iota(I32, (SUBLANES, nt), 0)
    tile_ref[...] = jnp.where(rt == 0, expert, jnp.where(rt == 1, used, 0))
    n_rows = cb_ref.shape[0]
    cb = cb_ref[...]
    cn = cn_ref[...]
    lane8 = lax.broadcasted_iota(I32, (SUBLANES, LANES), 1)
    total = cb[n_rows - SUBLANES:n_rows, :] + _ceil_to(cn[n_rows - SUBLANES:n_rows, :], RUN_ALIGN)
    live8 = lane8 < N_EXPERTS
    region = jnp.where(live8, _ceil_to(total, row_tile), 0.0)
    offs8 = _cumsum_lanes(region, N_EXPERTS) - region
    offs = offs8[0:1, :]
    tail_ref[...] = (jnp.where(live8, offs8 + total, 0.0)
                     + pltpu.roll(jnp.where(live8, (region - total) * (1.0 / RUN_ALIGN), 0.0), TABLE_N, axis=1)
                     ).astype(I32)
    run = _ceil_to(cn, RUN_ALIGN)
    local = _cumsum_lanes(run, N_EXPERTS) - run
    live = lax.broadcasted_iota(I32, cb.shape, 1) < N_EXPERTS
    tab = (jnp.where(live, cb + offs, 0.0)
           + pltpu.roll(jnp.where(live, cn, 0.0), TABLE_N, axis=1)
           + pltpu.roll(jnp.where(live, local, 0.0), TABLE_LOCAL, axis=1))
    tab_ref[...] = tab.astype(I32)


def _plan(cnt, cb, cn, *, row_tile, n_tiles_pad):
    full = lambda a: pl.BlockSpec(a.shape, lambda i: (0, 0))
    return pl.pallas_call(
        functools.partial(_plan_kernel, row_tile=row_tile),
        grid=(1,),
        in_specs=[full(cnt), full(cb), full(cn)],
        out_specs=(full(cb), pl.BlockSpec((SUBLANES, n_tiles_pad), lambda i: (0, 0)),
                   pl.BlockSpec((SUBLANES, LANES), lambda i: (0, 0))),
        out_shape=(jax.ShapeDtypeStruct(cb.shape, I32), jax.ShapeDtypeStruct((SUBLANES, n_tiles_pad), I32),
                   jax.ShapeDtypeStruct((SUBLANES, LANES), I32)),
        compiler_params=_params(("arbitrary",)),
    )(cnt, cb, cn)


def _local_rows(m):
    return 2 * m + N_EXPERTS * RUN_ALIGN


def _run_chunks(tab_ref, tile, e):
    return (tab_ref[tile * LANES + TABLE_N + e] + (RUN_ALIGN - 1)) >> RUN_SHIFT


def _start_runs(tab_ref, tile, chunk_copy):
    base = tile * LANES

    def per_expert(e, carry):
        first = tab_ref[base + e]
        local = tab_ref[base + TABLE_LOCAL + e]

        def per_chunk(c, carry):
            off = c * RUN_ALIGN
            chunk_copy(pl.multiple_of(local + off, RUN_ALIGN), pl.multiple_of(first + off, RUN_ALIGN)).start()
            return carry

        lax.fori_loop(0, _run_chunks(tab_ref, tile, e), per_chunk, 0)
        return carry

    lax.fori_loop(0, N_EXPERTS, per_expert, 0)


def _wait_runs(tab_ref, tile, chunk_copy):
    total = lax.fori_loop(0, N_EXPERTS, lambda e, acc: acc + _run_chunks(tab_ref, tile, e), 0)

    def wait_one(c, carry):
        chunk_copy(0, 0).wait()
        return carry

    lax.fori_loop(0, total, wait_one, 0)


def _scatter_kernel(tab_ref, tail_ref, nu_ref, ri_ref, h2_ref, xs_ref, xl_s, zero_s, sem, zsem, *, row_tile):
    m = h2_ref.shape[0]
    lc = xl_s.shape[0]
    i = pl.program_id(0)

    @pl.when(i == 0)
    def _():
        zero_s[...] = jnp.zeros(zero_s.shape, SORTED)

        def zero_chunk(row):
            return pltpu.make_async_copy(zero_s, xs_ref.at[pl.ds(pl.multiple_of(row, RUN_ALIGN), RUN_ALIGN)], zsem)

        def span(first, n_chunks):
            def body(c, carry):
                zero_chunk(first + c * RUN_ALIGN).start()
                return carry
            lax.fori_loop(0, n_chunks, body, 0)
            return n_chunks

        def per_expert(e, total):
            return total + span(tail_ref[e], tail_ref[TABLE_N + e])

        total = lax.fori_loop(0, N_EXPERTS, per_expert, 0)
        used_rows = nu_ref[0] * row_tile
        total = total + span(used_rows, (xs_ref.shape[0] - used_rows) >> RUN_SHIFT)

        def wait_one(c, carry):
            zero_chunk(0).wait()
            return carry

        lax.fori_loop(0, total, wait_one, 0)

    k = lax.broadcasted_iota(I32, (lc, m), 0)
    pick = jnp.where(k == ri_ref[0:1, :], 1.0, jnp.where(k == ri_ref[1:2, :], 1.0, 0.0)).astype(BF16)
    xl_s[...] = jnp.dot(pick, h2_ref[...].astype(BF16), preferred_element_type=F32).astype(SORTED)

    def chunk_copy(local_row, sorted_row):
        return pltpu.make_async_copy(xl_s.at[pl.ds(local_row, RUN_ALIGN)], xs_ref.at[pl.ds(sorted_row, RUN_ALIGN)],
                                     sem)

    _start_runs(tab_ref, i, chunk_copy)
    _wait_runs(tab_ref, i, chunk_copy)


def _scatter(tab_flat, tail_flat, n_used, ri, h2, *, m, row_tile, n_tiles):
    t, d = h2.shape
    grid_spec = pltpu.PrefetchScalarGridSpec(
        num_scalar_prefetch=3,
        grid=(t // m,),
        in_specs=[pl.BlockSpec((SUBLANES, m), lambda i, tab, tail, nu: (0, i)),
                  pl.BlockSpec((m, d), lambda i, tab, tail, nu: (i, 0))],
        out_specs=pl.BlockSpec(memory_space=pl.ANY),
        scratch_shapes=[pltpu.VMEM((_local_rows(m), d), SORTED), pltpu.VMEM((RUN_ALIGN, d), SORTED),
                        pltpu.SemaphoreType.DMA(()), pltpu.SemaphoreType.DMA(())],
    )
    return pl.pallas_call(
        functools.partial(_scatter_kernel, row_tile=row_tile),
        grid_spec=grid_spec,
        out_shape=jax.ShapeDtypeStruct((n_tiles * row_tile, d), SORTED),
        compiler_params=_params(("arbitrary",), VMEM_LIMIT),
    )(tab_flat, tail_flat, n_used, ri, h2)


def _experts_kernel(te_ref, nu_ref, x_ref, wg_ref, wu_ref, wd_ref, y_ref, wgb, wub, wdb, act_s):
    i = pl.program_id(0)
    de = wgb.shape[1]

    @pl.when(i < nu_ref[0])
    def _():
        @pl.when((i == 0) | (te_ref[i] != te_ref[jnp.maximum(i - 1, 0)]))
        def _():
            wgb[...] = wg_ref[0].astype(BF16)
            wub[...] = wu_ref[0].astype(BF16)
            wdb[...] = wd_ref[0].astype(BF16)

        x = x_ref[...].astype(BF16)
        for c in range(0, de, EXPERT_COL_BLOCK):
            cols = slice(c, c + EXPERT_COL_BLOCK)
            a = jnp.dot(x, wgb[:, cols], preferred_element_type=F32)
            u = jnp.dot(x, wub[:, cols], preferred_element_type=F32)
            act_s[:, cols] = ((a * _sigmoid(a)) * u).astype(BF16)
        y_ref[...] = jnp.dot(act_s[...], wdb[...], preferred_element_type=F32).astype(y_ref.dtype)

    @pl.when(i >= nu_ref[0])
    def _():
        y_ref[...] = jnp.zeros(y_ref.shape, y_ref.dtype)


def _experts(tile_expert, n_used, xs, w_gate, w_up, w_down, *, row_tile):
    p, d = xs.shape
    de = w_gate.shape[-1]
    row_map = lambda i, te, nu: (jnp.maximum(jnp.minimum(i, nu[0] - 1), 0), 0)
    grid_spec = pltpu.PrefetchScalarGridSpec(
        num_scalar_prefetch=2,
        grid=(p // row_tile,),
        in_specs=[pl.BlockSpec((row_tile, d), row_map),
                  pl.BlockSpec((1, d, de), lambda i, te, nu: (te[i], 0, 0)),
                  pl.BlockSpec((1, d, de), lambda i, te, nu: (te[i], 0, 0)),
                  pl.BlockSpec((1, de, d), lambda i, te, nu: (te[i], 0, 0))],
        out_specs=pl.BlockSpec((row_tile, d), lambda i, te, nu: (i, 0)),
        scratch_shapes=[pltpu.VMEM((d, de), BF16), pltpu.VMEM((d, de), BF16), pltpu.VMEM((de, d), BF16),
                        pltpu.VMEM((row_tile, de), BF16)],
    )
    return pl.pallas_call(
        _experts_kernel,
        grid_spec=grid_spec,
        out_shape=jax.ShapeDtypeStruct((p, d), SORTED),
        compiler_params=_params(("arbitrary",), VMEM_LIMIT),
    )(tile_expert, n_used, xs, w_gate, w_up, w_down)


def _combine_kernel(tab_ref, x1_ref, gf_ref, rw_ref, ys_ref, g2_ref, b2_ref, o_ref, yl_s, sem,
                    *, bb, ts, alpha):
    m = bb * ts
    d = x1_ref.shape[-1]
    lc = yl_s.shape[1]
    tile = pl.program_id(0) * pl.num_programs(1) + pl.program_id(1)
    n_tok_tiles = pl.num_programs(0) * pl.num_programs(1)
    slot = tile & 1

    def chunk_copy(buf):
        def build(local_row, sorted_row):
            return pltpu.make_async_copy(ys_ref.at[pl.ds(sorted_row, RUN_ALIGN)],
                                         yl_s.at[buf, pl.ds(local_row, RUN_ALIGN)], sem.at[buf])
        return build

    @pl.when(tile == 0)
    def _():
        yl_s[...] = jnp.zeros(yl_s.shape, SORTED)
        _start_runs(tab_ref, tile, chunk_copy(0))

    @pl.when(tile + 1 < n_tok_tiles)
    def _():
        _start_runs(tab_ref, tile + 1, chunk_copy(1 - slot))

    _wait_runs(tab_ref, tile, chunk_copy(slot))
    cols = jnp.concatenate([rw_ref[...], jnp.zeros((LANES - SUBLANES, m), F32)], axis=0).T
    kl = lax.broadcasted_iota(I32, (m, lc), 1)
    weights = jnp.where(kl == cols[:, 2:3].astype(I32), cols[:, 0:1],
                        jnp.where(kl == cols[:, 3:4].astype(I32), cols[:, 1:2], 0.0))
    moe = jnp.dot(weights.astype(BF16), yl_s[slot].astype(BF16), preferred_element_type=F32)
    y = alpha * x1_ref[...] + (1.0 + gf_ref[...]) * moe.reshape(bb, ts, d)
    o_ref[...] = _layernorm_rows(y, g2_ref[...], b2_ref[...])


def _combine(tab_flat, x1, gf, rw, ys, g2, b2, *, bb, ts, alpha):
    b, s, d = x1.shape
    m = bb * ts
    ns = s // ts
    grid_spec = pltpu.PrefetchScalarGridSpec(
        num_scalar_prefetch=1,
        grid=(b // bb, ns),
        in_specs=[pl.BlockSpec((bb, ts, d), lambda i, j, tab: (i, j, 0)),
                  pl.BlockSpec((bb, 1, d), lambda i, j, tab: (i, 0, 0)),
                  pl.BlockSpec((SUBLANES, m), lambda i, j, tab: (0, i * ns + j)),
                  pl.BlockSpec(memory_space=pl.ANY),
                  pl.BlockSpec((1, d), lambda i, j, tab: (0, 0)),
                  pl.BlockSpec((1, d), lambda i, j, tab: (0, 0))],
        out_specs=pl.BlockSpec((bb, ts, d), lambda i, j, tab: (i, j, 0)),
        scratch_shapes=[pltpu.VMEM((2, _local_rows(m), d), SORTED), pltpu.SemaphoreType.DMA((2,))],
    )
    return pl.pallas_call(
        functools.partial(_combine_kernel, bb=bb, ts=ts, alpha=alpha),
        grid_spec=grid_spec,
        out_shape=jax.ShapeDtypeStruct((b, s, d), F32),
        compiler_params=_params(("arbitrary", "arbitrary"), VMEM_LIMIT),
    )(tab_flat, x1, gf, rw, ys, g2, b2)


def _layer(x, mod, p, lam_init, alpha, conv_buf, c0, n0, m0, paged, *, sample):
    b, s, d = x.shape
    sh_a, sc_a, g_a, sh_f, sc_f, g_f = mod
    if sample:
        bb, ts, act = b, s, F32
    else:
        bb, ts, act = 1, min(s, 512), BF16
    ca, q, k, v, om, g, qd, kd, vd, kdn, vdn, conv_new = _proj(
        x, sc_a, sh_a, p["wa"], p["wg"], p["wb"], p["bg"], conv_buf, p["w_conv"], p["b_conv"], p["wqk"],
        bb=bb, ts=ts, act=act)

    if sample:
        chunk = LANES
        pad_rows = lambda a: jnp.pad(a, ((0, 0), (0, chunk - s), (0, 0)))
        lane = jnp.arange(GATE_LANES)
        gate_pad = jnp.where(lane < ML_HEADS, -jnp.inf, jnp.where(lane < 2 * ML_HEADS, jnp.inf, 0.0)).astype(F32)
        g_in = jnp.concatenate([g, jnp.broadcast_to(gate_pad, (b, chunk - s, GATE_LANES))], axis=1)
        hm, c1, n1, m1 = _mlstm(pad_rows(q), pad_rows(k), pad_rows(v), g_in, pad_rows(ca), pad_rows(om),
                                c0, n0, m0, p["gn_m"], p["skip_m"], chunk=chunk, act=act)
        hm = hm[:, :s]
        cache_k, cache_v, page_table = paged
        ad = _dattn_sample(qd, kdn, vdn, cache_k, cache_v, page_table, p["lams"], p["subln_g"],
                           pps=min(PAGES_PER_STEP, page_table.shape[1]), lam_init=lam_init)
    else:
        hm, c1, n1, m1 = _mlstm(q, k, v, g, ca, om, c0, n0, m0, p["gn_m"], p["skip_m"],
                                chunk=min(s, MLSTM_CHUNK), act=act)
        ad = _dattn_prompt(qd, kd, vd, p["lams"], p["subln_g"].reshape(DA_V_DIM, 1), tq=min(s, ATTN_TQ),
                           tk=min(s, ATTN_TK), lam_init=lam_init, act=act)

    t = b * s
    x1, h2, ri, rw, cnt, cb, cn = _mix(hm, ad, x, g_a, sc_f, sh_f, p["wo1"], p["wo2"], p["ln1_g"], p["ln1_b"],
                                       p["wrh"], p["wrl"], p["brt"], bb=bb, ts=ts, alpha=alpha)
    row_tile = SAMPLE_ROW_TILE if sample else ROW_TILE
    n_tok_tiles = t // (bb * ts)
    n_tiles = -(-(2 * t + N_EXPERTS * (RUN_ALIGN - 1) * n_tok_tiles) // row_tile) + N_EXPERTS
    n_tiles_pad = -(-n_tiles // LANES) * LANES
    tab, tiles, tail = _plan(cnt, cb, cn, row_tile=row_tile, n_tiles_pad=n_tiles_pad)
    tab_flat = tab[::SUBLANES].reshape(-1)
    n_used = tiles[1, 0:1]
    xs = _scatter(tab_flat, tail[0], n_used, ri, h2, m=bb * ts, row_tile=row_tile, n_tiles=n_tiles)
    ys = _experts(tiles[0, :n_tiles], n_used, xs, p["w_gate"], p["w_up"], p["w_down"], row_tile=row_tile)
    y = _combine(tab_flat, x1, g_f, rw, ys, p["ln2_g"], p["ln2_b"], bb=bb, ts=ts, alpha=alpha)
    return y, kdn, vdn, c1, n1, m1, conv_new


def _layer_params(l, w_in, w_conv, b_conv, w_mq, w_mk, b_i, b_f, gn_m, skip_m, lam_q1, lam_k1, lam_q2, lam_k2,
                  subln_g, w_out, ln1_g, ln1_b, w_rg, b_rg, w_re, b_re, w_gate, w_up, w_down, ln2_g, ln2_b):
    w3 = 3 * ML_WIDTH
    n_gate = 2 * ML_HEADS
    wi = w_in[l]
    d = wi.shape[0]
    wg = jnp.zeros((d, GATE_LANES), F32).at[:, :n_gate].set(wi[:, w3:w3 + n_gate])
    bg = jnp.zeros((1, GATE_LANES), F32).at[0, :ML_HEADS].set(b_i[l]).at[0, ML_HEADS:n_gate].set(b_f[l])
    wrt = jnp.zeros((ROUTER_ROWS, d), F32).at[:N_GROUPS].set(w_rg[l].T).at[8:8 + N_EXPERTS].set(w_re[l].T)
    brt = jnp.zeros((ROUTER_ROWS, 1), F32).at[:N_GROUPS, 0].set(b_rg[l]).at[8:8 + N_EXPERTS, 0].set(b_re[l])
    return {
        "wa": wi[:, :w3].astype(BF16),
        "wg": wg.astype(BF16),
        "wb": wi[:, w3 + n_gate:].astype(BF16),
        "bg": bg,
        "w_conv": w_conv[l],
        "b_conv": b_conv[l][None, :],
        "wqk": jnp.concatenate([w_mq[l], w_mk[l]], axis=-1).astype(BF16),
        "gn_m": gn_m[l], "skip_m": skip_m[l],
        "lams": (lam_q1[l][None, :], lam_k1[l][None, :], lam_q2[l][None, :], lam_k2[l][None, :]),
        "subln_g": subln_g[l][None, :],
        "wo1": w_out[l][:ML_WIDTH].astype(BF16),
        "wo2": w_out[l][ML_WIDTH:].astype(BF16),
        "ln1_g": ln1_g[l][None, :], "ln1_b": ln1_b[l][None, :],
        "wrh": wrt.astype(BF16), "wrl": (wrt - wrt.astype(BF16).astype(F32)).astype(BF16), "brt": brt,
        "w_gate": w_gate[l], "w_up": w_up[l], "w_down": w_down[l],
        "ln2_g": ln2_g[l][None, :], "ln2_b": ln2_b[l][None, :],
    }


def kernel(x_prompt, x_sample, cache_k, cache_v, state_C, state_n, state_m, state_conv, page_table, c_prompt, c_sample, w_ada, b_ada, w_in, w_conv, b_conv, w_mq, w_mk, b_i, b_f, gn_m, skip_m, lam_q1, lam_k1, lam_q2, lam_k2, subln_g, w_out, ln1_g, ln1_b, w_rg, b_rg, w_re, b_re, w_gate, w_up, w_down, ln2_g, ln2_b):
    depth = w_ada.shape[0]
    bp, sp, d = x_prompt.shape
    bs, ss, _ = x_sample.shape
    alpha = (2 * depth) ** 0.25
    yp, ys = x_prompt, x_sample
    outs_p = [[] for _ in range(6)]
    outs_s = [[] for _ in range(6)]
    c_all = jnp.concatenate([c_prompt, c_sample], axis=0)
    for l in range(depth):
        p = _layer_params(l, w_in, w_conv, b_conv, w_mq, w_mk, b_i, b_f, gn_m, skip_m, lam_q1, lam_k1, lam_q2,
                          lam_k2, subln_g, w_out, ln1_g, ln1_b, w_rg, b_rg, w_re, b_re, w_gate, w_up, w_down,
                          ln2_g, ln2_b)
        lam_init = 0.8 - 0.6 * math.exp(-0.3 * l)
        mod = _ada(c_all, w_ada[l], b_ada[l][None, :])
        mod_p = tuple(mod[:bp, None, i * d:(i + 1) * d] for i in range(6))
        mod_s = tuple(mod[bp:, None, i * d:(i + 1) * d] for i in range(6))
        h, hd = ML_HEADS, ML_HEAD_DIM
        res_p = _layer(yp, mod_p, p, lam_init, alpha,
                       jnp.zeros((bp, CONV_W - 1, ML_WIDTH), F32), jnp.zeros((bp, h, hd, hd), F32),
                       jnp.zeros((bp, h, hd), F32), jnp.zeros((bp, 1, h), F32), None, sample=False)
        n_pool, page = cache_k.shape[1], cache_k.shape[2]
        paged = (cache_k[l].reshape(n_pool, page * DA_HEADS, DA_V_DIM),
                 cache_v[l].reshape(n_pool, page * DA_HEADS, DA_V_DIM), page_table)
        res_s = _layer(ys, mod_s, p, lam_init, alpha, state_conv[l], state_C[l], state_n[l],
                       state_m[l][:, None, :], paged, sample=True)
        yp, ys = res_p[0], res_s[0]
        for outs, res, nb, ns in ((outs_p, res_p, bp, sp), (outs_s, res_s, bs, ss)):
            outs[0].append(res[1].reshape(nb, ns, DA_HEADS, 2 * DA_HEAD_DIM))
            outs[1].append(res[2].reshape(nb, ns, DA_HEADS, DA_V_DIM))
            outs[2].append(res[3])
            outs[3].append(res[4])
            outs[4].append(res[5].reshape(nb, h))
            outs[5].append(res[6])
    return (yp, ys, *(jnp.stack(o) for o in outs_p), *(jnp.stack(o) for o in outs_s))
```
